```python
import jax, jax.numpy as jnp
from jax import lax
import numpy as np

D_MODEL = 1024
BATCH = 4
SEQ = 8192
DEPTH = 4

GRID_W = 64
CTX_LEN = 256

D_FF = 2816

GLA_HEADS = 4
GLA_DK = 64
GLA_DV = 128
GLA_QK = GLA_HEADS * GLA_DK
GLA_V = GLA_HEADS * GLA_DV
GLA_RANK = 16
GLA_TAU = 16.0
GLA_CHUNK = 64

MLA_HEADS = 4
MLA_D_NOPE = 128
MLA_D_ROPE = 64
MLA_D_V = 128
MLA_D_CQ = 384
MLA_D_CKV = 128
MLA_SCALE = (MLA_D_NOPE + MLA_D_ROPE) ** -0.5

NA_HEADS = 16
NA_HEAD_DIM = 64
NA_KH = 8
NA_KW = 16

ROPE_BASE = 10000.0
Q_BLOCK = 128
EPS = 1e-6
N_EVEN = (DEPTH + 1) // 2
N_ODD = DEPTH // 2
ALPHA = (2 * DEPTH) ** 0.25
BETA = (8 * DEPTH) ** -0.25

EVEN_SIZES = (GLA_QK, GLA_QK, GLA_V, GLA_V, GLA_RANK, GLA_RANK, MLA_D_CQ, MLA_D_CKV, MLA_D_ROPE)
EVEN_OUT_IN = GLA_V + MLA_HEADS * MLA_D_V
NA_WIDTH = NA_HEADS * NA_HEAD_DIM

kernel_name = 'hybrid_gla_mla_natten_macaron_deepnorm'


def layer_norm(x):
    xf = x.astype(jnp.float32)
    mu = jnp.mean(xf, axis=-1, keepdims=True)
    var = jnp.mean(jnp.square(xf - mu), axis=-1, keepdims=True)
    return ((xf - mu) * lax.rsqrt(var + EPS)).astype(x.dtype)


def rms_norm(x, g):
    xf = x.astype(jnp.float32)
    y = xf * lax.rsqrt(jnp.mean(jnp.square(xf), axis=-1, keepdims=True) + EPS)
    return y.astype(x.dtype) * g


def modulate(x, shift, scale):
    return x * (1.0 + scale) + shift


def post_norm(x, y):
    return layer_norm(ALPHA * x + y)


def swiglu(h, w_in, w_out):
    gate, up = jnp.split(h @ w_in, 2, axis=-1)
    return (jax.nn.silu(gate) * up) @ w_out


def axial_rope(n_tokens):
    t = jnp.arange(n_tokens)
    row = (t // GRID_W).astype(jnp.float32)
    col = (t % GRID_W).astype(jnp.float32)
    n_freq = MLA_D_ROPE // 4
    inv = ROPE_BASE ** (-jnp.arange(n_freq, dtype=jnp.float32) / n_freq)
    ang = jnp.concatenate([row[:, None] * inv, col[:, None] * inv], axis=-1)
    return jnp.cos(ang), jnp.sin(ang)


def apply_rope(x, cos, sin):
    x2 = x.reshape(*x.shape[:-1], -1, 2).astype(jnp.float32)
    x0, x1 = x2[..., 0], x2[..., 1]
    out = jnp.stack([x0 * cos - x1 * sin, x0 * sin + x1 * cos], axis=-1)
    return out.reshape(x.shape).astype(x.dtype)


def dense_attention(q, k, v, scale):
    B, T, H, dk = q.shape
    nb = T // Q_BLOCK
    qb = q.reshape(B, nb, Q_BLOCK, H, dk).transpose(1, 0, 2, 3, 4)

    def block(qi):
        s = jnp.einsum('bqhd,bkhd->bhqk', qi, k).astype(jnp.float32) * scale
        p = jax.nn.softmax(s, axis=-1).astype(v.dtype)
        return jnp.einsum('bhqk,bkhd->bqhd', p, v)

    o = lax.map(block, qb)
    return o.transpose(1, 0, 2, 3, 4).reshape(B, T, H, v.shape[-1])


def gla_chunked(q, k, v, log_g, s0, include_diag):
    B, T, H, DK = q.shape
    DV = v.shape[-1]
    n = T // GLA_CHUNK

    def to_chunks(a):
        return a.reshape(B, n, GLA_CHUNK, H, a.shape[-1]).transpose(1, 0, 3, 2, 4)

    idx = jnp.arange(GLA_CHUNK)
    mask = (idx[:, None] >= idx[None, :]) if include_diag else (idx[:, None] > idx[None, :])

    def step(s, inp):
        qi, ki, vi, gi = inp
        b = jnp.cumsum(gi.astype(jnp.float32), axis=-2)
        diff = b[:, :, :, None, :] - b[:, :, None, :, :]
        decay = jnp.exp(jnp.where(mask[:, :, None], diff, -jnp.inf)).astype(qi.dtype)
        att = jnp.einsum('bhid,bhjd,bhijd->bhij', qi, ki, decay)
        o_intra = jnp.einsum('bhij,bhjv->bhiv', att, vi)
        o_inter = jnp.einsum('bhid,bhdv->bhiv', qi * jnp.exp(b).astype(qi.dtype), s)
        b_last = b[:, :, -1:, :]
        k_dec = ki * jnp.exp(b_last - b).astype(ki.dtype)
        s_new = s * jnp.exp(b_last[:, :, 0, :])[..., None].astype(s.dtype) + jnp.einsum('bhjd,bhjv->bhdv', k_dec, vi)
        return s_new, o_intra + o_inter

    s_fin, o = lax.scan(step, s0, (to_chunks(q), to_chunks(k), to_chunks(v), to_chunks(log_g)))
    return o.transpose(1, 0, 3, 2, 4).reshape(B, T, H, DV), s_fin


def gla_log_gate(lr, wg2, bg):
    return jax.nn.log_sigmoid((lr @ wg2 + bg).astype(jnp.float32)) / GLA_TAU


def gla_out(o, r, norm_g):
    B, T = o.shape[:2]
    return rms_norm(o, norm_g).reshape(B, T, GLA_V) * jax.nn.silu(r)


def even_project(h, w_in, wg2_f, bg_f, wg2_b, bg_b, q_norm_g, kv_norm_g, w_uq, w_ukv):
    B, T, _ = h.shape
    split_points = np.cumsum(EVEN_SIZES)[:-1].tolist()
    q_g, k_g, v_g, r_g, lr_f, lr_b, c_q, c_kv, k_r = jnp.split(h @ w_in, split_points, axis=-1)
    heads = lambda a, n: a.reshape(B, T, n, -1)
    gla = (heads(q_g, GLA_HEADS) * GLA_DK ** -0.5, heads(k_g, GLA_HEADS), heads(v_g, GLA_HEADS),
           heads(gla_log_gate(lr_f, wg2_f, bg_f), GLA_HEADS), heads(gla_log_gate(lr_b, wg2_b, bg_b), GLA_HEADS), r_g)
    q = heads(rms_norm(c_q, q_norm_g) @ w_uq, MLA_HEADS)
    kv = heads(rms_norm(c_kv, kv_norm_g) @ w_ukv, MLA_HEADS)
    mla = (q[..., :MLA_D_NOPE], q[..., MLA_D_NOPE:], kv[..., :MLA_D_NOPE], k_r, kv[..., MLA_D_NOPE:])
    return gla, mla


def even_mixer(hc, hl, cos, sin, w_in, wg2_f, bg_f, wg2_b, bg_b, gla_norm_g, q_norm_g, kv_norm_g,
               w_uq, w_ukv, w_out, with_ctx_out):
    B, S, _ = hl.shape
    args = (w_in, wg2_f, bg_f, wg2_b, bg_b, q_norm_g, kv_norm_g, w_uq, w_ukv)
    (gq_c, gk_c, gv_c, gf_c, gb_c, r_c), (qn_c, qr_c, kn_c, kr_c, v_c) = even_project(hc, *args)
    (gq_l, gk_l, gv_l, gf_l, gb_l, r_l), (qn_l, qr_l, kn_l, kr_l, v_l) = even_project(hl, *args)

    s0 = jnp.zeros((B, GLA_HEADS, GLA_DK, GLA_DV), hl.dtype)
    flip = lambda a: a[:, ::-1]
    o_cf, s_cf = gla_chunked(gq_c, gk_c, gv_c, gf_c, s0, True)
    o_cb, s_cb = gla_chunked(flip(gq_c), flip(gk_c), flip(gv_c), flip(gb_c), s0, False)
    o_lf, _ = gla_chunked(gq_l, gk_l, gv_l, gf_l, s_cf, True)
    o_lb, _ = gla_chunked(flip(gq_l), flip(gk_l), flip(gv_l), flip(gb_l), s_cb, False)
    gla_l = gla_out(o_lf + flip(o_lb), r_l, gla_norm_g)

    rope_shape = kn_l.shape[:-1] + (MLA_D_ROPE,)
    q_l = jnp.concatenate([qn_l, apply_rope(qr_l, cos[None, :, None], sin[None, :, None])], axis=-1)
    kr_l_rot = apply_rope(kr_l, cos[None], sin[None])
    k_l = jnp.concatenate([kn_l, jnp.broadcast_to(kr_l_rot[:, :, None], rope_shape)], axis=-1)
    k_c = jnp.concatenate([kn_c, jnp.broadcast_to(kr_c[:, :, None], kn_c.shape[:-1] + (MLA_D_ROPE,))], axis=-1)
    k_all = jnp.concatenate([k_c, k_l], axis=1)
    v_all = jnp.concatenate([v_c, v_l], axis=1)
    mla_l = dense_attention(q_l, k_all, v_all, MLA_SCALE).reshape(B, S, -1)

    yl = jnp.concatenate([gla_l, mla_l], axis=-1) @ w_out
    yc = None
    if with_ctx_out:
        Tc = hc.shape[1]
        gla_c = gla_out(o_cf + flip(o_cb), r_c, gla_norm_g)
        q_c = jnp.concatenate([qn_c, qr_c], axis=-1)
        mla_c = dense_attention(q_c, k_c, v_c, MLA_SCALE).reshape(B, Tc, -1)
        yc = jnp.concatenate([gla_c, mla_c], axis=-1) @ w_out
    return yc, yl


def neighbourhood_attention(q, k, v, k_ctx, v_ctx, rpb):
    B, S, H, d = q.shape
    rows = S // GRID_W
    kh = min(NA_KH, rows)
    scale = d ** -0.5
    qg = q.reshape(B, rows, GRID_W, H, d)
    kg = k.reshape(B, rows, GRID_W, H, d)
    vg = v.reshape(B, rows, GRID_W, H, d)
    col = jnp.arange(GRID_W)
    col_start = jnp.clip(col - NA_KW // 2, 0, GRID_W - NA_KW)
    col_idx = col_start[:, None] + jnp.arange(NA_KW)[None, :]
    bias_col = rpb[:, :, col_idx - col[:, None] + (NA_KW - 1)]

    def row_block(r):
        r_start = jnp.clip(r - kh // 2, 0, rows - kh)
        k_win = lax.dynamic_slice_in_dim(kg, r_start, kh, axis=1)[:, :, col_idx]
        v_win = lax.dynamic_slice_in_dim(vg, r_start, kh, axis=1)[:, :, col_idx]
        q_r = lax.dynamic_index_in_dim(qg, r, axis=1, keepdims=False)
        rel_row = r_start + jnp.arange(kh) - r + (NA_KH - 1)
        bias = jnp.take(bias_col, rel_row, axis=1).transpose(0, 2, 1, 3)
        s_nb = jnp.einsum('bchd,brcwhd->bhcrw', q_r, k_win).astype(jnp.float32) * scale + bias[None]
        s_cx = jnp.einsum('bchd,bkhd->bhck', q_r, k_ctx).astype(jnp.float32) * scale
        p = jax.nn.softmax(jnp.concatenate([s_nb.reshape(B, H, GRID_W, kh * NA_KW), s_cx], axis=-1), axis=-1)
        p = p.astype(v.dtype)
        p_nb = p[..., :kh * NA_KW].reshape(B, H, GRID_W, kh, NA_KW)
        p_cx = p[..., kh * NA_KW:]
        return (jnp.einsum('bhcrw,brcwhd->bchd', p_nb, v_win)
                + jnp.einsum('bhck,bkhd->bchd', p_cx, v_ctx))

    o = lax.map(row_block, jnp.arange(rows))
    return o.transpose(1, 0, 2, 3, 4).reshape(B, S, H, d)


def odd_mixer(hc, hl, w_in, rpb, w_out, with_ctx_out):
    def project(h):
        B, T, _ = h.shape
        q, k, v = jnp.split(h @ w_in, 3, axis=-1)
        return (q.reshape(B, T, NA_HEADS, NA_HEAD_DIM), k.reshape(B, T, NA_HEADS, NA_HEAD_DIM),
                v.reshape(B, T, NA_HEADS, NA_HEAD_DIM))

    B, S, _ = hl.shape
    q_c, k_c, v_c = project(hc)
    q_l, k_l, v_l = project(hl)
    yl = neighbourhood_attention(q_l, k_l, v_l, k_c, v_c, rpb).reshape(B, S, NA_WIDTH) @ w_out
    yc = None
    if with_ctx_out:
        o_c = dense_attention(q_c, k_c, v_c, NA_HEAD_DIM ** -0.5)
        yc = o_c.reshape(B, hc.shape[1], NA_WIDTH) @ w_out
    return yc, yl


def setup_inputs(seed: int = 0) -> dict:
    key = jax.random.key(seed)
    it = iter(list(jax.random.split(key, 32)))
    nrm = lambda shape, scale: jax.random.normal(next(it), shape, jnp.float32) * scale
    D = D_MODEL
    gate_base = jnp.tile(jnp.concatenate([jnp.zeros((2 * D,), jnp.float32), jnp.ones((D,), jnp.float32)]), 3)
    return {
        'x': nrm((BATCH, SEQ, D), 1.0),
        'c': nrm((BATCH, D), 1.0),
        'ctx': nrm((BATCH, CTX_LEN, D), 1.0),
        'c_ctx': nrm((D,), 1.0),
        'ada_w': nrm((DEPTH, D, 9 * D), 0.5 * D ** -0.5),
        'ada_b': gate_base + nrm((DEPTH, 9 * D), 0.02),
        'ffn1_w_in': nrm((DEPTH, D, 2 * D_FF), D ** -0.5),
        'ffn1_w_out': nrm((DEPTH, D_FF, D), BETA * D_FF ** -0.5),
        'ffn2_w_in': nrm((DEPTH, D, 2 * D_FF), D ** -0.5),
        'ffn2_w_out': nrm((DEPTH, D_FF, D), BETA * D_FF ** -0.5),
        'even_w_in': nrm((N_EVEN, D, sum(EVEN_SIZES)), D ** -0.5),
        'gla_wg2_f': nrm((N_EVEN, GLA_RANK, GLA_QK), GLA_RANK ** -0.5),
        'gla_bg_f': nrm((N_EVEN, GLA_QK), 0.5),
        'gla_wg2_b': nrm((N_EVEN, GLA_RANK, GLA_QK), GLA_RANK ** -0.5),
        'gla_bg_b': nrm((N_EVEN, GLA_QK), 0.5),
        'gla_norm_g': 1.0 + nrm((N_EVEN, GLA_DV), 0.02),
        'mla_q_norm_g': 1.0 + nrm((N_EVEN, MLA_D_CQ), 0.02),
        'mla_kv_norm_g': 1.0 + nrm((N_EVEN, MLA_D_CKV), 0.02),
        'mla_w_uq': nrm((N_EVEN, MLA_D_CQ, MLA_HEADS * (MLA_D_NOPE + MLA_D_ROPE)), MLA_D_CQ ** -0.5),
        'mla_w_ukv': nrm((N_EVEN, MLA_D_CKV, MLA_HEADS * (MLA_D_NOPE + MLA_D_V)), MLA_D_CKV ** -0.5),
        'even_w_out': nrm((N_EVEN, EVEN_OUT_IN, D), BETA * EVEN_OUT_IN ** -0.5),
        'na_w_in': nrm((N_ODD, D, 3 * NA_WIDTH), D ** -0.5),
        'na_rpb': nrm((N_ODD, NA_HEADS, 2 * NA_KH - 1, 2 * NA_KW - 1), 0.5),
        'na_w_out': nrm((N_ODD, NA_WIDTH, D), BETA * NA_WIDTH ** -0.5),
    }


def reference(x, c, ctx, c_ctx, ada_w, ada_b, ffn1_w_in, ffn1_w_out, ffn2_w_in, ffn2_w_out,
              even_w_in, gla_wg2_f, gla_bg_f, gla_wg2_b, gla_bg_b, gla_norm_g, mla_q_norm_g, mla_kv_norm_g,
              mla_w_uq, mla_w_ukv, even_w_out, na_w_in, na_rpb, na_w_out):
    S = x.shape[1]
    cos, sin = axial_rope(S)
    xl, xc = x, ctx
    for l in range(DEPTH):
        last = l == DEPTH - 1
        i = l // 2
        mod_l = jax.nn.silu(c) @ ada_w[l] + ada_b[l]
        mod_c = jax.nn.silu(c_ctx) @ ada_w[l] + ada_b[l]
        sh1, sc1, g1, sh2, sc2, g2, sh3, sc3, g3 = [m[:, None, :] for m in jnp.split(mod_l, 9, axis=-1)]
        csh1, csc1, cg1, csh2, csc2, cg2, csh3, csc3, cg3 = jnp.split(mod_c, 9)

        xl = post_norm(xl, 0.5 * g1 * swiglu(modulate(xl, sh1, sc1), ffn1_w_in[l], ffn1_w_out[l]))
        xc = post_norm(xc, 0.5 * cg1 * swiglu(modulate(xc, csh1, csc1), ffn1_w_in[l], ffn1_w_out[l]))

        hl = modulate(xl, sh2, sc2)
        hc = modulate(xc, csh2, csc2)
        if l % 2 == 0:
            yc, yl = even_mixer(hc, hl, cos, sin, even_w_in[i], gla_wg2_f[i], gla_bg_f[i], gla_wg2_b[i], gla_bg_b[i],
                                gla_norm_g[i], mla_q_norm_g[i], mla_kv_norm_g[i], mla_w_uq[i], mla_w_ukv[i],
                                even_w_out[i], not last)
        else:
            yc, yl = odd_mixer(hc, hl, na_w_in[i], na_rpb[i], na_w_out[i], not last)
        xl = post_norm(xl, g2 * yl)

        xl = post_norm(xl, 0.5 * g3 * swiglu(modulate(xl, sh3, sc3), ffn2_w_in[l], ffn2_w_out[l]))
        if not last:
            xc = post_norm(xc, cg2 * yc)
            xc = post_norm(xc, 0.5 * cg3 * swiglu(modulate(xc, csh3, csc3), ffn2_w_in[l], ffn2_w_out[l]))
    return xl
```

```python
import functools

import numpy as np
import jax
import jax.numpy as jnp
from jax import lax
from jax.experimental import pallas as pl
from jax.experimental.pallas import tpu as pltpu

DEPTH = 4
GRID_W = 64
D_FF = 2816
GLA_HEADS = 4
GLA_DK = 64
GLA_DV = 128
GLA_RANK = 16
GLA_TAU = 16.0
MLA_HEADS = 4
MLA_D_NOPE = 128
MLA_D_ROPE = 64
MLA_D_V = 128
MLA_D_CQ = 384
MLA_D_CKV = 128
MLA_SCALE = (MLA_D_NOPE + MLA_D_ROPE) ** -0.5
NA_HEADS = 16
NA_HEAD_DIM = 64
NA_KH = 8
NA_KW = 16
ROPE_BASE = 10000.0
EPS = 1e-6
ALPHA = (2 * DEPTH) ** 0.25

LANES = 128
VMEM_LIMIT = 56 * 1024 * 1024

GLA_CHUNK = 64
NA_ROWS = 8
NA_UNIT = 4
MASK_NEG = -1e30

BF16 = jnp.bfloat16
F32 = jnp.float32


def _cparams(sem):
    return pltpu.CompilerParams(dimension_semantics=sem, vmem_limit_bytes=VMEM_LIMIT)


def _dot(a, b):
    return jnp.dot(a, b, preferred_element_type=F32)


def _dot_nt(a, b):
    return lax.dot_general(a, b, (((1,), (1,)), ((), ())), preferred_element_type=F32)


def _dot_tn(a, b):
    return lax.dot_general(a, b, (((0,), (0,)), ((), ())), preferred_element_type=F32)


def _silu(v):
    return v * (1.0 / (1.0 + jnp.exp(-v)))


def _row_select(is_ctx, lat_ref, ctx_ref):
    return jnp.where(is_ctx, ctx_ref[0], lat_ref[0])


def _is_ctx_rows(tile_idx, tm, n_latent):
    rows = tile_idx * tm + lax.broadcasted_iota(jnp.int32, (tm, 1), 0)
    return rows >= n_latent


def _modulate(x, is_ctx, sh_l, sh_c, sc_l, sc_c):
    shift = _row_select(is_ctx, sh_l, sh_c)
    scale = _row_select(is_ctx, sc_l, sc_c)
    return x * (1.0 + scale) + shift


def _post_norm(x, y, gate, coef):
    z = ALPHA * x + (coef * gate) * y
    mu = jnp.mean(z, axis=-1, keepdims=True)
    zc = z - mu
    var = jnp.mean(zc * zc, axis=-1, keepdims=True)
    return zc * lax.rsqrt(var + EPS)


def _mod_specs(n_batch, d, grid_rank):
    if grid_rank == 2:
        lat = pl.BlockSpec((1, 1, d), lambda b, i: (b, 0, 0))
        ctx = pl.BlockSpec((1, 1, d), lambda b, i: (n_batch, 0, 0))
    else:
        lat = pl.BlockSpec((1, 1, d), lambda b, i, j: (b, 0, 0))
        ctx = pl.BlockSpec((1, 1, d), lambda b, i, j: (n_batch, 0, 0))
    return lat, ctx


def _ada_kernel(c_ref, w_ref, b_ref, o_ref):
    a = _silu(c_ref[...]).astype(BF16)
    o_ref[0] = _dot(a, w_ref[0].astype(BF16)) + b_ref[0]


def _ada_modulation(cc, ada_w, ada_b):
    depth, d, n9 = ada_w.shape
    tn = n9 // 8
    return pl.pallas_call(
        _ada_kernel,
        grid=(depth, n9 // tn),
        in_specs=[pl.BlockSpec((8, d), lambda l, j: (0, 0)),
                  pl.BlockSpec((1, d, tn), lambda l, j: (l, 0, j)),
                  pl.BlockSpec((1, 1, tn), lambda l, j: (l, 0, j))],
        out_specs=pl.BlockSpec((1, 8, tn), lambda l, j: (l, 0, j)),
        out_shape=jax.ShapeDtypeStruct((depth, 8, n9), F32),
        compiler_params=_cparams(("arbitrary", "arbitrary")),
        name="ada_modulation",
    )(cc, ada_w, ada_b.reshape(depth, 1, n9))


def _ffn_kernel(x_ref, shl, shc, scl, scc, gl, gc, wg_ref, wu_ref, wo_ref, o_ref, h_ref, acc_ref,
                *, tm, n_latent):
    i = pl.program_id(1)
    j = pl.program_id(2)
    is_ctx = _is_ctx_rows(i, tm, n_latent)

    @pl.when(j == 0)
    def _():
        h_ref[...] = _modulate(x_ref[0], is_ctx, shl, shc, scl, scc).astype(BF16)
        acc_ref[...] = jnp.zeros_like(acc_ref)

    h = h_ref[...]
    gate = _dot(h, wg_ref[...])
    up = _dot(h, wu_ref[...])
    act = (_silu(gate) * up).astype(BF16)
    acc_ref[...] += _dot(act, wo_ref[...])

    @pl.when(j == pl.num_programs(2) - 1)
    def _():
        g = _row_select(is_ctx, gl, gc)
        o_ref[0] = _post_norm(x_ref[0], acc_ref[...], g, 0.5)


def _ffn(x, mods, w_in, w_out, *, n_latent, n_rows, tm, tf):
    n_batch, _, d = x.shape
    sh, sc, g = mods
    n_ff = w_out.shape[0] // tf
    lat, ctx = _mod_specs(n_batch, d, 3)
    return pl.pallas_call(
        functools.partial(_ffn_kernel, tm=tm, n_latent=n_latent),
        grid=(n_batch, n_rows // tm, n_ff),
        in_specs=[pl.BlockSpec((1, tm, d), lambda b, i, j: (b, i, 0)),
                  lat, ctx, lat, ctx, lat, ctx,
                  pl.BlockSpec((d, tf), lambda b, i, j: (0, j)),
                  pl.BlockSpec((d, tf), lambda b, i, j: (0, j + n_ff)),
                  pl.BlockSpec((tf, d), lambda b, i, j: (j, 0))],
        out_specs=pl.BlockSpec((1, tm, d), lambda b, i, j: (b, i, 0)),
        out_shape=jax.ShapeDtypeStruct((n_batch, n_rows, d), F32),
        scratch_shapes=[pltpu.VMEM((tm, d), BF16), pltpu.VMEM((tm, d), F32)],
        compiler_params=_cparams(("arbitrary", "arbitrary", "arbitrary")),
        name="ffn_postnorm",
    )(x, sh, sh, sc, sc, g, g, w_in, w_in, w_out)


def _even_proj_kernel(x_ref, shl, shc, scl, scc, w1_ref, wg_ref, bg_ref, qng_ref, kvng_ref,
                      wqm_ref, wqs_ref, wkv_ref, cb_ref, sb_ref,
                      gq_ref, gk_ref, gv_ref, r_ref, gf_ref, gb_ref, qm_ref, km_ref, vm_ref,
                      *, tm, n_latent):
    i = pl.program_id(1)
    is_ctx = _is_ctx_rows(i, tm, n_latent)
    h = _modulate(x_ref[0], is_ctx, shl, shc, scl, scc).astype(BF16)
    p = _dot(h, w1_ref[...])
    hp = GLA_HEADS * LANES
    gv_w = GLA_HEADS * GLA_DV
    o0 = 0
    gq_ref[0] = p[:, o0:o0 + hp] * (GLA_DK ** -0.5)
    o0 += hp
    gk_ref[0] = p[:, o0:o0 + hp]
    o0 += hp
    gv_ref[0] = p[:, o0:o0 + gv_w].astype(BF16)
    o0 += gv_w
    r_ref[0] = p[:, o0:o0 + gv_w]
    o0 += gv_w
    lr = p[:, o0:o0 + LANES].astype(BF16)
    o0 += LANES
    cq = p[:, o0:o0 + MLA_D_CQ]
    o0 += MLA_D_CQ
    ckv = p[:, o0:o0 + MLA_D_CKV]
    o0 += MLA_D_CKV
    kr = p[:, o0:o0 + LANES]
    o0 += LANES
    krs = p[:, o0:o0 + LANES]

    z = _dot(lr, wg_ref[...]) + bg_ref[...]
    logsig = jnp.minimum(z, 0.0) - jnp.log1p(jnp.exp(-jnp.abs(z)))
    lane = lax.broadcasted_iota(jnp.int32, (1, 2 * hp), 1)
    real = (lane & (LANES - 1)) < GLA_DK
    logg = jnp.where(real, logsig / GLA_TAU, 0.0)
    gf_ref[0] = logg[:, :hp]
    gb_ref[0] = logg[:, hp:]

    cb = cb_ref[...]
    sb = sb_ref[...]
    cqn = (cq * lax.rsqrt(jnp.mean(cq * cq, axis=-1, keepdims=True) + EPS) * qng_ref[...]).astype(BF16)
    qmain = _dot(cqn, wqm_ref[...])
    qswap = _dot(cqn, wqs_ref[...])
    ckn = (ckv * lax.rsqrt(jnp.mean(ckv * ckv, axis=-1, keepdims=True) + EPS) * kvng_ref[...]).astype(BF16)
    kv = _dot(ckn, wkv_ref[...])
    k_rope = (kr * cb + krs * sb).astype(BF16)
    for hd in range(MLA_HEADS):
        b0 = 2 * LANES * hd
        qm_ref[0, :, b0:b0 + LANES] = (qmain[:, b0:b0 + LANES] * MLA_SCALE).astype(BF16)
        q_rope = qmain[:, b0 + LANES:b0 + 2 * LANES] * cb + qswap[:, LANES * hd:LANES * (hd + 1)] * sb
        qm_ref[0, :, b0 + LANES:b0 + 2 * LANES] = (q_rope * MLA_SCALE).astype(BF16)
        km_ref[0, :, b0:b0 + LANES] = kv[:, LANES * hd:LANES * (hd + 1)].astype(BF16)
        km_ref[0, :, b0 + LANES:b0 + 2 * LANES] = k_rope
    vm_ref[0] = kv[:, MLA_HEADS * MLA_D_NOPE:].astype(BF16)


def _even_proj(x, mods, wts, rope_c, rope_s, *, n_latent, tm):
    n_batch, t, d = x.shape
    sh, sc = mods
    lat, ctx = _mod_specs(n_batch, d, 2)
    full = lambda a: pl.BlockSpec(a.shape, lambda b, i: (0,) * a.ndim)
    row = lambda w: pl.BlockSpec((1, tm, w), lambda b, i: (b, i, 0))
    hp = GLA_HEADS * LANES
    gv_w = GLA_HEADS * GLA_DV
    out_w = [(hp, F32), (hp, F32), (gv_w, BF16), (gv_w, F32), (hp, F32), (hp, F32),
             (MLA_HEADS * 2 * LANES, BF16), (MLA_HEADS * 2 * LANES, BF16), (MLA_HEADS * MLA_D_V, BF16)]
    return pl.pallas_call(
        functools.partial(_even_proj_kernel, tm=tm, n_latent=n_latent),
        grid=(n_batch, t // tm),
        in_specs=[row(d), lat, ctx, lat, ctx] + [full(w) for w in wts]
                 + [pl.BlockSpec((tm, LANES), lambda b, i: (i, 0))] * 2,
        out_specs=[row(w) for w, _ in out_w],
        out_shape=[jax.ShapeDtypeStruct((n_batch, t, w), dt) for w, dt in out_w],
        compiler_params=_cparams(("arbitrary", "arbitrary")),
        name="even_project",
    )(x, sh, sh, sc, sc, *wts, rope_c, rope_s)


def _gla_constants(chunk, reverse):
    n_lvl = int(np.log2(chunk))
    pos = np.arange(chunk)
    tri = (pos[None, :] <= pos[:, None]).astype(np.float32)
    cum = [tri]
    mask = []
    for lvl in range(n_lvl):
        size = chunk >> lvl
        half = size // 2
        start = (pos // size) * size
        cum.append(tri[start + half - 1])
        upper = (pos % size) >= half
        same = (pos[:, None] // size) == (pos[None, :] // size)
        mask.append((same & upper[:, None] & ~upper[None, :]).astype(np.float32))
    cum = np.stack(cum)
    mask = np.stack(mask)
    if reverse:
        cum = cum[:, ::-1, ::-1]
        mask = mask[:, ::-1, ::-1]
    return np.ascontiguousarray(cum).reshape(-1, chunk), np.ascontiguousarray(mask)


def _gla_direction(q_ref, k_ref, v_ref, g_ref, cum_ref, mask_ref, st_ref, o_ref, *, chunk, reverse):
    n_lvl = mask_ref.shape[0]
    allb = jnp.dot(cum_ref[...], g_ref[0], precision=lax.Precision.HIGHEST, preferred_element_type=F32)
    last = 0 if reverse else chunk - 1
    eye = (lax.broadcasted_iota(jnp.int32, (chunk, chunk), 0)
           == lax.broadcasted_iota(jnp.int32, (chunk, chunk), 1)).astype(F32)
    for hd in range(GLA_HEADS):
        sl = slice(hd * LANES, (hd + 1) * LANES)
        q = q_ref[0, :, sl]
        k = k_ref[0, :, sl]
        v = v_ref[0, :, sl]
        b = allb[0:chunk, sl]
        att = jnp.zeros((chunk, chunk), F32)
        for lvl in range(n_lvl):
            ref = allb[(lvl + 1) * chunk:(lvl + 2) * chunk, sl]
            qt = (q * jnp.exp(jnp.minimum(b - ref, 0.0))).astype(BF16)
            kt = (k * jnp.exp(jnp.minimum(ref - b, 0.0))).astype(BF16)
            att = att + _dot_nt(qt, kt) * mask_ref[lvl]
        if not reverse:
            att = att + _dot_nt(q.astype(BF16), k.astype(BF16)) * eye
        st = st_ref[hd]
        o = _dot(att.astype(BF16), v)
        o = o + _dot_nt((q * jnp.exp(b)).astype(BF16), st.astype(BF16))
        o_ref[0, :, sl] = o
        b_last = b[last:last + 1, :]
        kd = (k * jnp.exp(b_last - b)).astype(BF16)
        st_ref[hd] = st * jnp.exp(b_last) + _dot_tn(v, kd)


def _gla_kernel(qf, kf, vf, gf, qb, kb, vb, gb, cumf, maskf, cumb, maskb, of_ref, ob_ref, stf, stb,
                *, chunk):
    @pl.when(pl.program_id(1) == 0)
    def _():
        stf[...] = jnp.zeros_like(stf)
        stb[...] = jnp.zeros_like(stb)

    _gla_direction(qf, kf, vf, gf, cumf, maskf, stf, of_ref, chunk=chunk, reverse=False)
    _gla_direction(qb, kb, vb, gb, cumb, maskb, stb, ob_ref, chunk=chunk, reverse=True)


def _gla(gq, gk, gv, gf, gb, *, n_latent, chunk):
    n_batch, t, hp = gq.shape
    n_lat = n_latent // chunk
    n_ctx = (t - n_latent) // chunk
    n_steps = n_lat + n_ctx

    def fwd(b, s):
        return (b, jnp.where(s < n_ctx, n_lat + s, s - n_ctx), 0)

    def bwd(b, s):
        return (b, n_steps - 1 - s, 0)

    cumf, maskf = _gla_constants(chunk, False)
    cumb, maskb = _gla_constants(chunk, True)
    consts = [jnp.asarray(a) for a in (cumf, maskf, cumb, maskb)]
    full = lambda a: pl.BlockSpec(a.shape, lambda b, s: (0,) * a.ndim)
    blk = lambda w, im: pl.BlockSpec((1, chunk, w), im)
    gv_w = gv.shape[-1]
    return pl.pallas_call(
        functools.partial(_gla_kernel, chunk=chunk),
        grid=(n_batch, n_steps),
        in_specs=[blk(hp, fwd), blk(hp, fwd), blk(gv_w, fwd), blk(hp, fwd),
                  blk(hp, bwd), blk(hp, bwd), blk(gv_w, bwd), blk(hp, bwd)] + [full(a) for a in consts],
        out_specs=[blk(gv_w, fwd), blk(gv_w, bwd)],
        out_shape=[jax.ShapeDtypeStruct((n_batch, t, gv_w), F32)] * 2,
        scratch_shapes=[pltpu.VMEM((GLA_HEADS, GLA_DV, LANES), F32)] * 2,
        compiler_params=_cparams(("arbitrary", "arbitrary")),
        name="gla_bidirectional",
    )(gq, gk, gv, gf, gq, gk, gv, gb, *consts)


def _flash_kernel(q_ref, k_ref, v_ref, o_ref, m_ref, l_ref, acc_ref):
    j = pl.program_id(3)

    @pl.when(j == 0)
    def _():
        m_ref[...] = jnp.full_like(m_ref, -jnp.inf)
        l_ref[...] = jnp.zeros_like(l_ref)
        acc_ref[...] = jnp.zeros_like(acc_ref)

    s = _dot_nt(q_ref[0], k_ref[0])
    m_prev = m_ref[...]
    m_new = jnp.maximum(m_prev, jnp.max(s, axis=-1, keepdims=True))
    alpha = jnp.exp(m_prev - m_new)
    p = jnp.exp(s - m_new)
    l_ref[...] = alpha * l_ref[...] + jnp.sum(p, axis=-1, keepdims=True)
    acc_ref[...] = alpha * acc_ref[...] + _dot(p.astype(BF16), v_ref[0])
    m_ref[...] = m_new

    @pl.when(j == pl.num_programs(3) - 1)
    def _():
        o_ref[0] = (acc_ref[...] / l_ref[...]).astype(o_ref.dtype)


def _flash(q, k, v, prev_out, *, n_heads, dq, dv, tq, tk, q_blk0, n_q, k_blk0, n_k):
    n_batch, t, _ = q.shape
    in_specs = [pl.BlockSpec((1, tq, dq), lambda b, h, i, j: (b, q_blk0 + i, h)),
                pl.BlockSpec((1, tk, dq), lambda b, h, i, j: (b, k_blk0 + j, h)),
                pl.BlockSpec((1, tk, dv), lambda b, h, i, j: (b, k_blk0 + j, h))]
    args = [q, k, v]
    aliases = {}
    if prev_out is not None:
        in_specs.append(pl.BlockSpec(memory_space=pl.ANY))
        args.append(prev_out)
        aliases = {3: 0}
    kern = _flash_kernel if prev_out is None else (lambda q_, k_, v_, _prev, *rest: _flash_kernel(q_, k_, v_, *rest))
    return pl.pallas_call(
        kern,
        grid=(n_batch, n_heads, n_q, n_k),
        in_specs=in_specs,
        out_specs=pl.BlockSpec((1, tq, dv), lambda b, h, i, j: (b, q_blk0 + i, h)),
        out_shape=jax.ShapeDtypeStruct((n_batch, t, n_heads * dv), BF16),
        scratch_shapes=[pltpu.VMEM((tq, 1), F32), pltpu.VMEM((tq, 1), F32), pltpu.VMEM((tq, dv), F32)],
        input_output_aliases=aliases,
        compiler_params=_cparams(("arbitrary",) * 4),
        name="flash_attention",
    )(*args)


def _even_out_kernel(x_ref, of_ref, ob_ref, r_ref, ng_ref, mo_ref, wa_ref, wb_ref, gl, gc, o_ref,
                     *, tm, n_latent):
    i = pl.program_id(1)
    is_ctx = _is_ctx_rows(i, tm, n_latent)
    o = of_ref[0] + ob_ref[0]
    r = r_ref[0]
    parts = []
    for hd in range(GLA_HEADS):
        sl = slice(hd * GLA_DV, (hd + 1) * GLA_DV)
        oh = o[:, sl]
        y = oh * lax.rsqrt(jnp.mean(oh * oh, axis=-1, keepdims=True) + EPS) * ng_ref[...]
        parts.append((y * _silu(r[:, sl])).astype(BF16))
    a = jnp.concatenate(parts, axis=-1)
    y = _dot(a, wa_ref[...]) + _dot(mo_ref[0], wb_ref[...])
    o_ref[0] = _post_norm(x_ref[0], y, _row_select(is_ctx, gl, gc), 1.0)


def _even_out(x, o_f, o_b, r, norm_g, mla_o, wa, wb, gate, *, n_latent, n_rows, tm):
    n_batch, _, d = x.shape
    lat, ctx = _mod_specs(n_batch, d, 2)
    row = lambda w: pl.BlockSpec((1, tm, w), lambda b, i: (b, i, 0))
    full = lambda a: pl.BlockSpec(a.shape, lambda b, i: (0,) * a.ndim)
    gv_w = o_f.shape[-1]
    return pl.pallas_call(
        functools.partial(_even_out_kernel, tm=tm, n_latent=n_latent),
        grid=(n_batch, n_rows // tm),
        in_specs=[row(d), row(gv_w), row(gv_w), row(gv_w), full(norm_g), row(mla_o.shape[-1]),
                  full(wa), full(wb), lat, ctx],
        out_specs=row(d),
        out_shape=jax.ShapeDtypeStruct((n_batch, n_rows, d), F32),
        compiler_params=_cparams(("arbitrary", "arbitrary")),
        name="even_out_postnorm",
    )(x, o_f, o_b, r, norm_g, mla_o, wa, wb, gate, gate)


def _qkv_kernel(x_ref, shl, shc, scl, scc, w_ref, o_ref, *, tm, n_latent, q_width, q_scale):
    i = pl.program_id(1)
    is_ctx = _is_ctx_rows(i, tm, n_latent)
    h = _modulate(x_ref[0], is_ctx, shl, shc, scl, scc).astype(BF16)
    p = _dot(h, w_ref[...])
    o_ref[0, :, :q_width] = (p[:, :q_width] * q_scale).astype(BF16)
    o_ref[0, :, q_width:] = p[:, q_width:].astype(BF16)


def _qkv_proj(x, mods, w, *, n_latent, tm):
    n_batch, t, d = x.shape
    sh, sc = mods
    lat, ctx = _mod_specs(n_batch, d, 2)
    n_out = w.shape[1]
    return pl.pallas_call(
        functools.partial(_qkv_kernel, tm=tm, n_latent=n_latent, q_width=n_out // 3,
                          q_scale=NA_HEAD_DIM ** -0.5),
        grid=(n_batch, t // tm),
        in_specs=[pl.BlockSpec((1, tm, d), lambda b, i: (b, i, 0)), lat, ctx, lat, ctx,
                  pl.BlockSpec(w.shape, lambda b, i: (0, 0))],
        out_specs=pl.BlockSpec((1, tm, n_out), lambda b, i: (b, i, 0)),
        out_shape=jax.ShapeDtypeStruct((n_batch, t, n_out), BF16),
        compiler_params=_cparams(("arbitrary", "arbitrary")),
        name="qkv_project",
    )(x, sh, sh, sc, sc, w)


def _pair_softmax_av(q, lane_lo, scores_fn, av_fn):
    outs = []
    for half in range(2):
        keep = lane_lo if half == 0 else jnp.logical_not(lane_lo)
        qh = jnp.where(keep, q, jnp.zeros_like(q))
        s_list = scores_fn(qh, half)
        m = s_list[0].max(axis=-1, keepdims=True)
        for s in s_list[1:]:
            m = jnp.maximum(m, s.max(axis=-1, keepdims=True))
        p_list = [jnp.exp(s - m) for s in s_list]
        denom = p_list[0].sum(axis=-1, keepdims=True)
        for p in p_list[1:]:
            denom = denom + p.sum(axis=-1, keepdims=True)
        outs.append(av_fn([p.astype(BF16) for p in p_list]) / denom)
    return jnp.where(lane_lo, outs[0], outs[1])


def _na_kernel(q_ref, k0, k1, k2, k3, v0, v1, v2, v3, kc_ref, vc_ref, bias_ref, o_ref):
    q = q_ref[0]
    k_win = jnp.concatenate([k0[0], k1[0], k2[0], k3[0]], axis=0)
    v_win = jnp.concatenate([v0[0], v1[0], v2[0], v3[0]], axis=0)
    k_ctx = kc_ref[0]
    v_ctx = vc_ref[0]
    lane_lo = lax.broadcasted_iota(jnp.int32, (1, LANES), 1) < NA_HEAD_DIM

    def scores(qh, half):
        return [_dot_nt(qh, k_win) + bias_ref[0, half], _dot_nt(qh, k_ctx)]

    def av(ps):
        return _dot(ps[0], v_win) + _dot(ps[1], v_ctx)

    o_ref[0] = _pair_softmax_av(q, lane_lo, scores, av).astype(o_ref.dtype)


def _na_bias_tables(rpb, rows):
    n_keyrows = NA_ROWS + 2 * NA_UNIT
    col = np.arange(GRID_W)
    col_start = np.clip(col - NA_KW // 2, 0, GRID_W - NA_KW)
    kc = np.arange(GRID_W)
    col_ok = (kc[None, :] >= col_start[:, None]) & (kc[None, :] < col_start[:, None] + NA_KW)
    rel_col = np.clip(kc[None, :] - col[:, None] + (NA_KW - 1), 0, 2 * NA_KW - 2)
    tables = []
    n_blocks = rows // NA_ROWS
    for blk in (0, min(1, n_blocks - 1), n_blocks - 1):
        r = blk * NA_ROWS + np.arange(NA_ROWS)
        kh = min(NA_KH, rows)
        r_start = np.clip(r - kh // 2, 0, rows - kh)
        key_row0 = np.clip(blk * NA_ROWS - NA_UNIT, 0, rows - n_keyrows)
        kr = key_row0 + np.arange(n_keyrows)
        row_ok = (kr[None, :] >= r_start[:, None]) & (kr[None, :] < r_start[:, None] + kh)
        rel_row = np.clip(kr[None, :] - r[:, None] + (NA_KH - 1), 0, 2 * NA_KH - 2)
        ok = row_ok[:, None, :, None] & col_ok[None, :, None, :]
        bias = rpb[:, rel_row[:, None, :, None], rel_col[None, :, None, :]]
        bias = jnp.where(jnp.asarray(ok)[None], bias, MASK_NEG)
        tables.append(bias.reshape(rpb.shape[0], NA_ROWS * GRID_W, n_keyrows * GRID_W))
    return jnp.stack(tables)


def _na_attention(qkv, bias, *, n_latent):
    n_batch, t, _ = qkv.shape
    rows = n_latent // GRID_W
    n_blocks = rows // NA_ROWS
    n_pairs = NA_HEADS // 2
    tq = NA_ROWS * GRID_W
    tu = NA_UNIT * GRID_W
    n_units = rows // NA_UNIT
    ctx_blk = n_latent // tu

    def unit(u):
        def im(p, b, rb):
            u0 = jnp.clip(rb * (NA_ROWS // NA_UNIT) - 1, 0, n_units - 4)
            return u0 + u
        return im

    k_specs = [pl.BlockSpec((1, tu, LANES), lambda p, b, rb, f=unit(u): (b, f(p, b, rb), n_pairs + p))
               for u in range(4)]
    v_specs = [pl.BlockSpec((1, tu, LANES), lambda p, b, rb, f=unit(u): (b, f(p, b, rb), 2 * n_pairs + p))
               for u in range(4)]

    def bias_im(p, b, rb):
        kind = jnp.where(rb == 0, 0, jnp.where(rb == n_blocks - 1, 2, 1))
        return (kind, p, 0, 0)

    return pl.pallas_call(
        _na_kernel,
        grid=(n_pairs, n_batch, n_blocks),
        in_specs=[pl.BlockSpec((1, tq, LANES), lambda p, b, rb: (b, rb, p))] + k_specs + v_specs
                 + [pl.BlockSpec((1, tu, LANES), lambda p, b, rb: (b, ctx_blk, n_pairs + p)),
                    pl.BlockSpec((1, tu, LANES), lambda p, b, rb: (b, ctx_blk, 2 * n_pairs + p)),
                    pl.BlockSpec((1, 2, tq, 4 * tu), bias_im)],
        out_specs=pl.BlockSpec((1, tq, LANES), lambda p, b, rb: (b, rb, p)),
        out_shape=jax.ShapeDtypeStruct((n_batch, t, NA_HEADS * NA_HEAD_DIM), BF16),
        compiler_params=_cparams(("arbitrary",) * 3),
        name="neighbourhood_attention",
    )(qkv, *([qkv] * 10), bias)


def _ctx_pair_kernel(q_ref, k_ref, v_ref, _prev, o_ref):
    k = k_ref[0]
    v = v_ref[0]
    lane_lo = lax.broadcasted_iota(jnp.int32, (1, LANES), 1) < NA_HEAD_DIM
    o_ref[0] = _pair_softmax_av(q_ref[0], lane_lo, lambda qh, half: [_dot_nt(qh, k)],
                                lambda ps: _dot(ps[0], v)).astype(o_ref.dtype)


def _ctx_pair_attention(qkv, prev_out, *, n_latent):
    n_batch, t, _ = qkv.shape
    n_ctx = t - n_latent
    blk = n_latent // n_ctx
    n_pairs = NA_HEADS // 2
    spec = lambda off: pl.BlockSpec((1, n_ctx, LANES), lambda b, p: (b, blk, off + p))
    return pl.pallas_call(
        _ctx_pair_kernel,
        grid=(n_batch, n_pairs),
        in_specs=[spec(0), spec(n_pairs), spec(2 * n_pairs), pl.BlockSpec(memory_space=pl.ANY)],
        out_specs=spec(0),
        out_shape=jax.ShapeDtypeStruct(prev_out.shape, prev_out.dtype),
        input_output_aliases={3: 0},
        compiler_params=_cparams(("arbitrary", "arbitrary")),
        name="context_pair_attention",
    )(qkv, qkv, qkv, prev_out)


def _odd_out_kernel(x_ref, a_ref, w_ref, gl, gc, o_ref, *, tm, n_latent):
    i = pl.program_id(1)
    is_ctx = _is_ctx_rows(i, tm, n_latent)
    y = _dot(a_ref[0], w_ref[...])
    o_ref[0] = _post_norm(x_ref[0], y, _row_select(is_ctx, gl, gc), 1.0)


def _odd_out(x, a, w, gate, *, n_latent, n_rows, tm):
    n_batch, _, d = x.shape
    lat, ctx = _mod_specs(n_batch, d, 2)
    row = lambda wd: pl.BlockSpec((1, tm, wd), lambda b, i: (b, i, 0))
    return pl.pallas_call(
        functools.partial(_odd_out_kernel, tm=tm, n_latent=n_latent),
        grid=(n_batch, n_rows // tm),
        in_specs=[row(d), row(a.shape[-1]), pl.BlockSpec(w.shape, lambda b, i: (0, 0)), lat, ctx],
        out_specs=row(d),
        out_shape=jax.ShapeDtypeStruct((n_batch, n_rows, d), F32),
        compiler_params=_cparams(("arbitrary", "arbitrary")),
        name="odd_out_postnorm",
    )(x, a, w, gate, gate)


def _pad_heads(w, n_heads, width):
    lead = w.shape[:-1]
    w = w.reshape(*lead, n_heads, -1)
    pad = [(0, 0)] * (w.ndim - 1) + [(0, width - w.shape[-1])]
    return jnp.pad(w, pad).reshape(*lead, n_heads * width)


def _even_weights(w_in, wg2_f, bg_f, wg2_b, bg_b, q_norm_g, kv_norm_g, w_uq, w_ukv):
    sizes = (GLA_HEADS * GLA_DK, GLA_HEADS * GLA_DK, GLA_HEADS * GLA_DV, GLA_HEADS * GLA_DV,
             GLA_RANK, GLA_RANK, MLA_D_CQ, MLA_D_CKV, MLA_D_ROPE)
    q_g, k_g, v_g, r_g, lr_f, lr_b, c_q, c_kv, k_r = jnp.split(w_in, np.cumsum(sizes)[:-1].tolist(), axis=-1)
    swap = np.arange(MLA_D_ROPE) ^ 1
    d = w_in.shape[0]
    zeros = lambda n: jnp.zeros((d, n), w_in.dtype)
    w1 = jnp.concatenate([
        _pad_heads(q_g, GLA_HEADS, LANES), _pad_heads(k_g, GLA_HEADS, LANES), v_g, r_g,
        lr_f, lr_b, zeros(LANES - 2 * GLA_RANK), c_q, c_kv,
        k_r, zeros(LANES - MLA_D_ROPE), k_r[:, swap], zeros(LANES - MLA_D_ROPE)], axis=-1).astype(BF16)
    hp = GLA_HEADS * LANES
    wg = jnp.zeros((LANES, 2 * hp), F32)
    wg = wg.at[:GLA_RANK, :hp].set(_pad_heads(wg2_f, GLA_HEADS, LANES))
    wg = wg.at[GLA_RANK:2 * GLA_RANK, hp:].set(_pad_heads(wg2_b, GLA_HEADS, LANES))
    bg = jnp.concatenate([_pad_heads(bg_f, GLA_HEADS, LANES), _pad_heads(bg_b, GLA_HEADS, LANES)])[None]
    uq = w_uq.reshape(MLA_D_CQ, MLA_HEADS, MLA_D_NOPE + MLA_D_ROPE)
    nope, rope = uq[..., :MLA_D_NOPE], uq[..., MLA_D_NOPE:]
    zr = jnp.zeros((MLA_D_CQ, MLA_HEADS, LANES - MLA_D_ROPE), w_uq.dtype)
    wq_main = jnp.concatenate([nope, rope, zr], axis=-1).reshape(MLA_D_CQ, -1).astype(BF16)
    wq_swap = jnp.concatenate([rope[..., swap], zr], axis=-1).reshape(MLA_D_CQ, -1).astype(BF16)
    ukv = w_ukv.reshape(MLA_D_CKV, MLA_HEADS, MLA_D_NOPE + MLA_D_V)
    wkv = jnp.concatenate([ukv[..., :MLA_D_NOPE].reshape(MLA_D_CKV, -1),
                           ukv[..., MLA_D_NOPE:].reshape(MLA_D_CKV, -1)], axis=-1).astype(BF16)
    return (w1, wg.astype(BF16), bg, q_norm_g[None], kv_norm_g[None], wq_main, wq_swap, wkv)


def _rope_tables(n_latent, n_ctx):
    t = jnp.arange(n_latent)
    row = (t // GRID_W).astype(F32)
    col = (t % GRID_W).astype(F32)
    n_freq = MLA_D_ROPE // 4
    inv = ROPE_BASE ** (-jnp.arange(n_freq, dtype=F32) / n_freq)
    ang = jnp.concatenate([row[:, None] * inv, col[:, None] * inv], axis=-1)
    cos = jnp.repeat(jnp.cos(ang), 2, axis=-1)
    sin = jnp.repeat(jnp.sin(ang), 2, axis=-1) * jnp.tile(jnp.array([-1.0, 1.0], F32), MLA_D_ROPE // 2)
    cos = jnp.concatenate([cos, jnp.ones((n_ctx, MLA_D_ROPE), F32)])
    sin = jnp.concatenate([sin, jnp.zeros((n_ctx, MLA_D_ROPE), F32)])
    pad = jnp.zeros((n_latent + n_ctx, LANES - MLA_D_ROPE), F32)
    return jnp.concatenate([cos, pad], axis=-1), jnp.concatenate([sin, pad], axis=-1)


def _row_tile(n_rows, target):
    best = 8
    for cand in range(8, target + 1, 8):
        if n_rows % cand == 0:
            best = cand
    return best


def kernel(x, c, ctx, c_ctx, ada_w, ada_b, ffn1_w_in, ffn1_w_out, ffn2_w_in, ffn2_w_out, even_w_in, gla_wg2_f, gla_bg_f, gla_wg2_b, gla_bg_b, gla_norm_g, mla_q_norm_g, mla_kv_norm_g, mla_w_uq, mla_w_ukv, even_w_out, na_w_in, na_rpb, na_w_out):
    n_batch, n_latent, d = x.shape
    n_ctx = ctx.shape[1]
    t_all = n_latent + n_ctx
    assert n_batch + 1 <= 8 and n_latent % (NA_ROWS * GRID_W) == 0 and n_ctx == NA_UNIT * GRID_W
    assert n_latent % GLA_CHUNK == 0 and n_ctx % GLA_CHUNK == 0

    xa = jnp.concatenate([x, ctx], axis=1)
    cc = jnp.concatenate([c, c_ctx[None], jnp.zeros((7 - n_batch, d), F32)], axis=0)
    mods = _ada_modulation(cc, ada_w, ada_b)
    mods = mods[:, :n_batch + 1].reshape(DEPTH, n_batch + 1, 9, 1, d).transpose(0, 2, 1, 3, 4)

    tm_ffn = _row_tile(t_all, 768)
    tm_ffn_lat = _row_tile(n_latent, 1024)
    tm_row = _row_tile(t_all, 256)
    tm_row_lat = _row_tile(n_latent, 256)
    tf = 256
    rope_c, rope_s = _rope_tables(n_latent, n_ctx)
    tq = _row_tile(n_latent, 1024)
    tk = _row_tile(t_all, 768)

    for l in range(DEPTH):
        last = l == DEPTH - 1
        i = l // 2
        m = mods[l]
        xa = _ffn(xa, (m[0], m[1], m[2]), ffn1_w_in[l].astype(BF16), ffn1_w_out[l].astype(BF16),
                  n_latent=n_latent, n_rows=t_all, tm=tm_ffn, tf=tf)
        n_rows = n_latent if last else t_all
        tm_out = tm_row_lat if last else tm_row
        if l % 2 == 0:
            wts = _even_weights(even_w_in[i], gla_wg2_f[i], gla_bg_f[i], gla_wg2_b[i], gla_bg_b[i],
                                mla_q_norm_g[i], mla_kv_norm_g[i], mla_w_uq[i], mla_w_ukv[i])
            gq, gk, gv, r, gf, gb, qm, km, vm = _even_proj(xa, (m[3], m[4]), wts, rope_c, rope_s,
                                                           n_latent=n_latent, tm=tm_row)
            o_f, o_b = _gla(gq, gk, gv, gf, gb, n_latent=n_latent, chunk=GLA_CHUNK)
            mla_o = _flash(qm, km, vm, None, n_heads=MLA_HEADS, dq=2 * LANES, dv=MLA_D_V, tq=tq, tk=tk,
                           q_blk0=0, n_q=n_latent // tq, k_blk0=0, n_k=t_all // tk)
            if not last:
                mla_o = _flash(qm, km, vm, mla_o, n_heads=MLA_HEADS, dq=2 * LANES, dv=MLA_D_V,
                               tq=n_ctx, tk=n_ctx, q_blk0=n_latent // n_ctx, n_q=1,
                               k_blk0=n_latent // n_ctx, n_k=1)
            w_out = even_w_out[i].astype(BF16)
            gv_w = GLA_HEADS * GLA_DV
            xa = _even_out(xa, o_f, o_b, r, gla_norm_g[i][None], mla_o, w_out[:gv_w], w_out[gv_w:], m[5],
                           n_latent=n_latent, n_rows=n_rows, tm=tm_out)
        else:
            qkv = _qkv_proj(xa, (m[3], m[4]), na_w_in[i].astype(BF16), n_latent=n_latent, tm=tm_row)
            bias = _na_bias_tables(na_rpb[i], n_latent // GRID_W)
            att = _na_attention(qkv, bias, n_latent=n_latent)
            if not last:
                att = _ctx_pair_attention(qkv, att, n_latent=n_latent)
            xa = _odd_out(xa, att, na_w_out[i].astype(BF16), m[5], n_latent=n_latent, n_rows=n_rows, tm=tm_out)
        xa = _ffn(xa, (m[6], m[7], m[8]), ffn2_w_in[l].astype(BF16), ffn2_w_out[l].astype(BF16),
                  n_latent=n_latent, n_rows=n_rows, tm=tm_ffn_lat if last else tm_ffn, tf=tf)
    return xa
```

```python
import functools

import numpy as np
import jax
import jax.numpy as jnp
from jax import lax
from jax.experimental import pallas as pl
from jax.experimental.pallas import tpu as pltpu

DEPTH = 4
GRID_W = 64
D_FF = 2816
GLA_HEADS = 4
GLA_DK = 64
GLA_DV = 128
GLA_RANK = 16
GLA_TAU = 16.0
MLA_HEADS = 4
MLA_D_NOPE = 128
MLA_D_ROPE = 64
MLA_D_V = 128
MLA_D_CQ = 384
MLA_D_CKV = 128
MLA_SCALE = (MLA_D_NOPE + MLA_D_ROPE) ** -0.5
LOG2_E = 1.4426950408889634
MLA_QSCALE = MLA_SCALE * LOG2_E
NA_HEADS = 16
NA_HEAD_DIM = 64
NA_KH = 8
NA_KW = 16
ROPE_BASE = 10000.0
EPS = 1e-6
ALPHA = (2 * DEPTH) ** 0.25

LANES = 128
VMEM_LIMIT = 56 * 1024 * 1024

GLA_CHUNK = 64
NA_ROWS = 8
NA_UNIT = 4
MASK_NEG = -1e30

BF16 = jnp.bfloat16
F32 = jnp.float32


def _cparams(sem):
    return pltpu.CompilerParams(dimension_semantics=sem, vmem_limit_bytes=VMEM_LIMIT)


def _dot(a, b):
    return jnp.dot(a, b, preferred_element_type=F32)


def _dot_nt(a, b):
    return lax.dot_general(a, b, (((1,), (1,)), ((), ())), preferred_element_type=F32)


def _dot_tn(a, b):
    return lax.dot_general(a, b, (((0,), (0,)), ((), ())), preferred_element_type=F32)


def _silu(v):
    return v * (1.0 / (1.0 + jnp.exp(-v)))


def _row_select(is_ctx, lat_ref, ctx_ref):
    return jnp.where(is_ctx, ctx_ref[0], lat_ref[0])


def _is_ctx_rows(tile_idx, tm, n_latent):
    rows = tile_idx * tm + lax.broadcasted_iota(jnp.int32, (tm, 1), 0)
    return rows >= n_latent


def _modulate(x, is_ctx, sh_l, sh_c, sc_l, sc_c):
    shift = _row_select(is_ctx, sh_l, sh_c)
    scale = _row_select(is_ctx, sc_l, sc_c)
    return x * (1.0 + scale) + shift


def _post_norm(x, y, gate, coef):
    z = ALPHA * x + (coef * gate) * y
    mu = jnp.mean(z, axis=-1, keepdims=True)
    zc = z - mu
    var = jnp.mean(zc * zc, axis=-1, keepdims=True)
    return zc * lax.rsqrt(var + EPS)


def _mod_specs(n_batch, d, grid_rank):
    if grid_rank == 2:
        lat = pl.BlockSpec((1, 1, d), lambda b, i: (b, 0, 0))
        ctx = pl.BlockSpec((1, 1, d), lambda b, i: (n_batch, 0, 0))
    else:
        lat = pl.BlockSpec((1, 1, d), lambda b, i, j: (b, 0, 0))
        ctx = pl.BlockSpec((1, 1, d), lambda b, i, j: (n_batch, 0, 0))
    return lat, ctx


def _ada_kernel(c_ref, w_ref, b_ref, o_ref):
    a = _silu(c_ref[...]).astype(BF16)
    o_ref[0] = _dot(a, w_ref[0].astype(BF16)) + b_ref[0]


def _ada_modulation(cc, ada_w, ada_b):
    depth, d, n9 = ada_w.shape
    tn = n9 // 8
    return pl.pallas_call(
        _ada_kernel,
        grid=(depth, n9 // tn),
        in_specs=[pl.BlockSpec((8, d), lambda l, j: (0, 0)),
                  pl.BlockSpec((1, d, tn), lambda l, j: (l, 0, j)),
                  pl.BlockSpec((1, 1, tn), lambda l, j: (l, 0, j))],
        out_specs=pl.BlockSpec((1, 8, tn), lambda l, j: (l, 0, j)),
        out_shape=jax.ShapeDtypeStruct((depth, 8, n9), F32),
        compiler_params=_cparams(("arbitrary", "arbitrary")),
        name="ada_modulation",
    )(cc, ada_w, ada_b.reshape(depth, 1, n9))


def _ffn_kernel(x_ref, shl, shc, scl, scc, gl, gc, wg_ref, wu_ref, wo_ref, o_ref, h_ref, acc_ref,
                *, tm, n_latent):
    i = pl.program_id(1)
    j = pl.program_id(2)
    is_ctx = _is_ctx_rows(i, tm, n_latent)

    @pl.when(j == 0)
    def _():
        h_ref[...] = _modulate(x_ref[0], is_ctx, shl, shc, scl, scc).astype(BF16)
        acc_ref[...] = jnp.zeros_like(acc_ref)

    h = h_ref[...]
    gate = _dot(h, wg_ref[...])
    up = _dot(h, wu_ref[...])
    act = (_silu(gate) * up).astype(BF16)
    acc_ref[...] += _dot(act, wo_ref[...])

    @pl.when(j == pl.num_programs(2) - 1)
    def _():
        g = _row_select(is_ctx, gl, gc)
        o_ref[0] = _post_norm(x_ref[0], acc_ref[...], g, 0.5)


def _ffn(x, mods, w_in, w_out, *, n_latent, n_rows, tm, tf):
    n_batch, _, d = x.shape
    sh, sc, g = mods
    n_ff = w_out.shape[0] // tf
    lat, ctx = _mod_specs(n_batch, d, 3)
    return pl.pallas_call(
        functools.partial(_ffn_kernel, tm=tm, n_latent=n_latent),
        grid=(n_batch, n_rows // tm, n_ff),
        in_specs=[pl.BlockSpec((1, tm, d), lambda b, i, j: (b, i, 0)),
                  lat, ctx, lat, ctx, lat, ctx,
                  pl.BlockSpec((d, tf), lambda b, i, j: (0, j)),
                  pl.BlockSpec((d, tf), lambda b, i, j: (0, j + n_ff)),
                  pl.BlockSpec((tf, d), lambda b, i, j: (j, 0))],
        out_specs=pl.BlockSpec((1, tm, d), lambda b, i, j: (b, i, 0)),
        out_shape=jax.ShapeDtypeStruct((n_batch, n_rows, d), F32),
        scratch_shapes=[pltpu.VMEM((tm, d), BF16), pltpu.VMEM((tm, d), F32)],
        compiler_params=_cparams(("arbitrary", "arbitrary", "arbitrary")),
        name="ffn_postnorm",
    )(x, sh, sh, sc, sc, g, g, w_in, w_in, w_out)


def _even_proj_kernel(x_ref, shl, shc, scl, scc, w1_ref, wg_ref, bg_ref, qng_ref, kvng_ref,
                      wqm_ref, wqs_ref, wkv_ref, cb_ref, sb_ref,
                      gq_ref, gk_ref, gv_ref, r_ref, gf_ref, gb_ref, qm_ref, km_ref, vm_ref,
                      *, tm, n_latent):
    i = pl.program_id(1)
    is_ctx = _is_ctx_rows(i, tm, n_latent)
    h = _modulate(x_ref[0], is_ctx, shl, shc, scl, scc).astype(BF16)
    p = _dot(h, w1_ref[...])
    hp = GLA_HEADS * GLA_DK
    gv_w = GLA_HEADS * GLA_DV
    o0 = 0
    gq_ref[0] = p[:, o0:o0 + hp] * (GLA_DK ** -0.5)
    o0 += hp
    gk_ref[0] = p[:, o0:o0 + hp]
    o0 += hp
    gv_ref[0] = p[:, o0:o0 + gv_w].astype(BF16)
    o0 += gv_w
    r_ref[0] = p[:, o0:o0 + gv_w]
    o0 += gv_w
    lr = p[:, o0:o0 + LANES].astype(BF16)
    o0 += LANES
    cq = p[:, o0:o0 + MLA_D_CQ]
    o0 += MLA_D_CQ
    ckv = p[:, o0:o0 + MLA_D_CKV]
    o0 += MLA_D_CKV
    kr = p[:, o0:o0 + LANES]
    o0 += LANES
    krs = p[:, o0:o0 + LANES]

    z = _dot(lr, wg_ref[...]) + bg_ref[...]
    logg = (jnp.minimum(z, 0.0) - jnp.log1p(jnp.exp(-jnp.abs(z)))) / GLA_TAU
    gf_ref[0] = logg[:, :hp]
    gb_ref[0] = logg[:, hp:]

    cb = cb_ref[...]
    sb = sb_ref[...]
    cqn = (cq * lax.rsqrt(jnp.mean(cq * cq, axis=-1, keepdims=True) + EPS) * qng_ref[...]).astype(BF16)
    qmain = _dot(cqn, wqm_ref[...])
    qswap = _dot(cqn, wqs_ref[...])
    ckn = (ckv * lax.rsqrt(jnp.mean(ckv * ckv, axis=-1, keepdims=True) + EPS) * kvng_ref[...]).astype(BF16)
    kv = _dot(ckn, wkv_ref[...])
    k_rope = (kr * cb + krs * sb).astype(BF16)
    ones_col = (lax.broadcasted_iota(jnp.int32, (1, LANES), 1) == 0).astype(BF16)
    for hd in range(MLA_HEADS):
        b0 = 2 * LANES * hd
        qm_ref[0, :, b0:b0 + LANES] = (qmain[:, b0:b0 + LANES] * MLA_QSCALE).astype(BF16)
        q_rope = qmain[:, b0 + LANES:b0 + 2 * LANES] * cb + qswap[:, LANES * hd:LANES * (hd + 1)] * sb
        qm_ref[0, :, b0 + LANES:b0 + 2 * LANES] = (q_rope * MLA_QSCALE).astype(BF16)
        km_ref[0, :, b0:b0 + LANES] = kv[:, LANES * hd:LANES * (hd + 1)].astype(BF16)
        km_ref[0, :, b0 + LANES:b0 + 2 * LANES] = k_rope
        v0 = MLA_HEADS * MLA_D_NOPE + MLA_D_V * hd
        vm_ref[0, :, b0:b0 + LANES] = kv[:, v0:v0 + MLA_D_V].astype(BF16)
        vm_ref[0, :, b0 + LANES:b0 + 2 * LANES] = jnp.broadcast_to(ones_col, (tm, LANES))


def _even_proj(x, mods, wts, rope_c, rope_s, *, n_latent, tm):
    n_batch, t, d = x.shape
    sh, sc = mods
    lat, ctx = _mod_specs(n_batch, d, 2)
    full = lambda a: pl.BlockSpec(a.shape, lambda b, i: (0,) * a.ndim)
    row = lambda w: pl.BlockSpec((1, tm, w), lambda b, i: (b, i, 0))
    hp = GLA_HEADS * GLA_DK
    gv_w = GLA_HEADS * GLA_DV
    out_w = [(hp, F32), (hp, F32), (gv_w, BF16), (gv_w, F32), (hp, F32), (hp, F32),
             (MLA_HEADS * 2 * LANES, BF16), (MLA_HEADS * 2 * LANES, BF16), (MLA_HEADS * 2 * MLA_D_V, BF16)]
    return pl.pallas_call(
        functools.partial(_even_proj_kernel, tm=tm, n_latent=n_latent),
        grid=(n_batch, t // tm),
        in_specs=[row(d), lat, ctx, lat, ctx] + [full(w) for w in wts]
                 + [pl.BlockSpec((tm, LANES), lambda b, i: (i, 0))] * 2,
        out_specs=[row(w) for w, _ in out_w],
        out_shape=[jax.ShapeDtypeStruct((n_batch, t, w), dt) for w, dt in out_w],
        compiler_params=_cparams(("arbitrary", "arbitrary")),
        name="even_project",
    )(x, sh, sh, sc, sc, *wts, rope_c, rope_s)


def _gla_constants(chunk, reverse):
    n_lvl = int(np.log2(chunk))
    pos = np.arange(chunk)
    src = pos[None, :]
    tri = src <= pos[:, None]
    cum, mask = [], []
    for lvl in range(n_lvl):
        size = chunk >> lvl
        half = size // 2
        ref = ((pos // size) * size + half - 1)[:, None]
        upper = (pos % size) >= half
        a_up = (src > ref) & (src <= pos[:, None])
        a_lo = (src > pos[:, None]) & (src <= ref)
        cum.append(np.where(upper[:, None], a_up, a_lo))
        same = (pos[:, None] // size) == (pos[None, :] // size)
        mask.append(same & upper[:, None] & ~upper[None, :])
    cum += [tri, ~tri]
    cum = np.stack(cum).astype(np.float32)
    mask = np.stack(mask).astype(np.float32)
    if reverse:
        cum = cum[:, ::-1, ::-1]
        mask = mask[:, ::-1, ::-1]
    return np.ascontiguousarray(cum).reshape(-1, chunk), np.ascontiguousarray(mask)


def _gla_direction(q_ref, k_ref, v_ref, g_ref, cum_ref, mask_ref, st_ref, o_ref, *, chunk, reverse):
    n_lvl = mask_ref.shape[0]
    g = g_ref[0]
    g_hi = g.astype(BF16)
    g_lo = (g - g_hi.astype(F32)).astype(BF16)
    cum = cum_ref[...]
    sums = _dot(cum, g_hi) + _dot(cum, g_lo)
    b_rows = slice(n_lvl * chunk, (n_lvl + 1) * chunk)
    rest_rows = slice((n_lvl + 1) * chunk, (n_lvl + 2) * chunk)
    last = 0 if reverse else chunk - 1
    eye = (lax.broadcasted_iota(jnp.int32, (chunk, chunk), 0)
           == lax.broadcasted_iota(jnp.int32, (chunk, chunk), 1)).astype(F32)
    lane_lo = lax.broadcasted_iota(jnp.int32, (1, LANES), 1) < GLA_DK
    heads_per_group = LANES // GLA_DK

    def head_only(x, hh):
        return jnp.where(lane_lo if hh == 0 else jnp.logical_not(lane_lo), x, jnp.zeros_like(x))

    for grp in range(GLA_HEADS // heads_per_group):
        sl = slice(grp * LANES, (grp + 1) * LANES)
        q = q_ref[0, :, sl]
        k = k_ref[0, :, sl]
        att = [jnp.zeros((chunk, chunk), F32) for _ in range(heads_per_group)]
        for lvl in range(n_lvl):
            w = jnp.exp(sums[lvl * chunk:(lvl + 1) * chunk, sl])
            qt = (q * w).astype(BF16)
            kt = (k * w).astype(BF16)
            for hh in range(heads_per_group):
                att[hh] = att[hh] + _dot_nt(head_only(qt, hh), kt) * mask_ref[lvl]
        if not reverse:
            qb, kb = q.astype(BF16), k.astype(BF16)
            for hh in range(heads_per_group):
                att[hh] = att[hh] + _dot_nt(head_only(qb, hh), kb) * eye
        b = sums[b_rows, sl]
        qe = (q * jnp.exp(b)).astype(BF16)
        kd = (k * jnp.exp(sums[rest_rows, sl])).astype(BF16)
        decay = jnp.exp(b[last:last + 1, :])
        for hh in range(heads_per_group):
            hd = grp * heads_per_group + hh
            vs = slice(hd * GLA_DV, (hd + 1) * GLA_DV)
            v = v_ref[0, :, vs]
            st = st_ref[hd]
            o = _dot(att[hh].astype(BF16), v) + _dot_nt(head_only(qe, hh), st.astype(BF16))
            o_ref[0, :, vs] = o
            st_ref[hd] = st * decay + _dot_tn(v, kd)


def _gla_kernel(qf, kf, vf, gf, qb, kb, vb, gb, cumf, maskf, cumb, maskb, of_ref, ob_ref, stf, stb,
                *, chunk):
    @pl.when(pl.program_id(1) == 0)
    def _():
        stf[...] = jnp.zeros_like(stf)
        stb[...] = jnp.zeros_like(stb)

    _gla_direction(qf, kf, vf, gf, cumf, maskf, stf, of_ref, chunk=chunk, reverse=False)
    _gla_direction(qb, kb, vb, gb, cumb, maskb, stb, ob_ref, chunk=chunk, reverse=True)


def _gla(gq, gk, gv, gf, gb, *, n_latent, chunk):
    n_batch, t, hp = gq.shape
    n_lat = n_latent // chunk
    n_ctx = (t - n_latent) // chunk
    n_steps = n_lat + n_ctx

    def fwd(b, s):
        return (b, jnp.where(s < n_ctx, n_lat + s, s - n_ctx), 0)

    def bwd(b, s):
        return (b, n_steps - 1 - s, 0)

    cumf, maskf = _gla_constants(chunk, False)
    cumb, maskb = _gla_constants(chunk, True)
    consts = [jnp.asarray(cumf, BF16), jnp.asarray(maskf), jnp.asarray(cumb, BF16), jnp.asarray(maskb)]
    full = lambda a: pl.BlockSpec(a.shape, lambda b, s: (0,) * a.ndim)
    blk = lambda w, im: pl.BlockSpec((1, chunk, w), im)
    gv_w = gv.shape[-1]
    return pl.pallas_call(
        functools.partial(_gla_kernel, chunk=chunk),
        grid=(n_batch, n_steps),
        in_specs=[blk(hp, fwd), blk(hp, fwd), blk(gv_w, fwd), blk(hp, fwd),
                  blk(hp, bwd), blk(hp, bwd), blk(gv_w, bwd), blk(hp, bwd)] + [full(a) for a in consts],
        out_specs=[blk(gv_w, fwd), blk(gv_w, bwd)],
        out_shape=[jax.ShapeDtypeStruct((n_batch, t, gv_w), F32)] * 2,
        scratch_shapes=[pltpu.VMEM((GLA_HEADS, GLA_DV, LANES), F32)] * 2,
        compiler_params=_cparams(("arbitrary", "arbitrary")),
        name="gla_bidirectional",
    )(gq, gk, gv, gf, gq, gk, gv, gb, *consts)


def _flash_kernel(q_ref, k_ref, v_ref, o_ref, m_ref, acc_ref, *, ck, dv):
    j = pl.program_id(3)
    n_sub = k_ref.shape[1] // ck

    @pl.when(j == 0)
    def _():
        m_ref[...] = jnp.full_like(m_ref, -jnp.inf)
        acc_ref[...] = jnp.zeros_like(acc_ref)

    q = q_ref[0]

    for c in range(n_sub):
        off = c * ck
        s = _dot_nt(q, k_ref[0, pl.ds(off, ck), :])
        m_prev = m_ref[...]
        m_new = jnp.maximum(m_prev, jnp.max(s, axis=-1, keepdims=True))
        alpha = jnp.exp2(m_prev - m_new)
        p = jnp.concatenate([jnp.exp2(s[:, u * LANES:(u + 1) * LANES] - m_new)
                             for u in range(ck // LANES)], axis=-1).astype(BF16)
        pv = _dot(p, v_ref[0, pl.ds(off, ck), :])
        acc_ref[...] = jnp.concatenate([alpha] * (acc_ref.shape[1] // LANES), axis=-1) * acc_ref[...] + pv
        m_ref[...] = m_new

    @pl.when(j == pl.num_programs(3) - 1)
    def _():
        acc = acc_ref[...]
        denom = jnp.sum(acc[:, dv:], axis=-1, keepdims=True)
        o_ref[0] = (acc[:, :dv] / denom).astype(o_ref.dtype)


def _flash(q, k, v, prev_out, *, n_heads, dq, dv, tq, tk, ck, q_blk0, n_q, k_blk0, n_k):
    n_batch, t, _ = q.shape
    in_specs = [pl.BlockSpec((1, tq, dq), lambda b, h, i, j: (b, q_blk0 + i, h)),
                pl.BlockSpec((1, tk, dq), lambda b, h, i, j: (b, k_blk0 + j, h)),
                pl.BlockSpec((1, tk, 2 * dv), lambda b, h, i, j: (b, k_blk0 + j, h))]
    args = [q, k, v]
    aliases = {}
    base = functools.partial(_flash_kernel, ck=ck, dv=dv)
    kern = base
    if prev_out is not None:
        in_specs.append(pl.BlockSpec(memory_space=pl.ANY))
        args.append(prev_out)
        aliases = {3: 0}
        kern = lambda q_, k_, v_, _prev, *rest: base(q_, k_, v_, *rest)
    return pl.pallas_call(
        kern,
        grid=(n_batch, n_heads, n_q, n_k),
        in_specs=in_specs,
        out_specs=pl.BlockSpec((1, tq, dv), lambda b, h, i, j: (b, q_blk0 + i, h)),
        out_shape=jax.ShapeDtypeStruct((n_batch, t, n_heads * dv), BF16),
        scratch_shapes=[pltpu.VMEM((tq, LANES), F32), pltpu.VMEM((tq, 2 * dv), F32)],
        input_output_aliases=aliases,
        compiler_params=_cparams(("arbitrary",) * 4),
        name="flash_attention",
    )(*args)


def _even_out_kernel(x_ref, of_ref, ob_ref, r_ref, ng_ref, mo_ref, wa_ref, wb_ref, gl, gc, o_ref,
                     *, tm, n_latent):
    i = pl.program_id(1)
    is_ctx = _is_ctx_rows(i, tm, n_latent)
    o = of_ref[0] + ob_ref[0]
    r = r_ref[0]
    parts = []
    for hd in range(GLA_HEADS):
        sl = slice(hd * GLA_DV, (hd + 1) * GLA_DV)
        oh = o[:, sl]
        y = oh * lax.rsqrt(jnp.mean(oh * oh, axis=-1, keepdims=True) + EPS) * ng_ref[...]
        parts.append((y * _silu(r[:, sl])).astype(BF16))
    a = jnp.concatenate(parts, axis=-1)
    y = _dot(a, wa_ref[...]) + _dot(mo_ref[0], wb_ref[...])
    o_ref[0] = _post_norm(x_ref[0], y, _row_select(is_ctx, gl, gc), 1.0)


def _even_out(x, o_f, o_b, r, norm_g, mla_o, wa, wb, gate, *, n_latent, n_rows, tm):
    n_batch, _, d = x.shape
    lat, ctx = _mod_specs(n_batch, d, 2)
    row = lambda w: pl.BlockSpec((1, tm, w), lambda b, i: (b, i, 0))
    full = lambda a: pl.BlockSpec(a.shape, lambda b, i: (0,) * a.ndim)
    gv_w = o_f.shape[-1]
    return pl.pallas_call(
        functools.partial(_even_out_kernel, tm=tm, n_latent=n_latent),
        grid=(n_batch, n_rows // tm),
        in_specs=[row(d), row(gv_w), row(gv_w), row(gv_w), full(norm_g), row(mla_o.shape[-1]),
                  full(wa), full(wb), lat, ctx],
        out_specs=row(d),
        out_shape=jax.ShapeDtypeStruct((n_batch, n_rows, d), F32),
        compiler_params=_cparams(("arbitrary", "arbitrary")),
        name="even_out_postnorm",
    )(x, o_f, o_b, r, norm_g, mla_o, wa, wb, gate, gate)


def _qkv_kernel(x_ref, shl, shc, scl, scc, w_ref, o_ref, *, tm, n_latent, q_width, q_scale):
    i = pl.program_id(1)
    is_ctx = _is_ctx_rows(i, tm, n_latent)
    h = _modulate(x_ref[0], is_ctx, shl, shc, scl, scc).astype(BF16)
    p = _dot(h, w_ref[...])
    o_ref[0, :, :q_width] = (p[:, :q_width] * q_scale).astype(BF16)
    o_ref[0, :, q_width:] = p[:, q_width:].astype(BF16)


def _qkv_proj(x, mods, w, *, n_latent, tm):
    n_batch, t, d = x.shape
    sh, sc = mods
    lat, ctx = _mod_specs(n_batch, d, 2)
    n_out = w.shape[1]
    return pl.pallas_call(
        functools.partial(_qkv_kernel, tm=tm, n_latent=n_latent, q_width=n_out // 3,
                          q_scale=NA_HEAD_DIM ** -0.5),
        grid=(n_batch, t // tm),
        in_specs=[pl.BlockSpec((1, tm, d), lambda b, i: (b, i, 0)), lat, ctx, lat, ctx,
                  pl.BlockSpec(w.shape, lambda b, i: (0, 0))],
        out_specs=pl.BlockSpec((1, tm, n_out), lambda b, i: (b, i, 0)),
        out_shape=jax.ShapeDtypeStruct((n_batch, t, n_out), BF16),
        compiler_params=_cparams(("arbitrary", "arbitrary")),
        name="qkv_project",
    )(x, sh, sh, sc, sc, w)


def _pair_softmax_av(q, lane_lo, scores_fn, av_fn):
    outs = []
    for half in range(2):
        keep = lane_lo if half == 0 else jnp.logical_not(lane_lo)
        qh = jnp.where(keep, q, jnp.zeros_like(q))
        s_list = scores_fn(qh, half)
        m = s_list[0].max(axis=-1, keepdims=True)
        for s in s_list[1:]:
            m = jnp.maximum(m, s.max(axis=-1, keepdims=True))
        p_list = [jnp.exp(s - m) for s in s_list]
        denom = p_list[0].sum(axis=-1, keepdims=True)
        for p in p_list[1:]:
            denom = denom + p.sum(axis=-1, keepdims=True)
        outs.append(av_fn([p.astype(BF16) for p in p_list]) / denom)
    return jnp.where(lane_lo, outs[0], outs[1])


def _na_kernel(q_ref, k0, k1, k2, k3, v0, v1, v2, v3, kc_ref, vc_ref, bias_ref, o_ref):
    q = q_ref[0]
    k_win = jnp.concatenate([k0[0], k1[0], k2[0], k3[0]], axis=0)
    v_win = jnp.concatenate([v0[0], v1[0], v2[0], v3[0]], axis=0)
    k_ctx = kc_ref[0]
    v_ctx = vc_ref[0]
    lane_lo = lax.broadcasted_iota(jnp.int32, (1, LANES), 1) < NA_HEAD_DIM

    def scores(qh, half):
        return [_dot_nt(qh, k_win) + bias_ref[0, half], _dot_nt(qh, k_ctx)]

    def av(ps):
        return _dot(ps[0], v_win) + _dot(ps[1], v_ctx)

    o_ref[0] = _pair_softmax_av(q, lane_lo, scores, av).astype(o_ref.dtype)


def _na_bias_tables(rpb, rows):
    n_keyrows = NA_ROWS + 2 * NA_UNIT
    col = np.arange(GRID_W)
    col_start = np.clip(col - NA_KW // 2, 0, GRID_W - NA_KW)
    kc = np.arange(GRID_W)
    col_ok = (kc[None, :] >= col_start[:, None]) & (kc[None, :] < col_start[:, None] + NA_KW)
    rel_col = np.clip(kc[None, :] - col[:, None] + (NA_KW - 1), 0, 2 * NA_KW - 2)
    col_sel = ((rel_col[:, :, None] == np.arange(2 * NA_KW - 1)) & col_ok[:, :, None]).astype(np.float32)
    row_sel, row_oks = [], []
    n_blocks = rows // NA_ROWS
    for blk in (0, min(1, n_blocks - 1), n_blocks - 1):
        r = blk * NA_ROWS + np.arange(NA_ROWS)
        kh = min(NA_KH, rows)
        r_start = np.clip(r - kh // 2, 0, rows - kh)
        key_row0 = np.clip(blk * NA_ROWS - NA_UNIT, 0, rows - n_keyrows)
        kr = key_row0 + np.arange(n_keyrows)
        row_ok = (kr[None, :] >= r_start[:, None]) & (kr[None, :] < r_start[:, None] + kh)
        rel_row = np.clip(kr[None, :] - r[:, None] + (NA_KH - 1), 0, 2 * NA_KH - 2)
        row_sel.append(((rel_row[:, :, None] == np.arange(2 * NA_KH - 1)) & row_ok[:, :, None]).astype(np.float32))
        row_oks.append(row_ok)
    row_sel = np.stack(row_sel)
    ok = np.stack(row_oks)[:, :, None, :, None] & col_ok[None, None, :, None, :]
    by_col = jnp.einsum('hab,ckb->hack', rpb, jnp.asarray(col_sel), precision=lax.Precision.HIGHEST)
    bias = jnp.einsum('zita,hack->zhictk', jnp.asarray(row_sel), by_col, precision=lax.Precision.HIGHEST)
    bias = jnp.where(jnp.asarray(ok)[:, None], bias, MASK_NEG)
    return bias.reshape(3, rpb.shape[0], NA_ROWS * GRID_W, n_keyrows * GRID_W)


def _na_attention(qkv, bias, *, n_latent):
    n_batch, t, _ = qkv.shape
    rows = n_latent // GRID_W
    n_blocks = rows // NA_ROWS
    n_pairs = NA_HEADS // 2
    tq = NA_ROWS * GRID_W
    tu = NA_UNIT * GRID_W
    n_units = rows // NA_UNIT
    ctx_blk = n_latent // tu

    def unit(u):
        def im(p, b, rb):
            u0 = jnp.clip(rb * (NA_ROWS // NA_UNIT) - 1, 0, n_units - 4)
            return u0 + u
        return im

    k_specs = [pl.BlockSpec((1, tu, LANES), lambda p, b, rb, f=unit(u): (b, f(p, b, rb), n_pairs + p))
               for u in range(4)]
    v_specs = [pl.BlockSpec((1, tu, LANES), lambda p, b, rb, f=unit(u): (b, f(p, b, rb), 2 * n_pairs + p))
               for u in range(4)]

    def bias_im(p, b, rb):
        kind = jnp.where(rb == 0, 0, jnp.where(rb == n_blocks - 1, 2, 1))
        return (kind, p, 0, 0)

    return pl.pallas_call(
        _na_kernel,
        grid=(n_pairs, n_batch, n_blocks),
        in_specs=[pl.BlockSpec((1, tq, LANES), lambda p, b, rb: (b, rb, p))] + k_specs + v_specs
                 + [pl.BlockSpec((1, tu, LANES), lambda p, b, rb: (b, ctx_blk, n_pairs + p)),
                    pl.BlockSpec((1, tu, LANES), lambda p, b, rb: (b, ctx_blk, 2 * n_pairs + p)),
                    pl.BlockSpec((1, 2, tq, 4 * tu), bias_im)],
        out_specs=pl.BlockSpec((1, tq, LANES), lambda p, b, rb: (b, rb, p)),
        out_shape=jax.ShapeDtypeStruct((n_batch, t, NA_HEADS * NA_HEAD_DIM), BF16),
        compiler_params=_cparams(("arbitrary",) * 3),
        name="neighbourhood_attention",
    )(qkv, *([qkv] * 10), bias)


def _ctx_pair_kernel(q_ref, k_ref, v_ref, _prev, o_ref):
    k = k_ref[0]
    v = v_ref[0]
    lane_lo = lax.broadcasted_iota(jnp.int32, (1, LANES), 1) < NA_HEAD_DIM
    o_ref[0] = _pair_softmax_av(q_ref[0], lane_lo, lambda qh, half: [_dot_nt(qh, k)],
                                lambda ps: _dot(ps[0], v)).astype(o_ref.dtype)


def _ctx_pair_attention(qkv, prev_out, *, n_latent):
    n_batch, t, _ = qkv.shape
    n_ctx = t - n_latent
    blk = n_latent // n_ctx
    n_pairs = NA_HEADS // 2
    spec = lambda off: pl.BlockSpec((1, n_ctx, LANES), lambda b, p: (b, blk, off + p))
    return pl.pallas_call(
        _ctx_pair_kernel,
        grid=(n_batch, n_pairs),
        in_specs=[spec(0), spec(n_pairs), spec(2 * n_pairs), pl.BlockSpec(memory_space=pl.ANY)],
        out_specs=spec(0),
        out_shape=jax.ShapeDtypeStruct(prev_out.shape, prev_out.dtype),
        input_output_aliases={3: 0},
        compiler_params=_cparams(("arbitrary", "arbitrary")),
        name="context_pair_attention",
    )(qkv, qkv, qkv, prev_out)


def _odd_out_kernel(x_ref, a_ref, w_ref, gl, gc, o_ref, *, tm, n_latent):
    i = pl.program_id(1)
    is_ctx = _is_ctx_rows(i, tm, n_latent)
    y = _dot(a_ref[0], w_ref[...])
    o_ref[0] = _post_norm(x_ref[0], y, _row_select(is_ctx, gl, gc), 1.0)


def _odd_out(x, a, w, gate, *, n_latent, n_rows, tm):
    n_batch, _, d = x.shape
    lat, ctx = _mod_specs(n_batch, d, 2)
    row = lambda wd: pl.BlockSpec((1, tm, wd), lambda b, i: (b, i, 0))
    return pl.pallas_call(
        functools.partial(_odd_out_kernel, tm=tm, n_latent=n_latent),
        grid=(n_batch, n_rows // tm),
        in_specs=[row(d), row(a.shape[-1]), pl.BlockSpec(w.shape, lambda b, i: (0, 0)), lat, ctx],
        out_specs=row(d),
        out_shape=jax.ShapeDtypeStruct((n_batch, n_rows, d), F32),
        compiler_params=_cparams(("arbitrary", "arbitrary")),
        name="odd_out_postnorm",
    )(x, a, w, gate, gate)


def _even_weights(w_in, wg2_f, bg_f, wg2_b, bg_b, q_norm_g, kv_norm_g, w_uq, w_ukv):
    sizes = (GLA_HEADS * GLA_DK, GLA_HEADS * GLA_DK, GLA_HEADS * GLA_DV, GLA_HEADS * GLA_DV,
             GLA_RANK, GLA_RANK, MLA_D_CQ, MLA_D_CKV, MLA_D_ROPE)
    q_g, k_g, v_g, r_g, lr_f, lr_b, c_q, c_kv, k_r = jnp.split(w_in, np.cumsum(sizes)[:-1].tolist(), axis=-1)
    swap = np.arange(MLA_D_ROPE) ^ 1
    d = w_in.shape[0]
    zeros = lambda n: jnp.zeros((d, n), w_in.dtype)
    w1 = jnp.concatenate([
        q_g, k_g, v_g, r_g,
        lr_f, lr_b, zeros(LANES - 2 * GLA_RANK), c_q, c_kv,
        k_r, zeros(LANES - MLA_D_ROPE), k_r[:, swap], zeros(LANES - MLA_D_ROPE)], axis=-1).astype(BF16)
    hp = GLA_HEADS * GLA_DK
    wg = jnp.zeros((LANES, 2 * hp), F32)
    wg = wg.at[:GLA_RANK, :hp].set(wg2_f)
    wg = wg.at[GLA_RANK:2 * GLA_RANK, hp:].set(wg2_b)
    bg = jnp.concatenate([bg_f, bg_b])[None]
    uq = w_uq.reshape(MLA_D_CQ, MLA_HEADS, MLA_D_NOPE + MLA_D_ROPE)
    nope, rope = uq[..., :MLA_D_NOPE], uq[..., MLA_D_NOPE:]
    zr = jnp.zeros((MLA_D_CQ, MLA_HEADS, LANES - MLA_D_ROPE), w_uq.dtype)
    wq_main = jnp.concatenate([nope, rope, zr], axis=-1).reshape(MLA_D_CQ, -1).astype(BF16)
    wq_swap = jnp.concatenate([rope[..., swap], zr], axis=-1).reshape(MLA_D_CQ, -1).astype(BF16)
    ukv = w_ukv.reshape(MLA_D_CKV, MLA_HEADS, MLA_D_NOPE + MLA_D_V)
    wkv = jnp.concatenate([ukv[..., :MLA_D_NOPE].reshape(MLA_D_CKV, -1),
                           ukv[..., MLA_D_NOPE:].reshape(MLA_D_CKV, -1)], axis=-1).astype(BF16)
    return (w1, wg.astype(BF16), bg, q_norm_g[None], kv_norm_g[None], wq_main, wq_swap, wkv)


def _rope_tables(n_latent, n_ctx):
    t = jnp.arange(n_latent)
    row = (t // GRID_W).astype(F32)
    col = (t % GRID_W).astype(F32)
    n_freq = MLA_D_ROPE // 4
    inv = ROPE_BASE ** (-jnp.arange(n_freq, dtype=F32) / n_freq)
    ang = jnp.concatenate([row[:, None] * inv, col[:, None] * inv], axis=-1)
    cos = jnp.repeat(jnp.cos(ang), 2, axis=-1)
    sin = jnp.repeat(jnp.sin(ang), 2, axis=-1) * jnp.tile(jnp.array([-1.0, 1.0], F32), MLA_D_ROPE // 2)
    cos = jnp.concatenate([cos, jnp.ones((n_ctx, MLA_D_ROPE), F32)])
    sin = jnp.concatenate([sin, jnp.zeros((n_ctx, MLA_D_ROPE), F32)])
    pad = jnp.zeros((n_latent + n_ctx, LANES - MLA_D_ROPE), F32)
    return jnp.concatenate([cos, pad], axis=-1), jnp.concatenate([sin, pad], axis=-1)


def _row_tile(n_rows, target):
    best = 8
    for cand in range(8, target + 1, 8):
        if n_rows % cand == 0:
            best = cand
    return best


def kernel(x, c, ctx, c_ctx, ada_w, ada_b, ffn1_w_in, ffn1_w_out, ffn2_w_in, ffn2_w_out, even_w_in, gla_wg2_f, gla_bg_f, gla_wg2_b, gla_bg_b, gla_norm_g, mla_q_norm_g, mla_kv_norm_g, mla_w_uq, mla_w_ukv, even_w_out, na_w_in, na_rpb, na_w_out):
    n_batch, n_latent, d = x.shape
    n_ctx = ctx.shape[1]
    t_all = n_latent + n_ctx
    assert n_batch + 1 <= 8 and n_latent % (NA_ROWS * GRID_W) == 0 and n_ctx == NA_UNIT * GRID_W
    assert n_latent % GLA_CHUNK == 0 and n_ctx % GLA_CHUNK == 0

    xa = jnp.concatenate([x, ctx], axis=1)
    cc = jnp.concatenate([c, c_ctx[None], jnp.zeros((7 - n_batch, d), F32)], axis=0)
    mods = _ada_modulation(cc, ada_w, ada_b)
    mods = mods[:, :n_batch + 1].reshape(DEPTH, n_batch + 1, 9, 1, d).transpose(0, 2, 1, 3, 4)

    tm_ffn = _row_tile(t_all, 768)
    tm_ffn_lat = _row_tile(n_latent, 1024)
    tm_row = _row_tile(t_all, 256)
    tm_row_lat = _row_tile(n_latent, 256)
    tf = D_FF // 2
    rope_c, rope_s = _rope_tables(n_latent, n_ctx)
    tq = _row_tile(n_latent, 1024)
    ck = 2 * LANES
    tk = ck * max(n for n in range(1, 12) if t_all % (ck * n) == 0)

    for l in range(DEPTH):
        last = l == DEPTH - 1
        i = l // 2
        m = mods[l]
        xa = _ffn(xa, (m[0], m[1], m[2]), ffn1_w_in[l].astype(BF16), ffn1_w_out[l].astype(BF16),
                  n_latent=n_latent, n_rows=t_all, tm=tm_ffn, tf=tf)
        n_rows = n_latent if last else t_all
        tm_out = tm_row_lat if last else tm_row
        if l % 2 == 0:
            wts = _even_weights(even_w_in[i], gla_wg2_f[i], gla_bg_f[i], gla_wg2_b[i], gla_bg_b[i],
                                mla_q_norm_g[i], mla_kv_norm_g[i], mla_w_uq[i], mla_w_ukv[i])
            gq, gk, gv, r, gf, gb, qm, km, vm = _even_proj(xa, (m[3], m[4]), wts, rope_c, rope_s,
                                                           n_latent=n_latent, tm=tm_row)
            o_f, o_b = _gla(gq, gk, gv, gf, gb, n_latent=n_latent, chunk=GLA_CHUNK)
            mla_o = _flash(qm, km, vm, None, n_heads=MLA_HEADS, dq=2 * LANES, dv=MLA_D_V, tq=tq, tk=tk, ck=ck,
                           q_blk0=0, n_q=n_latent // tq, k_blk0=0, n_k=t_all // tk)
            if not last:
                mla_o = _flash(qm, km, vm, mla_o, n_heads=MLA_HEADS, dq=2 * LANES, dv=MLA_D_V,
                               tq=n_ctx, tk=n_ctx, ck=ck, q_blk0=n_latent // n_ctx, n_q=1,
                               k_blk0=n_latent // n_ctx, n_k=1)
            w_out = even_w_out[i].astype(BF16)
            gv_w = GLA_HEADS * GLA_DV
            xa = _even_out(xa, o_f, o_b, r, gla_norm_g[i][None], mla_o, w_out[:gv_w], w_out[gv_w:], m[5],
                           n_latent=n_latent, n_rows=n_rows, tm=tm_out)
        else:
            qkv = _qkv_proj(xa, (m[3], m[4]), na_w_in[i].astype(BF16), n_latent=n_latent, tm=tm_row)
            bias = _na_bias_tables(na_rpb[i], n_latent // GRID_W)
            att = _na_attention(qkv, bias, n_latent=n_latent)
            if not last:
                att = _ctx_pair_attention(qkv, att, n_latent=n_latent)
            xa = _odd_out(xa, att, na_w_out[i].astype(BF16), m[5], n_latent=n_latent, n_rows=n_rows, tm=tm_out)
        xa = _ffn(xa, (m[6], m[7], m[8]), ffn2_w_in[l].astype(BF16), ffn2_w_out[l].astype(BF16),
                  n_latent=n_latent, n_rows=n_rows, tm=tm_ffn_lat if last else tm_ffn, tf=tf)
    return xa
```

```python
import functools

import numpy as np
import jax
import jax.numpy as jnp
from jax import lax
from jax.experimental import pallas as pl
from jax.experimental.pallas import tpu as pltpu

DEPTH = 4
GRID_W = 64
D_FF = 2816
GLA_HEADS = 4
GLA_DK = 64
GLA_DV = 128
GLA_RANK = 16
GLA_TAU = 16.0
MLA_HEADS = 4
MLA_D_NOPE = 128
MLA_D_ROPE = 64
MLA_D_V = 128
MLA_D_CQ = 384
MLA_D_CKV = 128
MLA_SCALE = (MLA_D_NOPE + MLA_D_ROPE) ** -0.5
LOG2_E = 1.4426950408889634
MLA_QSCALE = MLA_SCALE * LOG2_E
NA_HEADS = 16
NA_HEAD_DIM = 64
NA_KH = 8
NA_KW = 16
ROPE_BASE = 10000.0
EPS = 1e-6
ALPHA = (2 * DEPTH) ** 0.25

LANES = 128
VMEM_LIMIT = 56 * 1024 * 1024

GLA_CHUNK = 128
NA_ROWS = 8
NA_UNIT = 4
MASK_NEG = -1e30

BF16 = jnp.bfloat16
F32 = jnp.float32


def _cparams(sem):
    return pltpu.CompilerParams(dimension_semantics=sem, vmem_limit_bytes=VMEM_LIMIT)


def _dot(a, b):
    return jnp.dot(a, b, preferred_element_type=F32)


def _dot_nt(a, b):
    return lax.dot_general(a, b, (((1,), (1,)), ((), ())), preferred_element_type=F32)


def _dot_tn(a, b):
    return lax.dot_general(a, b, (((0,), (0,)), ((), ())), preferred_element_type=F32)


def _silu(v):
    return v * (1.0 / (1.0 + jnp.exp(-v)))


def _row_select(is_ctx, lat_ref, ctx_ref):
    return jnp.where(is_ctx, ctx_ref[0], lat_ref[0])


def _is_ctx_rows(tile_idx, tm, n_latent):
    rows = tile_idx * tm + lax.broadcasted_iota(jnp.int32, (tm, 1), 0)
    return rows >= n_latent


def _modulate(x, is_ctx, sh_l, sh_c, sc_l, sc_c):
    shift = _row_select(is_ctx, sh_l, sh_c)
    scale = _row_select(is_ctx, sc_l, sc_c)
    return x * (1.0 + scale) + shift


def _post_norm(x, y, gate, coef):
    z = ALPHA * x + (coef * gate) * y
    mu = jnp.mean(z, axis=-1, keepdims=True)
    zc = z - mu
    var = jnp.mean(zc * zc, axis=-1, keepdims=True)
    return zc * lax.rsqrt(var + EPS)


def _mod_specs(n_batch, d, grid_rank):
    if grid_rank == 2:
        lat = pl.BlockSpec((1, 1, d), lambda b, i: (b, 0, 0))
        ctx = pl.BlockSpec((1, 1, d), lambda b, i: (n_batch, 0, 0))
    else:
        lat = pl.BlockSpec((1, 1, d), lambda b, i, j: (b, 0, 0))
        ctx = pl.BlockSpec((1, 1, d), lambda b, i, j: (n_batch, 0, 0))
    return lat, ctx


def _ada_kernel(c_ref, w_ref, b_ref, o_ref):
    a = _silu(c_ref[...]).astype(BF16)
    o_ref[0] = _dot(a, w_ref[0].astype(BF16)) + b_ref[0]


def _ada_modulation(cc, ada_w, ada_b):
    depth, d, n9 = ada_w.shape
    tn = n9 // 8
    return pl.pallas_call(
        _ada_kernel,
        grid=(depth, n9 // tn),
        in_specs=[pl.BlockSpec((8, d), lambda l, j: (0, 0)),
                  pl.BlockSpec((1, d, tn), lambda l, j: (l, 0, j)),
                  pl.BlockSpec((1, 1, tn), lambda l, j: (l, 0, j))],
        out_specs=pl.BlockSpec((1, 8, tn), lambda l, j: (l, 0, j)),
        out_shape=jax.ShapeDtypeStruct((depth, 8, n9), F32),
        compiler_params=_cparams(("arbitrary", "arbitrary")),
        name="ada_modulation",
    )(cc, ada_w, ada_b.reshape(depth, 1, n9))


def _ffn_kernel(x_ref, shl, shc, scl, scc, gl, gc, wi_ref, wo_ref, o_ref, *, tm, n_latent, n_chunks):
    i = pl.program_id(1)
    is_ctx = _is_ctx_rows(i, tm, n_latent)
    x = x_ref[0]
    h = _modulate(x, is_ctx, shl, shc, scl, scc).astype(BF16)
    d_ff = wo_ref.shape[0]
    tf = d_ff // n_chunks
    y = None
    for c in range(n_chunks):
        gate = _dot(h, wi_ref[:, c * tf:(c + 1) * tf])
        up = _dot(h, wi_ref[:, d_ff + c * tf:d_ff + (c + 1) * tf])
        act = (_silu(gate) * up).astype(BF16)
        part = _dot(act, wo_ref[c * tf:(c + 1) * tf, :])
        y = part if y is None else y + part
    o_ref[0] = _post_norm(x, y, _row_select(is_ctx, gl, gc), 0.5)


def _ffn(x, mods, w_in, w_out, *, n_latent, n_rows, tm, n_chunks):
    n_batch, _, d = x.shape
    sh, sc, g = mods
    lat, ctx = _mod_specs(n_batch, d, 2)
    resident = lambda w: pl.BlockSpec(w.shape, lambda b, i: (0, 0), pipeline_mode=pl.Buffered(1))
    return pl.pallas_call(
        functools.partial(_ffn_kernel, tm=tm, n_latent=n_latent, n_chunks=n_chunks),
        grid=(n_batch, n_rows // tm),
        in_specs=[pl.BlockSpec((1, tm, d), lambda b, i: (b, i, 0)),
                  lat, ctx, lat, ctx, lat, ctx, resident(w_in), resident(w_out)],
        out_specs=pl.BlockSpec((1, tm, d), lambda b, i: (b, i, 0)),
        out_shape=jax.ShapeDtypeStruct((n_batch, n_rows, d), F32),
        compiler_params=_cparams(("arbitrary", "arbitrary")),
        name="ffn_postnorm",
    )(x, sh, sh, sc, sc, g, g, w_in, w_out)


def _even_proj_kernel(x_ref, shl, shc, scl, scc, w1_ref, wg_ref, bg_ref, qng_ref, kvng_ref,
                      wqm_ref, wqs_ref, wkv_ref, cb_ref, sb_ref,
                      gq_ref, gk_ref, gv_ref, r_ref, gf_ref, gb_ref, qm_ref, km_ref, vm_ref,
                      *, tm, n_latent):
    i = pl.program_id(1)
    is_ctx = _is_ctx_rows(i, tm, n_latent)
    h = _modulate(x_ref[0], is_ctx, shl, shc, scl, scc).astype(BF16)
    p = _dot(h, w1_ref[...])
    hp = GLA_HEADS * GLA_DK
    gv_w = GLA_HEADS * GLA_DV
    o0 = 0
    gq_ref[0] = p[:, o0:o0 + hp] * (GLA_DK ** -0.5)
    o0 += hp
    gk_ref[0] = p[:, o0:o0 + hp]
    o0 += hp
    gv_ref[0] = p[:, o0:o0 + gv_w].astype(BF16)
    o0 += gv_w
    r_ref[0] = p[:, o0:o0 + gv_w]
    o0 += gv_w
    lr = p[:, o0:o0 + LANES].astype(BF16)
    o0 += LANES
    cq = p[:, o0:o0 + MLA_D_CQ]
    o0 += MLA_D_CQ
    ckv = p[:, o0:o0 + MLA_D_CKV]
    o0 += MLA_D_CKV
    kr = p[:, o0:o0 + LANES]
    o0 += LANES
    krs = p[:, o0:o0 + LANES]

    z = _dot(lr, wg_ref[...]) + bg_ref[...]
    logg = (jnp.minimum(z, 0.0) - jnp.log1p(jnp.exp(-jnp.abs(z)))) / GLA_TAU
    gf_ref[0] = logg[:, :hp]
    gb_ref[0] = logg[:, hp:]

    cb = cb_ref[...]
    sb = sb_ref[...]
    cqn = (cq * lax.rsqrt(jnp.mean(cq * cq, axis=-1, keepdims=True) + EPS) * qng_ref[...]).astype(BF16)
    qmain = _dot(cqn, wqm_ref[...])
    qswap = _dot(cqn, wqs_ref[...])
    ckn = (ckv * lax.rsqrt(jnp.mean(ckv * ckv, axis=-1, keepdims=True) + EPS) * kvng_ref[...]).astype(BF16)
    kv = _dot(ckn, wkv_ref[...])
    k_rope = (kr * cb + krs * sb).astype(BF16)
    ones_col = (lax.broadcasted_iota(jnp.int32, (1, LANES), 1) == 0).astype(BF16)
    for hd in range(MLA_HEADS):
        b0 = 2 * LANES * hd
        qm_ref[0, :, b0:b0 + LANES] = (qmain[:, b0:b0 + LANES] * MLA_QSCALE).astype(BF16)
        q_rope = qmain[:, b0 + LANES:b0 + 2 * LANES] * cb + qswap[:, LANES * hd:LANES * (hd + 1)] * sb
        qm_ref[0, :, b0 + LANES:b0 + 2 * LANES] = (q_rope * MLA_QSCALE).astype(BF16)
        km_ref[0, :, b0:b0 + LANES] = kv[:, LANES * hd:LANES * (hd + 1)].astype(BF16)
        km_ref[0, :, b0 + LANES:b0 + 2 * LANES] = k_rope
        v0 = MLA_HEADS * MLA_D_NOPE + MLA_D_V * hd
        vm_ref[0, :, b0:b0 + LANES] = kv[:, v0:v0 + MLA_D_V].astype(BF16)
        vm_ref[0, :, b0 + LANES:b0 + 2 * LANES] = jnp.broadcast_to(ones_col, (tm, LANES))


def _even_proj(x, mods, wts, rope_c, rope_s, *, n_latent, tm):
    n_batch, t, d = x.shape
    sh, sc = mods
    lat, ctx = _mod_specs(n_batch, d, 2)
    full = lambda a: pl.BlockSpec(a.shape, lambda b, i: (0,) * a.ndim)
    row = lambda w: pl.BlockSpec((1, tm, w), lambda b, i: (b, i, 0))
    hp = GLA_HEADS * GLA_DK
    gv_w = GLA_HEADS * GLA_DV
    out_w = [(hp, F32), (hp, F32), (gv_w, BF16), (gv_w, F32), (hp, F32), (hp, F32),
             (MLA_HEADS * 2 * LANES, BF16), (MLA_HEADS * 2 * LANES, BF16), (MLA_HEADS * 2 * MLA_D_V, BF16)]
    return pl.pallas_call(
        functools.partial(_even_proj_kernel, tm=tm, n_latent=n_latent),
        grid=(n_batch, t // tm),
        in_specs=[row(d), lat, ctx, lat, ctx] + [full(w) for w in wts]
                 + [pl.BlockSpec((tm, LANES), lambda b, i: (i, 0))] * 2,
        out_specs=[row(w) for w, _ in out_w],
        out_shape=[jax.ShapeDtypeStruct((n_batch, t, w), dt) for w, dt in out_w],
        compiler_params=_cparams(("arbitrary", "arbitrary")),
        name="even_project",
    )(x, sh, sh, sc, sc, *wts, rope_c, rope_s)


def _gla_constants(chunk, reverse):
    n_lvl = int(np.log2(chunk))
    pos = np.arange(chunk)
    src = pos[None, :]
    tri = src <= pos[:, None]
    cum, mask = [], []
    for lvl in range(n_lvl):
        size = chunk >> lvl
        half = size // 2
        ref = ((pos // size) * size + half - 1)[:, None]
        upper = (pos % size) >= half
        a_up = (src > ref) & (src <= pos[:, None])
        a_lo = (src > pos[:, None]) & (src <= ref)
        cum.append(np.where(upper[:, None], a_up, a_lo))
        same = (pos[:, None] // size) == (pos[None, :] // size)
        mask.append(same & upper[:, None] & ~upper[None, :])
    cum += [tri, ~tri]
    cum = np.stack(cum).astype(np.float32)
    mask = np.stack(mask).astype(np.float32)
    if reverse:
        cum = cum[:, ::-1, ::-1]
        mask = mask[:, ::-1, ::-1]
    return np.ascontiguousarray(cum).reshape(-1, chunk), np.ascontiguousarray(mask)


def _gla_direction(q_ref, k_ref, v_ref, g_ref, cum_ref, mask_ref, st_ref, o_ref, *, chunk, reverse):
    n_lvl = mask_ref.shape[0]
    g = g_ref[0]
    g_hi = g.astype(BF16)
    g_lo = (g - g_hi.astype(F32)).astype(BF16)
    cum = cum_ref[...]
    sums = _dot(cum, g_hi) + _dot(cum, g_lo)
    b_rows = slice(n_lvl * chunk, (n_lvl + 1) * chunk)
    rest_rows = slice((n_lvl + 1) * chunk, (n_lvl + 2) * chunk)
    last = 0 if reverse else chunk - 1
    eye = (lax.broadcasted_iota(jnp.int32, (chunk, chunk), 0)
           == lax.broadcasted_iota(jnp.int32, (chunk, chunk), 1)).astype(F32)
    lane_lo = lax.broadcasted_iota(jnp.int32, (1, LANES), 1) < GLA_DK
    heads_per_group = LANES // GLA_DK

    def head_only(x, hh):
        return jnp.where(lane_lo if hh == 0 else jnp.logical_not(lane_lo), x, jnp.zeros_like(x))

    for grp in range(GLA_HEADS // heads_per_group):
        sl = slice(grp * LANES, (grp + 1) * LANES)
        q = q_ref[0, :, sl]
        k = k_ref[0, :, sl]
        att = [jnp.zeros((chunk, chunk), F32) for _ in range(heads_per_group)]
        for lvl in range(n_lvl):
            w = jnp.exp(sums[lvl * chunk:(lvl + 1) * chunk, sl])
            qt = (q * w).astype(BF16)
            kt = (k * w).astype(BF16)
            for hh in range(heads_per_group):
                att[hh] = att[hh] + _dot_nt(head_only(qt, hh), kt) * mask_ref[lvl]
        if not reverse:
            qb, kb = q.astype(BF16), k.astype(BF16)
            for hh in range(heads_per_group):
                att[hh] = att[hh] + _dot_nt(head_only(qb, hh), kb) * eye
        b = sums[b_rows, sl]
        qe = (q * jnp.exp(b)).astype(BF16)
        kd = (k * jnp.exp(sums[rest_rows, sl])).astype(BF16)
        decay = jnp.exp(b[last:last + 1, :])
        for hh in range(heads_per_group):
            hd = grp * heads_per_group + hh
            vs = slice(hd * GLA_DV, (hd + 1) * GLA_DV)
            v = v_ref[0, :, vs]
            st = st_ref[hd]
            o = _dot(att[hh].astype(BF16), v) + _dot_nt(head_only(qe, hh), st.astype(BF16))
            o_ref[0, :, vs] = o
            st_ref[hd] = st * decay + _dot_tn(v, kd)


def _gla_kernel(qf, kf, vf, gf, qb, kb, vb, gb, cumf, maskf, cumb, maskb, of_ref, ob_ref, stf, stb,
                *, chunk):
    @pl.when(pl.program_id(1) == 0)
    def _():
        stf[...] = jnp.zeros_like(stf)
        stb[...] = jnp.zeros_like(stb)

    _gla_direction(qf, kf, vf, gf, cumf, maskf, stf, of_ref, chunk=chunk, reverse=False)
    _gla_direction(qb, kb, vb, gb, cumb, maskb, stb, ob_ref, chunk=chunk, reverse=True)


def _gla(gq, gk, gv, gf, gb, *, n_latent, chunk):
    n_batch, t, hp = gq.shape
    n_lat = n_latent // chunk
    n_ctx = (t - n_latent) // chunk
    n_steps = n_lat + n_ctx

    def fwd(b, s):
        return (b, jnp.where(s < n_ctx, n_lat + s, s - n_ctx), 0)

    def bwd(b, s):
        return (b, n_steps - 1 - s, 0)

    cumf, maskf = _gla_constants(chunk, False)
    cumb, maskb = _gla_constants(chunk, True)
    consts = [jnp.asarray(cumf, BF16), jnp.asarray(maskf), jnp.asarray(cumb, BF16), jnp.asarray(maskb)]
    full = lambda a: pl.BlockSpec(a.shape, lambda b, s: (0,) * a.ndim)
    blk = lambda w, im: pl.BlockSpec((1, chunk, w), im)
    gv_w = gv.shape[-1]
    return pl.pallas_call(
        functools.partial(_gla_kernel, chunk=chunk),
        grid=(n_batch, n_steps),
        in_specs=[blk(hp, fwd), blk(hp, fwd), blk(gv_w, fwd), blk(hp, fwd),
                  blk(hp, bwd), blk(hp, bwd), blk(gv_w, bwd), blk(hp, bwd)] + [full(a) for a in consts],
        out_specs=[blk(gv_w, fwd), blk(gv_w, bwd)],
        out_shape=[jax.ShapeDtypeStruct((n_batch, t, gv_w), F32)] * 2,
        scratch_shapes=[pltpu.VMEM((GLA_HEADS, GLA_DV, LANES), F32)] * 2,
        compiler_params=_cparams(("arbitrary", "arbitrary")),
        name="gla_bidirectional",
    )(gq, gk, gv, gf, gq, gk, gv, gb, *consts)


def _flash_kernel(q_ref, k_ref, v_ref, o_ref, m_ref, acc_ref, *, ck, dv):
    j = pl.program_id(3)
    n_sub = k_ref.shape[1] // ck

    @pl.when(j == 0)
    def _():
        m_ref[...] = jnp.full_like(m_ref, -jnp.inf)
        acc_ref[...] = jnp.zeros_like(acc_ref)

    q = q_ref[0]

    for c in range(n_sub):
        off = c * ck
        s = _dot_nt(q, k_ref[0, pl.ds(off, ck), :])
        m_prev = m_ref[...]
        m_new = jnp.maximum(m_prev, jnp.max(s, axis=-1, keepdims=True))
        alpha = jnp.exp2(m_prev - m_new)
        p = jnp.concatenate([jnp.exp2(s[:, u * LANES:(u + 1) * LANES] - m_new)
                             for u in range(ck // LANES)], axis=-1).astype(BF16)
        pv = _dot(p, v_ref[0, pl.ds(off, ck), :])
        acc_ref[...] = jnp.concatenate([alpha] * (acc_ref.shape[1] // LANES), axis=-1) * acc_ref[...] + pv
        m_ref[...] = m_new

    @pl.when(j == pl.num_programs(3) - 1)
    def _():
        acc = acc_ref[...]
        denom = jnp.sum(acc[:, dv:], axis=-1, keepdims=True)
        o_ref[0] = (acc[:, :dv] / denom).astype(o_ref.dtype)


def _flash(q, k, v, prev_out, *, n_heads, dq, dv, tq, tk, ck, q_blk0, n_q, k_blk0, n_k):
    n_batch, t, _ = q.shape
    in_specs = [pl.BlockSpec((1, tq, dq), lambda b, h, i, j: (b, q_blk0 + i, h)),
                pl.BlockSpec((1, tk, dq), lambda b, h, i, j: (b, k_blk0 + j, h)),
                pl.BlockSpec((1, tk, 2 * dv), lambda b, h, i, j: (b, k_blk0 + j, h))]
    args = [q, k, v]
    aliases = {}
    base = functools.partial(_flash_kernel, ck=ck, dv=dv)
    kern = base
    if prev_out is not None:
        in_specs.append(pl.BlockSpec(memory_space=pl.ANY))
        args.append(prev_out)
        aliases = {3: 0}
        kern = lambda q_, k_, v_, _prev, *rest: base(q_, k_, v_, *rest)
    return pl.pallas_call(
        kern,
        grid=(n_batch, n_heads, n_q, n_k),
        in_specs=in_specs,
        out_specs=pl.BlockSpec((1, tq, dv), lambda b, h, i, j: (b, q_blk0 + i, h)),
        out_shape=jax.ShapeDtypeStruct((n_batch, t, n_heads * dv), BF16),
        scratch_shapes=[pltpu.VMEM((tq, LANES), F32), pltpu.VMEM((tq, 2 * dv), F32)],
        input_output_aliases=aliases,
        compiler_params=_cparams(("arbitrary",) * 4),
        name="flash_attention",
    )(*args)


def _even_out_kernel(x_ref, of_ref, ob_ref, r_ref, ng_ref, mo_ref, wa_ref, wb_ref, gl, gc, o_ref,
                     *, tm, n_latent):
    i = pl.program_id(1)
    is_ctx = _is_ctx_rows(i, tm, n_latent)
    o = of_ref[0] + ob_ref[0]
    r = r_ref[0]
    parts = []
    for hd in range(GLA_HEADS):
        sl = slice(hd * GLA_DV, (hd + 1) * GLA_DV)
        oh = o[:, sl]
        y = oh * lax.rsqrt(jnp.mean(oh * oh, axis=-1, keepdims=True) + EPS) * ng_ref[...]
        parts.append((y * _silu(r[:, sl])).astype(BF16))
    a = jnp.concatenate(parts, axis=-1)
    y = _dot(a, wa_ref[...]) + _dot(mo_ref[0], wb_ref[...])
    o_ref[0] = _post_norm(x_ref[0], y, _row_select(is_ctx, gl, gc), 1.0)


def _even_out(x, o_f, o_b, r, norm_g, mla_o, wa, wb, gate, *, n_latent, n_rows, tm):
    n_batch, _, d = x.shape
    lat, ctx = _mod_specs(n_batch, d, 2)
    row = lambda w: pl.BlockSpec((1, tm, w), lambda b, i: (b, i, 0))
    full = lambda a: pl.BlockSpec(a.shape, lambda b, i: (0,) * a.ndim)
    gv_w = o_f.shape[-1]
    return pl.pallas_call(
        functools.partial(_even_out_kernel, tm=tm, n_latent=n_latent),
        grid=(n_batch, n_rows // tm),
        in_specs=[row(d), row(gv_w), row(gv_w), row(gv_w), full(norm_g), row(mla_o.shape[-1]),
                  full(wa), full(wb), lat, ctx],
        out_specs=row(d),
        out_shape=jax.ShapeDtypeStruct((n_batch, n_rows, d), F32),
        compiler_params=_cparams(("arbitrary", "arbitrary")),
        name="even_out_postnorm",
    )(x, o_f, o_b, r, norm_g, mla_o, wa, wb, gate, gate)


def _qkv_kernel(x_ref, shl, shc, scl, scc, w_ref, o_ref, *, tm, n_latent, q_width, q_scale):
    i = pl.program_id(1)
    is_ctx = _is_ctx_rows(i, tm, n_latent)
    h = _modulate(x_ref[0], is_ctx, shl, shc, scl, scc).astype(BF16)
    p = _dot(h, w_ref[...])
    o_ref[0, :, :q_width] = (p[:, :q_width] * q_scale).astype(BF16)
    o_ref[0, :, q_width:] = p[:, q_width:].astype(BF16)


def _qkv_proj(x, mods, w, *, n_latent, tm):
    n_batch, t, d = x.shape
    sh, sc = mods
    lat, ctx = _mod_specs(n_batch, d, 2)
    n_out = w.shape[1]
    return pl.pallas_call(
        functools.partial(_qkv_kernel, tm=tm, n_latent=n_latent, q_width=n_out // 3,
                          q_scale=NA_HEAD_DIM ** -0.5),
        grid=(n_batch, t // tm),
        in_specs=[pl.BlockSpec((1, tm, d), lambda b, i: (b, i, 0)), lat, ctx, lat, ctx,
                  pl.BlockSpec(w.shape, lambda b, i: (0, 0))],
        out_specs=pl.BlockSpec((1, tm, n_out), lambda b, i: (b, i, 0)),
        out_shape=jax.ShapeDtypeStruct((n_batch, t, n_out), BF16),
        compiler_params=_cparams(("arbitrary", "arbitrary")),
        name="qkv_project",
    )(x, sh, sh, sc, sc, w)


def _pair_softmax_av(q, lane_lo, scores_fn, av_fn):
    outs = []
    for half in range(2):
        keep = lane_lo if half == 0 else jnp.logical_not(lane_lo)
        qh = jnp.where(keep, q, jnp.zeros_like(q))
        s_list = scores_fn(qh, half)
        m = s_list[0].max(axis=-1, keepdims=True)
        for s in s_list[1:]:
            m = jnp.maximum(m, s.max(axis=-1, keepdims=True))
        p_list = [jnp.exp(s - m) for s in s_list]
        denom = p_list[0].sum(axis=-1, keepdims=True)
        for p in p_list[1:]:
            denom = denom + p.sum(axis=-1, keepdims=True)
        outs.append(av_fn([p.astype(BF16) for p in p_list]) / denom)
    return jnp.where(lane_lo, outs[0], outs[1])


def _na_kernel(q_ref, k0, k1, k2, k3, v0, v1, v2, v3, kc_ref, vc_ref, bias_ref, o_ref):
    q = q_ref[0]
    k_win = jnp.concatenate([k0[0], k1[0], k2[0], k3[0]], axis=0)
    v_win = jnp.concatenate([v0[0], v1[0], v2[0], v3[0]], axis=0)
    k_ctx = kc_ref[0]
    v_ctx = vc_ref[0]
    lane_lo = lax.broadcasted_iota(jnp.int32, (1, LANES), 1) < NA_HEAD_DIM

    def scores(qh, half):
        return [_dot_nt(qh, k_win) + bias_ref[0, half], _dot_nt(qh, k_ctx)]

    def av(ps):
        return _dot(ps[0], v_win) + _dot(ps[1], v_ctx)

    o_ref[0] = _pair_softmax_av(q, lane_lo, scores, av).astype(o_ref.dtype)


def _na_bias_tables(rpb, rows):
    n_keyrows = NA_ROWS + 2 * NA_UNIT
    col = np.arange(GRID_W)
    col_start = np.clip(col - NA_KW // 2, 0, GRID_W - NA_KW)
    kc = np.arange(GRID_W)
    col_ok = (kc[None, :] >= col_start[:, None]) & (kc[None, :] < col_start[:, None] + NA_KW)
    rel_col = np.clip(kc[None, :] - col[:, None] + (NA_KW - 1), 0, 2 * NA_KW - 2)
    col_sel = ((rel_col[:, :, None] == np.arange(2 * NA_KW - 1)) & col_ok[:, :, None]).astype(np.float32)
    row_sel, row_oks = [], []
    n_blocks = rows // NA_ROWS
    for blk in (0, min(1, n_blocks - 1), n_blocks - 1):
        r = blk * NA_ROWS + np.arange(NA_ROWS)
        kh = min(NA_KH, rows)
        r_start = np.clip(r - kh // 2, 0, rows - kh)
        key_row0 = np.clip(blk * NA_ROWS - NA_UNIT, 0, rows - n_keyrows)
        kr = key_row0 + np.arange(n_keyrows)
        row_ok = (kr[None, :] >= r_start[:, None]) & (kr[None, :] < r_start[:, None] + kh)
        rel_row = np.clip(kr[None, :] - r[:, None] + (NA_KH - 1), 0, 2 * NA_KH - 2)
        row_sel.append(((rel_row[:, :, None] == np.arange(2 * NA_KH - 1)) & row_ok[:, :, None]).astype(np.float32))
        row_oks.append(row_ok)
    row_sel = np.stack(row_sel)
    ok = np.stack(row_oks)[:, :, None, :, None] & col_ok[None, None, :, None, :]
    by_col = jnp.einsum('hab,ckb->hack', rpb, jnp.asarray(col_sel), precision=lax.Precision.HIGHEST)
    bias = jnp.einsum('zita,hack->zhictk', jnp.asarray(row_sel), by_col, precision=lax.Precision.HIGHEST)
    bias = jnp.where(jnp.asarray(ok)[:, None], bias, MASK_NEG)
    return bias.reshape(3, rpb.shape[0], NA_ROWS * GRID_W, n_keyrows * GRID_W)


def _na_attention(qkv, bias, *, n_latent):
    n_batch, t, _ = qkv.shape
    rows = n_latent // GRID_W
    n_blocks = rows // NA_ROWS
    n_pairs = NA_HEADS // 2
    tq = NA_ROWS * GRID_W
    tu = NA_UNIT * GRID_W
    n_units = rows // NA_UNIT
    ctx_blk = n_latent // tu

    def unit(u):
        def im(p, b, rb):
            u0 = jnp.clip(rb * (NA_ROWS // NA_UNIT) - 1, 0, n_units - 4)
            return u0 + u
        return im

    k_specs = [pl.BlockSpec((1, tu, LANES), lambda p, b, rb, f=unit(u): (b, f(p, b, rb), n_pairs + p))
               for u in range(4)]
    v_specs = [pl.BlockSpec((1, tu, LANES), lambda p, b, rb, f=unit(u): (b, f(p, b, rb), 2 * n_pairs + p))
               for u in range(4)]

    def bias_im(p, b, rb):
        kind = jnp.where(rb == 0, 0, jnp.where(rb == n_blocks - 1, 2, 1))
        return (kind, p, 0, 0)

    return pl.pallas_call(
        _na_kernel,
        grid=(n_pairs, n_batch, n_blocks),
        in_specs=[pl.BlockSpec((1, tq, LANES), lambda p, b, rb: (b, rb, p))] + k_specs + v_specs
                 + [pl.BlockSpec((1, tu, LANES), lambda p, b, rb: (b, ctx_blk, n_pairs + p)),
                    pl.BlockSpec((1, tu, LANES), lambda p, b, rb: (b, ctx_blk, 2 * n_pairs + p)),
                    pl.BlockSpec((1, 2, tq, 4 * tu), bias_im)],
        out_specs=pl.BlockSpec((1, tq, LANES), lambda p, b, rb: (b, rb, p)),
        out_shape=jax.ShapeDtypeStruct((n_batch, t, NA_HEADS * NA_HEAD_DIM), BF16),
        compiler_params=_cparams(("arbitrary",) * 3),
        name="neighbourhood_attention",
    )(qkv, *([qkv] * 10), bias)


def _ctx_pair_kernel(q_ref, k_ref, v_ref, _prev, o_ref):
    k = k_ref[0]
    v = v_ref[0]
    lane_lo = lax.broadcasted_iota(jnp.int32, (1, LANES), 1) < NA_HEAD_DIM
    o_ref[0] = _pair_softmax_av(q_ref[0], lane_lo, lambda qh, half: [_dot_nt(qh, k)],
                                lambda ps: _dot(ps[0], v)).astype(o_ref.dtype)


def _ctx_pair_attention(qkv, prev_out, *, n_latent):
    n_batch, t, _ = qkv.shape
    n_ctx = t - n_latent
    blk = n_latent // n_ctx
    n_pairs = NA_HEADS // 2
    spec = lambda off: pl.BlockSpec((1, n_ctx, LANES), lambda b, p: (b, blk, off + p))
    return pl.pallas_call(
        _ctx_pair_kernel,
        grid=(n_batch, n_pairs),
        in_specs=[spec(0), spec(n_pairs), spec(2 * n_pairs), pl.BlockSpec(memory_space=pl.ANY)],
        out_specs=spec(0),
        out_shape=jax.ShapeDtypeStruct(prev_out.shape, prev_out.dtype),
        input_output_aliases={3: 0},
        compiler_params=_cparams(("arbitrary", "arbitrary")),
        name="context_pair_attention",
    )(qkv, qkv, qkv, prev_out)


def _odd_out_kernel(x_ref, a_ref, w_ref, gl, gc, o_ref, *, tm, n_latent):
    i = pl.program_id(1)
    is_ctx = _is_ctx_rows(i, tm, n_latent)
    y = _dot(a_ref[0], w_ref[...])
    o_ref[0] = _post_norm(x_ref[0], y, _row_select(is_ctx, gl, gc), 1.0)


def _odd_out(x, a, w, gate, *, n_latent, n_rows, tm):
    n_batch, _, d = x.shape
    lat, ctx = _mod_specs(n_batch, d, 2)
    row = lambda wd: pl.BlockSpec((1, tm, wd), lambda b, i: (b, i, 0))
    return pl.pallas_call(
        functools.partial(_odd_out_kernel, tm=tm, n_latent=n_latent),
        grid=(n_batch, n_rows // tm),
        in_specs=[row(d), row(a.shape[-1]), pl.BlockSpec(w.shape, lambda b, i: (0, 0)), lat, ctx],
        out_specs=row(d),
        out_shape=jax.ShapeDtypeStruct((n_batch, n_rows, d), F32),
        compiler_params=_cparams(("arbitrary", "arbitrary")),
        name="odd_out_postnorm",
    )(x, a, w, gate, gate)


def _even_weights(w_in, wg2_f, bg_f, wg2_b, bg_b, q_norm_g, kv_norm_g, w_uq, w_ukv):
    sizes = (GLA_HEADS * GLA_DK, GLA_HEADS * GLA_DK, GLA_HEADS * GLA_DV, GLA_HEADS * GLA_DV,
             GLA_RANK, GLA_RANK, MLA_D_CQ, MLA_D_CKV, MLA_D_ROPE)
    q_g, k_g, v_g, r_g, lr_f, lr_b, c_q, c_kv, k_r = jnp.split(w_in, np.cumsum(sizes)[:-1].tolist(), axis=-1)
    swap = np.arange(MLA_D_ROPE) ^ 1
    d = w_in.shape[0]
    zeros = lambda n: jnp.zeros((d, n), w_in.dtype)
    w1 = jnp.concatenate([
        q_g, k_g, v_g, r_g,
        lr_f, lr_b, zeros(LANES - 2 * GLA_RANK), c_q, c_kv,
        k_r, zeros(LANES - MLA_D_ROPE), k_r[:, swap], zeros(LANES - MLA_D_ROPE)], axis=-1).astype(BF16)
    hp = GLA_HEADS * GLA_DK
    wg = jnp.zeros((LANES, 2 * hp), F32)
    wg = wg.at[:GLA_RANK, :hp].set(wg2_f)
    wg = wg.at[GLA_RANK:2 * GLA_RANK, hp:].set(wg2_b)
    bg = jnp.concatenate([bg_f, bg_b])[None]
    uq = w_uq.reshape(MLA_D_CQ, MLA_HEADS, MLA_D_NOPE + MLA_D_ROPE)
    nope, rope = uq[..., :MLA_D_NOPE], uq[..., MLA_D_NOPE:]
    zr = jnp.zeros((MLA_D_CQ, MLA_HEADS, LANES - MLA_D_ROPE), w_uq.dtype)
    wq_main = jnp.concatenate([nope, rope, zr], axis=-1).reshape(MLA_D_CQ, -1).astype(BF16)
    wq_swap = jnp.concatenate([rope[..., swap], zr], axis=-1).reshape(MLA_D_CQ, -1).astype(BF16)
    ukv = w_ukv.reshape(MLA_D_CKV, MLA_HEADS, MLA_D_NOPE + MLA_D_V)
    wkv = jnp.concatenate([ukv[..., :MLA_D_NOPE].reshape(MLA_D_CKV, -1),
                           ukv[..., MLA_D_NOPE:].reshape(MLA_D_CKV, -1)], axis=-1).astype(BF16)
    return (w1, wg.astype(BF16), bg, q_norm_g[None], kv_norm_g[None], wq_main, wq_swap, wkv)


def _rope_tables(n_latent, n_ctx):
    t = jnp.arange(n_latent)
    row = (t // GRID_W).astype(F32)
    col = (t % GRID_W).astype(F32)
    n_freq = MLA_D_ROPE // 4
    inv = ROPE_BASE ** (-jnp.arange(n_freq, dtype=F32) / n_freq)
    ang = jnp.concatenate([row[:, None] * inv, col[:, None] * inv], axis=-1)
    cos = jnp.repeat(jnp.cos(ang), 2, axis=-1)
    sin = jnp.repeat(jnp.sin(ang), 2, axis=-1) * jnp.tile(jnp.array([-1.0, 1.0], F32), MLA_D_ROPE // 2)
    cos = jnp.concatenate([cos, jnp.ones((n_ctx, MLA_D_ROPE), F32)])
    sin = jnp.concatenate([sin, jnp.zeros((n_ctx, MLA_D_ROPE), F32)])
    pad = jnp.zeros((n_latent + n_ctx, LANES - MLA_D_ROPE), F32)
    return jnp.concatenate([cos, pad], axis=-1), jnp.concatenate([sin, pad], axis=-1)


def _row_tile(n_rows, target):
    best = 8
    for cand in range(8, target + 1, 8):
        if n_rows % cand == 0:
            best = cand
    return best


def kernel(x, c, ctx, c_ctx, ada_w, ada_b, ffn1_w_in, ffn1_w_out, ffn2_w_in, ffn2_w_out, even_w_in, gla_wg2_f, gla_bg_f, gla_wg2_b, gla_bg_b, gla_norm_g, mla_q_norm_g, mla_kv_norm_g, mla_w_uq, mla_w_ukv, even_w_out, na_w_in, na_rpb, na_w_out):
    n_batch, n_latent, d = x.shape
    n_ctx = ctx.shape[1]
    t_all = n_latent + n_ctx
    assert n_batch + 1 <= 8 and n_latent % (NA_ROWS * GRID_W) == 0 and n_ctx == NA_UNIT * GRID_W
    assert n_latent % GLA_CHUNK == 0 and n_ctx % GLA_CHUNK == 0

    xa = jnp.concatenate([x, ctx], axis=1)
    cc = jnp.concatenate([c, c_ctx[None], jnp.zeros((7 - n_batch, d), F32)], axis=0)
    mods = _ada_modulation(cc, ada_w, ada_b)
    mods = mods[:, :n_batch + 1].reshape(DEPTH, n_batch + 1, 9, 1, d).transpose(0, 2, 1, 3, 4)

    tm_ffn = _row_tile(t_all, 768)
    tm_ffn_lat = _row_tile(n_latent, 768)
    tm_row = _row_tile(t_all, 256)
    tm_row_lat = _row_tile(n_latent, 256)
    rope_c, rope_s = _rope_tables(n_latent, n_ctx)
    tq = _row_tile(n_latent, 1024)
    ck = next(cand for cand in (768, 512, 256, 128) if t_all % cand == 0)
    ck_ctx = next(cand for cand in (768, 512, 256, 128) if n_ctx % cand == 0)

    for l in range(DEPTH):
        last = l == DEPTH - 1
        i = l // 2
        m = mods[l]
        xa = _ffn(xa, (m[0], m[1], m[2]), ffn1_w_in[l].astype(BF16), ffn1_w_out[l].astype(BF16),
                  n_latent=n_latent, n_rows=t_all, tm=tm_ffn, n_chunks=2)
        n_rows = n_latent if last else t_all
        tm_out = tm_row_lat if last else tm_row
        if l % 2 == 0:
            wts = _even_weights(even_w_in[i], gla_wg2_f[i], gla_bg_f[i], gla_wg2_b[i], gla_bg_b[i],
                                mla_q_norm_g[i], mla_kv_norm_g[i], mla_w_uq[i], mla_w_ukv[i])
            gq, gk, gv, r, gf, gb, qm, km, vm = _even_proj(xa, (m[3], m[4]), wts, rope_c, rope_s,
                                                           n_latent=n_latent, tm=tm_row)
            o_f, o_b = _gla(gq, gk, gv, gf, gb, n_latent=n_latent, chunk=GLA_CHUNK)
            mla_o = _flash(qm, km, vm, None, n_heads=MLA_HEADS, dq=2 * LANES, dv=MLA_D_V, tq=tq, tk=t_all, ck=ck,
                           q_blk0=0, n_q=n_latent // tq, k_blk0=0, n_k=1)
            if not last:
                mla_o = _flash(qm, km, vm, mla_o, n_heads=MLA_HEADS, dq=2 * LANES, dv=MLA_D_V,
                               tq=n_ctx, tk=n_ctx, ck=ck_ctx, q_blk0=n_latent // n_ctx, n_q=1,
                               k_blk0=n_latent // n_ctx, n_k=1)
            w_out = even_w_out[i].astype(BF16)
            gv_w = GLA_HEADS * GLA_DV
            xa = _even_out(xa, o_f, o_b, r, gla_norm_g[i][None], mla_o, w_out[:gv_w], w_out[gv_w:], m[5],
                           n_latent=n_latent, n_rows=n_rows, tm=tm_out)
        else:
            qkv = _qkv_proj(xa, (m[3], m[4]), na_w_in[i].astype(BF16), n_latent=n_latent, tm=tm_row)
            bias = _na_bias_tables(na_rpb[i], n_latent // GRID_W)
            att = _na_attention(qkv, bias, n_latent=n_latent)
            if not last:
                att = _ctx_pair_attention(qkv, att, n_latent=n_latent)
            xa = _odd_out(xa, att, na_w_out[i].astype(BF16), m[5], n_latent=n_latent, n_rows=n_rows, tm=tm_out)
        xa = _ffn(xa, (m[6], m[7], m[8]), ffn2_w_in[l].astype(BF16), ffn2_w_out[l].astype(BF16),
                  n_latent=n_latent, n_rows=n_rows, tm=tm_ffn_lat if last else tm_ffn, n_chunks=2)
    return xa
```

```python
import functools

import numpy as np
import jax
import jax.numpy as jnp
from jax import lax
from jax.experimental import pallas as pl
from jax.experimental.pallas import tpu as pltpu

DEPTH = 4
GRID_W = 64
D_FF = 2816
GLA_HEADS = 4
GLA_DK = 64
GLA_DV = 128
GLA_RANK = 16
GLA_TAU = 16.0
MLA_HEADS = 4
MLA_D_NOPE = 128
MLA_D_ROPE = 64
MLA_D_V = 128
MLA_D_CQ = 384
MLA_D_CKV = 128
MLA_SCALE = (MLA_D_NOPE + MLA_D_ROPE) ** -0.5
LOG2_E = 1.4426950408889634
MLA_QSCALE = MLA_SCALE * LOG2_E
NA_HEADS = 16
NA_HEAD_DIM = 64
NA_KH = 8
NA_KW = 16
ROPE_BASE = 10000.0
EPS = 1e-6
ALPHA = (2 * DEPTH) ** 0.25

LANES = 128
VMEM_LIMIT = 56 * 1024 * 1024

GLA_CHUNK = 128
NA_ROWS = 8
NA_UNIT = 4
MASK_NEG = -1e30
NA_COL_GROUPS = ((0, 24, 0), (24, 16, 16), (40, 24, 32))
NA_GROUP_KEYS = 32

BF16 = jnp.bfloat16
F32 = jnp.float32


def _cparams(sem):
    return pltpu.CompilerParams(dimension_semantics=sem, vmem_limit_bytes=VMEM_LIMIT)


def _dot(a, b):
    return jnp.dot(a, b, preferred_element_type=F32)


def _dot_nt(a, b):
    return lax.dot_general(a, b, (((1,), (1,)), ((), ())), preferred_element_type=F32)


def _dot_tn(a, b):
    return lax.dot_general(a, b, (((0,), (0,)), ((), ())), preferred_element_type=F32)


def _silu(v):
    return v * (1.0 / (1.0 + jnp.exp(-v)))


def _row_select(is_ctx, lat_ref, ctx_ref):
    return jnp.where(is_ctx, ctx_ref[0], lat_ref[0])


def _is_ctx_rows(tile_idx, tm, n_latent):
    rows = tile_idx * tm + lax.broadcasted_iota(jnp.int32, (tm, 1), 0)
    return rows >= n_latent


def _modulate(x, is_ctx, sh_l, sh_c, sc_l, sc_c):
    shift = _row_select(is_ctx, sh_l, sh_c)
    scale = _row_select(is_ctx, sc_l, sc_c)
    return x * (1.0 + scale) + shift


def _post_norm(x, y, gate, coef):
    z = ALPHA * x + (coef * gate) * y
    mu = jnp.mean(z, axis=-1, keepdims=True)
    zc = z - mu
    var = jnp.mean(zc * zc, axis=-1, keepdims=True)
    return zc * lax.rsqrt(var + EPS)


def _mod_specs(n_batch, d, grid_rank):
    if grid_rank == 2:
        lat = pl.BlockSpec((1, 1, d), lambda b, i: (b, 0, 0))
        ctx = pl.BlockSpec((1, 1, d), lambda b, i: (n_batch, 0, 0))
    else:
        lat = pl.BlockSpec((1, 1, d), lambda b, i, j: (b, 0, 0))
        ctx = pl.BlockSpec((1, 1, d), lambda b, i, j: (n_batch, 0, 0))
    return lat, ctx


def _ada_kernel(c_ref, w_ref, b_ref, o_ref):
    a = _silu(c_ref[...]).astype(BF16)
    o_ref[0] = _dot(a, w_ref[0].astype(BF16)) + b_ref[0]


def _ada_modulation(cc, ada_w, ada_b):
    depth, d, n9 = ada_w.shape
    tn = n9 // 8
    return pl.pallas_call(
        _ada_kernel,
        grid=(depth, n9 // tn),
        in_specs=[pl.BlockSpec((8, d), lambda l, j: (0, 0)),
                  pl.BlockSpec((1, d, tn), lambda l, j: (l, 0, j)),
                  pl.BlockSpec((1, 1, tn), lambda l, j: (l, 0, j))],
        out_specs=pl.BlockSpec((1, 8, tn), lambda l, j: (l, 0, j)),
        out_shape=jax.ShapeDtypeStruct((depth, 8, n9), F32),
        compiler_params=_cparams(("arbitrary", "arbitrary")),
        name="ada_modulation",
    )(cc, ada_w, ada_b.reshape(depth, 1, n9))


def _ffn_kernel(x_ref, shl, shc, scl, scc, gl, gc, wi_ref, wo_ref, o_ref, *, tm, n_latent, n_chunks):
    i = pl.program_id(1)
    is_ctx = _is_ctx_rows(i, tm, n_latent)
    x = x_ref[0]
    h = _modulate(x, is_ctx, shl, shc, scl, scc).astype(BF16)
    d_ff = wo_ref.shape[0]
    tf = d_ff // n_chunks
    y = None
    for c in range(n_chunks):
        gate = _dot(h, wi_ref[:, c * tf:(c + 1) * tf])
        up = _dot(h, wi_ref[:, d_ff + c * tf:d_ff + (c + 1) * tf])
        act = (_silu(gate) * up).astype(BF16)
        part = _dot(act, wo_ref[c * tf:(c + 1) * tf, :])
        y = part if y is None else y + part
    o_ref[0] = _post_norm(x, y, _row_select(is_ctx, gl, gc), 0.5)


def _ffn(x, mods, w_in, w_out, *, n_latent, n_rows, tm, n_chunks):
    n_batch, _, d = x.shape
    sh, sc, g = mods
    lat, ctx = _mod_specs(n_batch, d, 2)
    resident = lambda w: pl.BlockSpec(w.shape, lambda b, i: (0, 0), pipeline_mode=pl.Buffered(1))
    return pl.pallas_call(
        functools.partial(_ffn_kernel, tm=tm, n_latent=n_latent, n_chunks=n_chunks),
        grid=(n_batch, n_rows // tm),
        in_specs=[pl.BlockSpec((1, tm, d), lambda b, i: (b, i, 0)),
                  lat, ctx, lat, ctx, lat, ctx, resident(w_in), resident(w_out)],
        out_specs=pl.BlockSpec((1, tm, d), lambda b, i: (b, i, 0)),
        out_shape=jax.ShapeDtypeStruct((n_batch, n_rows, d), F32),
        compiler_params=_cparams(("arbitrary", "arbitrary")),
        name="ffn_postnorm",
    )(x, sh, sh, sc, sc, g, g, w_in, w_out)


def _even_proj_kernel(x_ref, shl, shc, scl, scc, w1_ref, wg_ref, bg_ref, qng_ref, kvng_ref,
                      wqm_ref, wqs_ref, wkv_ref, cb_ref, sb_ref,
                      gq_ref, gk_ref, gv_ref, r_ref, gf_ref, gb_ref, qm_ref, km_ref, vm_ref,
                      *, tm, n_latent):
    i = pl.program_id(1)
    is_ctx = _is_ctx_rows(i, tm, n_latent)
    h = _modulate(x_ref[0], is_ctx, shl, shc, scl, scc).astype(BF16)
    p = _dot(h, w1_ref[...])
    hp = GLA_HEADS * GLA_DK
    gv_w = GLA_HEADS * GLA_DV
    o0 = 0
    gq_ref[0] = p[:, o0:o0 + hp] * (GLA_DK ** -0.5)
    o0 += hp
    gk_ref[0] = p[:, o0:o0 + hp]
    o0 += hp
    gv_ref[0] = p[:, o0:o0 + gv_w].astype(BF16)
    o0 += gv_w
    r_ref[0] = p[:, o0:o0 + gv_w]
    o0 += gv_w
    lr = p[:, o0:o0 + LANES].astype(BF16)
    o0 += LANES
    cq = p[:, o0:o0 + MLA_D_CQ]
    o0 += MLA_D_CQ
    ckv = p[:, o0:o0 + MLA_D_CKV]
    o0 += MLA_D_CKV
    kr = p[:, o0:o0 + LANES]
    o0 += LANES
    krs = p[:, o0:o0 + LANES]

    z = _dot(lr, wg_ref[...]) + bg_ref[...]
    logg = (jnp.minimum(z, 0.0) - jnp.log1p(jnp.exp(-jnp.abs(z)))) / GLA_TAU
    gf_ref[0] = logg[:, :hp]
    gb_ref[0] = logg[:, hp:]

    cb = cb_ref[...]
    sb = sb_ref[...]
    cqn = (cq * lax.rsqrt(jnp.mean(cq * cq, axis=-1, keepdims=True) + EPS) * qng_ref[...]).astype(BF16)
    qmain = _dot(cqn, wqm_ref[...])
    qswap = _dot(cqn, wqs_ref[...])
    ckn = (ckv * lax.rsqrt(jnp.mean(ckv * ckv, axis=-1, keepdims=True) + EPS) * kvng_ref[...]).astype(BF16)
    kv = _dot(ckn, wkv_ref[...])
    k_rope = (kr * cb + krs * sb).astype(BF16)
    ones_col = (lax.broadcasted_iota(jnp.int32, (1, LANES), 1) == 0).astype(BF16)
    for hd in range(MLA_HEADS):
        b0 = 2 * LANES * hd
        qm_ref[0, :, b0:b0 + LANES] = (qmain[:, b0:b0 + LANES] * MLA_QSCALE).astype(BF16)
        q_rope = qmain[:, b0 + LANES:b0 + 2 * LANES] * cb + qswap[:, LANES * hd:LANES * (hd + 1)] * sb
        qm_ref[0, :, b0 + LANES:b0 + 2 * LANES] = (q_rope * MLA_QSCALE).astype(BF16)
        km_ref[0, :, b0:b0 + LANES] = kv[:, LANES * hd:LANES * (hd + 1)].astype(BF16)
        km_ref[0, :, b0 + LANES:b0 + 2 * LANES] = k_rope
        v0 = MLA_HEADS * MLA_D_NOPE + MLA_D_V * hd
        vm_ref[0, :, b0:b0 + LANES] = kv[:, v0:v0 + MLA_D_V].astype(BF16)
        vm_ref[0, :, b0 + LANES:b0 + 2 * LANES] = jnp.broadcast_to(ones_col, (tm, LANES))


def _even_proj(x, mods, wts, rope_c, rope_s, *, n_latent, tm):
    n_batch, t, d = x.shape
    sh, sc = mods
    lat, ctx = _mod_specs(n_batch, d, 2)
    full = lambda a: pl.BlockSpec(a.shape, lambda b, i: (0,) * a.ndim)
    row = lambda w: pl.BlockSpec((1, tm, w), lambda b, i: (b, i, 0))
    hp = GLA_HEADS * GLA_DK
    gv_w = GLA_HEADS * GLA_DV
    out_w = [(hp, F32), (hp, F32), (gv_w, BF16), (gv_w, F32), (hp, F32), (hp, F32),
             (MLA_HEADS * 2 * LANES, BF16), (MLA_HEADS * 2 * LANES, BF16), (MLA_HEADS * 2 * MLA_D_V, BF16)]
    return pl.pallas_call(
        functools.partial(_even_proj_kernel, tm=tm, n_latent=n_latent),
        grid=(n_batch, t // tm),
        in_specs=[row(d), lat, ctx, lat, ctx] + [full(w) for w in wts]
                 + [pl.BlockSpec((tm, LANES), lambda b, i: (i, 0))] * 2,
        out_specs=[row(w) for w, _ in out_w],
        out_shape=[jax.ShapeDtypeStruct((n_batch, t, w), dt) for w, dt in out_w],
        compiler_params=_cparams(("arbitrary", "arbitrary")),
        name="even_project",
    )(x, sh, sh, sc, sc, *wts, rope_c, rope_s)


def _gla_constants(chunk, reverse):
    n_lvl = int(np.log2(chunk))
    pos = np.arange(chunk)
    src = pos[None, :]
    tri = src <= pos[:, None]
    cum, mask = [], []
    for lvl in range(n_lvl):
        size = chunk >> lvl
        half = size // 2
        ref = ((pos // size) * size + half - 1)[:, None]
        upper = (pos % size) >= half
        a_up = (src > ref) & (src <= pos[:, None])
        a_lo = (src > pos[:, None]) & (src <= ref)
        cum.append(np.where(upper[:, None], a_up, a_lo))
        same = (pos[:, None] // size) == (pos[None, :] // size)
        mask.append(same & upper[:, None] & ~upper[None, :])
    cum += [tri, ~tri]
    cum = np.stack(cum).astype(np.float32)
    mask = np.stack(mask).astype(np.float32)
    if reverse:
        cum = cum[:, ::-1, ::-1]
        mask = mask[:, ::-1, ::-1]
    return np.ascontiguousarray(cum).reshape(-1, chunk), np.ascontiguousarray(mask)


def _gla_direction(q_ref, k_ref, v_ref, g_ref, cum_ref, mask_ref, st_ref, o_ref, *, chunk, reverse):
    n_lvl = mask_ref.shape[0]
    g = g_ref[0]
    g_hi = g.astype(BF16)
    g_lo = (g - g_hi.astype(F32)).astype(BF16)
    cum = cum_ref[...]
    sums = _dot(cum, g_hi) + _dot(cum, g_lo)
    b_rows = slice(n_lvl * chunk, (n_lvl + 1) * chunk)
    rest_rows = slice((n_lvl + 1) * chunk, (n_lvl + 2) * chunk)
    last = 0 if reverse else chunk - 1
    eye = (lax.broadcasted_iota(jnp.int32, (chunk, chunk), 0)
           == lax.broadcasted_iota(jnp.int32, (chunk, chunk), 1)).astype(F32)
    lane_lo = lax.broadcasted_iota(jnp.int32, (1, LANES), 1) < GLA_DK
    heads_per_group = LANES // GLA_DK

    def head_only(x, hh):
        return jnp.where(lane_lo if hh == 0 else jnp.logical_not(lane_lo), x, jnp.zeros_like(x))

    for grp in range(GLA_HEADS // heads_per_group):
        sl = slice(grp * LANES, (grp + 1) * LANES)
        q = q_ref[0, :, sl]
        k = k_ref[0, :, sl]
        att = [jnp.zeros((chunk, chunk), F32) for _ in range(heads_per_group)]
        for lvl in range(n_lvl):
            w = jnp.exp(sums[lvl * chunk:(lvl + 1) * chunk, sl])
            qt = (q * w).astype(BF16)
            kt = (k * w).astype(BF16)
            for hh in range(heads_per_group):
                att[hh] = att[hh] + _dot_nt(head_only(qt, hh), kt) * mask_ref[lvl]
        if not reverse:
            qb, kb = q.astype(BF16), k.astype(BF16)
            for hh in range(heads_per_group):
                att[hh] = att[hh] + _dot_nt(head_only(qb, hh), kb) * eye
        b = sums[b_rows, sl]
        qe = (q * jnp.exp(b)).astype(BF16)
        kd = (k * jnp.exp(sums[rest_rows, sl])).astype(BF16)
        decay = jnp.exp(b[last:last + 1, :])
        for hh in range(heads_per_group):
            hd = grp * heads_per_group + hh
            vs = slice(hd * GLA_DV, (hd + 1) * GLA_DV)
            v = v_ref[0, :, vs]
            st = st_ref[hd]
            o = _dot(att[hh].astype(BF16), v) + _dot_nt(head_only(qe, hh), st.astype(BF16))
            o_ref[0, :, vs] = o
            st_ref[hd] = st * decay + _dot_tn(v, kd)


def _gla_kernel(qf, kf, vf, gf, qb, kb, vb, gb, cumf, maskf, cumb, maskb, of_ref, ob_ref, stf, stb,
                *, chunk):
    @pl.when(pl.program_id(1) == 0)
    def _():
        stf[...] = jnp.zeros_like(stf)
        stb[...] = jnp.zeros_like(stb)

    _gla_direction(qf, kf, vf, gf, cumf, maskf, stf, of_ref, chunk=chunk, reverse=False)
    _gla_direction(qb, kb, vb, gb, cumb, maskb, stb, ob_ref, chunk=chunk, reverse=True)


def _gla(gq, gk, gv, gf, gb, *, n_latent, chunk):
    n_batch, t, hp = gq.shape
    n_lat = n_latent // chunk
    n_ctx = (t - n_latent) // chunk
    n_steps = n_lat + n_ctx

    def fwd(b, s):
        return (b, jnp.where(s < n_ctx, n_lat + s, s - n_ctx), 0)

    def bwd(b, s):
        return (b, n_steps - 1 - s, 0)

    cumf, maskf = _gla_constants(chunk, False)
    cumb, maskb = _gla_constants(chunk, True)
    consts = [jnp.asarray(cumf, BF16), jnp.asarray(maskf), jnp.asarray(cumb, BF16), jnp.asarray(maskb)]
    full = lambda a: pl.BlockSpec(a.shape, lambda b, s: (0,) * a.ndim)
    blk = lambda w, im: pl.BlockSpec((1, chunk, w), im)
    gv_w = gv.shape[-1]
    return pl.pallas_call(
        functools.partial(_gla_kernel, chunk=chunk),
        grid=(n_batch, n_steps),
        in_specs=[blk(hp, fwd), blk(hp, fwd), blk(gv_w, fwd), blk(hp, fwd),
                  blk(hp, bwd), blk(hp, bwd), blk(gv_w, bwd), blk(hp, bwd)] + [full(a) for a in consts],
        out_specs=[blk(gv_w, fwd), blk(gv_w, bwd)],
        out_shape=[jax.ShapeDtypeStruct((n_batch, t, gv_w), F32)] * 2,
        scratch_shapes=[pltpu.VMEM((GLA_HEADS, GLA_DV, LANES), F32)] * 2,
        compiler_params=_cparams(("arbitrary", "arbitrary")),
        name="gla_bidirectional",
    )(gq, gk, gv, gf, gq, gk, gv, gb, *consts)


def _flash_kernel(q_ref, k_ref, v_ref, o_ref, m_ref, acc_ref, *, ck, dv):
    j = pl.program_id(3)
    n_sub = k_ref.shape[1] // ck

    @pl.when(j == 0)
    def _():
        m_ref[...] = jnp.full_like(m_ref, -jnp.inf)
        acc_ref[...] = jnp.zeros_like(acc_ref)

    q = q_ref[0]

    for c in range(n_sub):
        off = c * ck
        s = _dot_nt(q, k_ref[0, pl.ds(off, ck), :])
        m_prev = m_ref[...]
        m_new = jnp.maximum(m_prev, jnp.max(s, axis=-1, keepdims=True))
        alpha = jnp.exp2(m_prev - m_new)
        p = jnp.concatenate([jnp.exp2(s[:, u * LANES:(u + 1) * LANES] - m_new)
                             for u in range(ck // LANES)], axis=-1).astype(BF16)
        pv = _dot(p, v_ref[0, pl.ds(off, ck), :])
        acc_ref[...] = jnp.concatenate([alpha] * (acc_ref.shape[1] // LANES), axis=-1) * acc_ref[...] + pv
        m_ref[...] = m_new

    @pl.when(j == pl.num_programs(3) - 1)
    def _():
        acc = acc_ref[...]
        denom = jnp.sum(acc[:, dv:], axis=-1, keepdims=True)
        o_ref[0] = (acc[:, :dv] / denom).astype(o_ref.dtype)


def _flash(q, k, v, prev_out, *, n_heads, dq, dv, tq, tk, ck, q_blk0, n_q, k_blk0, n_k):
    n_batch, t, _ = q.shape
    in_specs = [pl.BlockSpec((1, tq, dq), lambda b, h, i, j: (b, q_blk0 + i, h)),
                pl.BlockSpec((1, tk, dq), lambda b, h, i, j: (b, k_blk0 + j, h)),
                pl.BlockSpec((1, tk, 2 * dv), lambda b, h, i, j: (b, k_blk0 + j, h))]
    args = [q, k, v]
    aliases = {}
    base = functools.partial(_flash_kernel, ck=ck, dv=dv)
    kern = base
    if prev_out is not None:
        in_specs.append(pl.BlockSpec(memory_space=pl.ANY))
        args.append(prev_out)
        aliases = {3: 0}
        kern = lambda q_, k_, v_, _prev, *rest: base(q_, k_, v_, *rest)
    return pl.pallas_call(
        kern,
        grid=(n_batch, n_heads, n_q, n_k),
        in_specs=in_specs,
        out_specs=pl.BlockSpec((1, tq, dv), lambda b, h, i, j: (b, q_blk0 + i, h)),
        out_shape=jax.ShapeDtypeStruct((n_batch, t, n_heads * dv), BF16),
        scratch_shapes=[pltpu.VMEM((tq, LANES), F32), pltpu.VMEM((tq, 2 * dv), F32)],
        input_output_aliases=aliases,
        compiler_params=_cparams(("arbitrary",) * 4),
        name="flash_attention",
    )(*args)


def _even_out_kernel(x_ref, of_ref, ob_ref, r_ref, ng_ref, mo_ref, wa_ref, wb_ref, gl, gc, o_ref,
                     *, tm, n_latent):
    i = pl.program_id(1)
    is_ctx = _is_ctx_rows(i, tm, n_latent)
    o = of_ref[0] + ob_ref[0]
    r = r_ref[0]
    parts = []
    for hd in range(GLA_HEADS):
        sl = slice(hd * GLA_DV, (hd + 1) * GLA_DV)
        oh = o[:, sl]
        y = oh * lax.rsqrt(jnp.mean(oh * oh, axis=-1, keepdims=True) + EPS) * ng_ref[...]
        parts.append((y * _silu(r[:, sl])).astype(BF16))
    a = jnp.concatenate(parts, axis=-1)
    y = _dot(a, wa_ref[...]) + _dot(mo_ref[0], wb_ref[...])
    o_ref[0] = _post_norm(x_ref[0], y, _row_select(is_ctx, gl, gc), 1.0)


def _even_out(x, o_f, o_b, r, norm_g, mla_o, wa, wb, gate, *, n_latent, n_rows, tm):
    n_batch, _, d = x.shape
    lat, ctx = _mod_specs(n_batch, d, 2)
    row = lambda w: pl.BlockSpec((1, tm, w), lambda b, i: (b, i, 0))
    full = lambda a: pl.BlockSpec(a.shape, lambda b, i: (0,) * a.ndim)
    gv_w = o_f.shape[-1]
    return pl.pallas_call(
        functools.partial(_even_out_kernel, tm=tm, n_latent=n_latent),
        grid=(n_batch, n_rows // tm),
        in_specs=[row(d), row(gv_w), row(gv_w), row(gv_w), full(norm_g), row(mla_o.shape[-1]),
                  full(wa), full(wb), lat, ctx],
        out_specs=row(d),
        out_shape=jax.ShapeDtypeStruct((n_batch, n_rows, d), F32),
        compiler_params=_cparams(("arbitrary", "arbitrary")),
        name="even_out_postnorm",
    )(x, o_f, o_b, r, norm_g, mla_o, wa, wb, gate, gate)


def _qkv_kernel(x_ref, shl, shc, scl, scc, w_ref, o_ref, *, tm, n_latent, q_width, q_scale):
    i = pl.program_id(1)
    is_ctx = _is_ctx_rows(i, tm, n_latent)
    h = _modulate(x_ref[0], is_ctx, shl, shc, scl, scc).astype(BF16)
    p = _dot(h, w_ref[...])
    o_ref[0, :, :q_width] = (p[:, :q_width] * q_scale).astype(BF16)
    o_ref[0, :, q_width:] = p[:, q_width:].astype(BF16)


def _qkv_proj(x, mods, w, *, n_latent, tm):
    n_batch, t, d = x.shape
    sh, sc = mods
    lat, ctx = _mod_specs(n_batch, d, 2)
    n_out = w.shape[1]
    return pl.pallas_call(
        functools.partial(_qkv_kernel, tm=tm, n_latent=n_latent, q_width=n_out // 3,
                          q_scale=NA_HEAD_DIM ** -0.5 * LOG2_E),
        grid=(n_batch, t // tm),
        in_specs=[pl.BlockSpec((1, tm, d), lambda b, i: (b, i, 0)), lat, ctx, lat, ctx,
                  pl.BlockSpec(w.shape, lambda b, i: (0, 0))],
        out_specs=pl.BlockSpec((1, tm, n_out), lambda b, i: (b, i, 0)),
        out_shape=jax.ShapeDtypeStruct((n_batch, t, n_out), BF16),
        compiler_params=_cparams(("arbitrary", "arbitrary")),
        name="qkv_project",
    )(x, sh, sh, sc, sc, w)


def _pair_softmax_av(q, lane_lo, scores_fn, av_fn):
    outs = []
    for half in range(2):
        keep = lane_lo if half == 0 else jnp.logical_not(lane_lo)
        qh = jnp.where(keep, q, jnp.zeros_like(q))
        s_list = scores_fn(qh, half)
        m = s_list[0].max(axis=-1, keepdims=True)
        for s in s_list[1:]:
            m = jnp.maximum(m, s.max(axis=-1, keepdims=True))
        p_list = [jnp.exp2(s - m) for s in s_list]
        denom = p_list[0].sum(axis=-1, keepdims=True)
        for p in p_list[1:]:
            denom = denom + p.sum(axis=-1, keepdims=True)
        outs.append(av_fn([p.astype(BF16) for p in p_list]) / denom)
    return jnp.where(lane_lo, outs[0], outs[1])


def _na_kernel(q_ref, k0, k1, k2, k3, v0, v1, v2, v3, kc_ref, vc_ref, bias_ref, o_ref, *, n_blocks):
    rb = pl.program_id(2)
    kind = jnp.where(rb == 0, 0, jnp.where(rb == n_blocks - 1, 2, 1))
    q = q_ref[0].astype(F32)
    k_win = jnp.concatenate([k0[0], k1[0], k2[0], k3[0]], axis=0)
    v_win = jnp.concatenate([v0[0], v1[0], v2[0], v3[0]], axis=0)
    n_keyrows = k_win.shape[0] // GRID_W
    lane_lo = lax.broadcasted_iota(jnp.int32, (1, LANES), 1) < NA_HEAD_DIM
    ones_col = (lax.broadcasted_iota(jnp.int32, (1, LANES), 1) == 0).astype(BF16)

    def with_ones(v):
        return jnp.concatenate([v, jnp.broadcast_to(ones_col, v.shape)], axis=-1)

    k_ctx = kc_ref[0]
    v_ctx = with_ones(vc_ref[0])
    pieces = [[None] * len(NA_COL_GROUPS) for _ in range(NA_ROWS)]
    row0 = 0
    for g, (c0, gw, kc0) in enumerate(NA_COL_GROUPS):
        mg = NA_ROWS * gw
        qg = jnp.concatenate([q[i * GRID_W + c0:i * GRID_W + c0 + gw] for i in range(NA_ROWS)], axis=0)
        qs = jnp.concatenate([jnp.where(lane_lo, qg, 0.0), jnp.where(lane_lo, 0.0, qg)], axis=0).astype(BF16)
        kg = jnp.concatenate([k_win[t * GRID_W + kc0:t * GRID_W + kc0 + NA_GROUP_KEYS] for t in range(n_keyrows)], axis=0)
        vg = jnp.concatenate([v_win[t * GRID_W + kc0:t * GRID_W + kc0 + NA_GROUP_KEYS] for t in range(n_keyrows)], axis=0)
        s_w = _dot_nt(qs, kg) + bias_ref[kind, 0, pl.ds(row0, 2 * mg), :]
        s_c = _dot_nt(qs, k_ctx)
        m = jnp.maximum(jnp.max(s_w, axis=-1, keepdims=True), jnp.max(s_c, axis=-1, keepdims=True))
        pv = _dot(jnp.exp2(s_w - m).astype(BF16), with_ones(vg)) + _dot(jnp.exp2(s_c - m).astype(BF16), v_ctx)
        o = pv[:, :LANES] / jnp.sum(pv[:, LANES:], axis=-1, keepdims=True)
        og = jnp.where(lane_lo, o[:mg], o[mg:])
        for i in range(NA_ROWS):
            pieces[i][g] = og[i * gw:(i + 1) * gw]
        row0 += 2 * mg
    o_ref[0] = jnp.concatenate([pc for row in pieces for pc in row], axis=0).astype(o_ref.dtype)


def _na_bias_tables(rpb, rows):
    n_keyrows = NA_ROWS + 2 * NA_UNIT
    col = np.arange(GRID_W)
    col_start = np.clip(col - NA_KW // 2, 0, GRID_W - NA_KW)
    kc = np.arange(GRID_W)
    col_ok = (kc[None, :] >= col_start[:, None]) & (kc[None, :] < col_start[:, None] + NA_KW)
    rel_col = np.clip(kc[None, :] - col[:, None] + (NA_KW - 1), 0, 2 * NA_KW - 2)
    col_sel = ((rel_col[:, :, None] == np.arange(2 * NA_KW - 1)) & col_ok[:, :, None]).astype(np.float32)
    row_sel, row_oks = [], []
    n_blocks = rows // NA_ROWS
    for blk in (0, min(1, n_blocks - 1), n_blocks - 1):
        r = blk * NA_ROWS + np.arange(NA_ROWS)
        kh = min(NA_KH, rows)
        r_start = np.clip(r - kh // 2, 0, rows - kh)
        key_row0 = np.clip(blk * NA_ROWS - NA_UNIT, 0, rows - n_keyrows)
        kr = key_row0 + np.arange(n_keyrows)
        row_ok = (kr[None, :] >= r_start[:, None]) & (kr[None, :] < r_start[:, None] + kh)
        rel_row = np.clip(kr[None, :] - r[:, None] + (NA_KH - 1), 0, 2 * NA_KH - 2)
        row_sel.append(((rel_row[:, :, None] == np.arange(2 * NA_KH - 1)) & row_ok[:, :, None]).astype(np.float32))
        row_oks.append(row_ok)
    row_sel = jnp.asarray(np.stack(row_sel))
    row_oks = np.stack(row_oks)
    n_heads = rpb.shape[0]
    by_col = jnp.einsum('hab,ckb->hack', rpb, jnp.asarray(col_sel), precision=lax.Precision.HIGHEST)
    parts = []
    for c0, gw, kc0 in NA_COL_GROUPS:
        sub = by_col[:, :, c0:c0 + gw, kc0:kc0 + NA_GROUP_KEYS]
        bias = jnp.einsum('zita,hack->zhictk', row_sel, sub, precision=lax.Precision.HIGHEST)
        ok = row_oks[:, :, None, :, None] & col_ok[None, None, c0:c0 + gw, None, kc0:kc0 + NA_GROUP_KEYS]
        bias = jnp.where(jnp.asarray(ok)[:, None], bias * LOG2_E, MASK_NEG)
        parts.append(bias.reshape(3, n_heads // 2, 2 * NA_ROWS * gw, n_keyrows * NA_GROUP_KEYS))
    return jnp.concatenate(parts, axis=2)


def _na_attention(qkv, bias, *, n_latent):
    n_batch, t, _ = qkv.shape
    rows = n_latent // GRID_W
    n_blocks = rows // NA_ROWS
    n_pairs = NA_HEADS // 2
    tq = NA_ROWS * GRID_W
    tu = NA_UNIT * GRID_W
    n_units = rows // NA_UNIT
    ctx_blk = n_latent // tu

    def unit(u):
        def im(p, b, rb):
            u0 = jnp.clip(rb * (NA_ROWS // NA_UNIT) - 1, 0, n_units - 4)
            return u0 + u
        return im

    k_specs = [pl.BlockSpec((1, tu, LANES), lambda p, b, rb, f=unit(u): (b, f(p, b, rb), n_pairs + p))
               for u in range(4)]
    v_specs = [pl.BlockSpec((1, tu, LANES), lambda p, b, rb, f=unit(u): (b, f(p, b, rb), 2 * n_pairs + p))
               for u in range(4)]
    return pl.pallas_call(
        functools.partial(_na_kernel, n_blocks=n_blocks),
        grid=(n_pairs, n_batch, n_blocks),
        in_specs=[pl.BlockSpec((1, tq, LANES), lambda p, b, rb: (b, rb, p))] + k_specs + v_specs
                 + [pl.BlockSpec((1, tu, LANES), lambda p, b, rb: (b, ctx_blk, n_pairs + p)),
                    pl.BlockSpec((1, tu, LANES), lambda p, b, rb: (b, ctx_blk, 2 * n_pairs + p)),
                    pl.BlockSpec((3, 1) + bias.shape[2:], lambda p, b, rb: (0, p, 0, 0))],
        out_specs=pl.BlockSpec((1, tq, LANES), lambda p, b, rb: (b, rb, p)),
        out_shape=jax.ShapeDtypeStruct((n_batch, t, NA_HEADS * NA_HEAD_DIM), BF16),
        compiler_params=_cparams(("arbitrary",) * 3),
        name="neighbourhood_attention",
    )(qkv, *([qkv] * 10), bias)


def _ctx_pair_kernel(q_ref, k_ref, v_ref, _prev, o_ref):
    k = k_ref[0]
    v = v_ref[0]
    lane_lo = lax.broadcasted_iota(jnp.int32, (1, LANES), 1) < NA_HEAD_DIM
    o_ref[0] = _pair_softmax_av(q_ref[0], lane_lo, lambda qh, half: [_dot_nt(qh, k)],
                                lambda ps: _dot(ps[0], v)).astype(o_ref.dtype)


def _ctx_pair_attention(qkv, prev_out, *, n_latent):
    n_batch, t, _ = qkv.shape
    n_ctx = t - n_latent
    blk = n_latent // n_ctx
    n_pairs = NA_HEADS // 2
    spec = lambda off: pl.BlockSpec((1, n_ctx, LANES), lambda b, p: (b, blk, off + p))
    return pl.pallas_call(
        _ctx_pair_kernel,
        grid=(n_batch, n_pairs),
        in_specs=[spec(0), spec(n_pairs), spec(2 * n_pairs), pl.BlockSpec(memory_space=pl.ANY)],
        out_specs=spec(0),
        out_shape=jax.ShapeDtypeStruct(prev_out.shape, prev_out.dtype),
        input_output_aliases={3: 0},
        compiler_params=_cparams(("arbitrary", "arbitrary")),
        name="context_pair_attention",
    )(qkv, qkv, qkv, prev_out)


def _odd_out_kernel(x_ref, a_ref, w_ref, gl, gc, o_ref, *, tm, n_latent):
    i = pl.program_id(1)
    is_ctx = _is_ctx_rows(i, tm, n_latent)
    y = _dot(a_ref[0], w_ref[...])
    o_ref[0] = _post_norm(x_ref[0], y, _row_select(is_ctx, gl, gc), 1.0)


def _odd_out(x, a, w, gate, *, n_latent, n_rows, tm):
    n_batch, _, d = x.shape
    lat, ctx = _mod_specs(n_batch, d, 2)
    row = lambda wd: pl.BlockSpec((1, tm, wd), lambda b, i: (b, i, 0))
    return pl.pallas_call(
        functools.partial(_odd_out_kernel, tm=tm, n_latent=n_latent),
        grid=(n_batch, n_rows // tm),
        in_specs=[row(d), row(a.shape[-1]), pl.BlockSpec(w.shape, lambda b, i: (0, 0)), lat, ctx],
        out_specs=row(d),
        out_shape=jax.ShapeDtypeStruct((n_batch, n_rows, d), F32),
        compiler_params=_cparams(("arbitrary", "arbitrary")),
        name="odd_out_postnorm",
    )(x, a, w, gate, gate)


def _even_weights(w_in, wg2_f, bg_f, wg2_b, bg_b, q_norm_g, kv_norm_g, w_uq, w_ukv):
    sizes = (GLA_HEADS * GLA_DK, GLA_HEADS * GLA_DK, GLA_HEADS * GLA_DV, GLA_HEADS * GLA_DV,
             GLA_RANK, GLA_RANK, MLA_D_CQ, MLA_D_CKV, MLA_D_ROPE)
    q_g, k_g, v_g, r_g, lr_f, lr_b, c_q, c_kv, k_r = jnp.split(w_in, np.cumsum(sizes)[:-1].tolist(), axis=-1)
    swap = np.arange(MLA_D_ROPE) ^ 1
    d = w_in.shape[0]
    zeros = lambda n: jnp.zeros((d, n), w_in.dtype)
    w1 = jnp.concatenate([
        q_g, k_g, v_g, r_g,
        lr_f, lr_b, zeros(LANES - 2 * GLA_RANK), c_q, c_kv,
        k_r, zeros(LANES - MLA_D_ROPE), k_r[:, swap], zeros(LANES - MLA_D_ROPE)], axis=-1).astype(BF16)
    hp = GLA_HEADS * GLA_DK
    wg = jnp.zeros((LANES, 2 * hp), F32)
    wg = wg.at[:GLA_RANK, :hp].set(wg2_f)
    wg = wg.at[GLA_RANK:2 * GLA_RANK, hp:].set(wg2_b)
    bg = jnp.concatenate([bg_f, bg_b])[None]
    uq = w_uq.reshape(MLA_D_CQ, MLA_HEADS, MLA_D_NOPE + MLA_D_ROPE)
    nope, rope = uq[..., :MLA_D_NOPE], uq[..., MLA_D_NOPE:]
    zr = jnp.zeros((MLA_D_CQ, MLA_HEADS, LANES - MLA_D_ROPE), w_uq.dtype)
    wq_main = jnp.concatenate([nope, rope, zr], axis=-1).reshape(MLA_D_CQ, -1).astype(BF16)
    wq_swap = jnp.concatenate([rope[..., swap], zr], axis=-1).reshape(MLA_D_CQ, -1).astype(BF16)
    ukv = w_ukv.reshape(MLA_D_CKV, MLA_HEADS, MLA_D_NOPE + MLA_D_V)
    wkv = jnp.concatenate([ukv[..., :MLA_D_NOPE].reshape(MLA_D_CKV, -1),
                           ukv[..., MLA_D_NOPE:].reshape(MLA_D_CKV, -1)], axis=-1).astype(BF16)
    return (w1, wg.astype(BF16), bg, q_norm_g[None], kv_norm_g[None], wq_main, wq_swap, wkv)


def _rope_tables(n_latent, n_ctx):
    t = jnp.arange(n_latent)
    row = (t // GRID_W).astype(F32)
    col = (t % GRID_W).astype(F32)
    n_freq = MLA_D_ROPE // 4
    inv = ROPE_BASE ** (-jnp.arange(n_freq, dtype=F32) / n_freq)
    ang = jnp.concatenate([row[:, None] * inv, col[:, None] * inv], axis=-1)
    cos = jnp.repeat(jnp.cos(ang), 2, axis=-1)
    sin = jnp.repeat(jnp.sin(ang), 2, axis=-1) * jnp.tile(jnp.array([-1.0, 1.0], F32), MLA_D_ROPE // 2)
    cos = jnp.concatenate([cos, jnp.ones((n_ctx, MLA_D_ROPE), F32)])
    sin = jnp.concatenate([sin, jnp.zeros((n_ctx, MLA_D_ROPE), F32)])
    pad = jnp.zeros((n_latent + n_ctx, LANES - MLA_D_ROPE), F32)
    return jnp.concatenate([cos, pad], axis=-1), jnp.concatenate([sin, pad], axis=-1)


def _row_tile(n_rows, target):
    best = 8
    for cand in range(8, target + 1, 8):
        if n_rows % cand == 0:
            best = cand
    return best


def kernel(x, c, ctx, c_ctx, ada_w, ada_b, ffn1_w_in, ffn1_w_out, ffn2_w_in, ffn2_w_out, even_w_in, gla_wg2_f, gla_bg_f, gla_wg2_b, gla_bg_b, gla_norm_g, mla_q_norm_g, mla_kv_norm_g, mla_w_uq, mla_w_ukv, even_w_out, na_w_in, na_rpb, na_w_out):
    n_batch, n_latent, d = x.shape
    n_ctx = ctx.shape[1]
    t_all = n_latent + n_ctx
    assert n_batch + 1 <= 8 and n_latent % (NA_ROWS * GRID_W) == 0 and n_ctx == NA_UNIT * GRID_W
    assert n_latent % GLA_CHUNK == 0 and n_ctx % GLA_CHUNK == 0

    xa = jnp.concatenate([x, ctx], axis=1)
    cc = jnp.concatenate([c, c_ctx[None], jnp.zeros((7 - n_batch, d), F32)], axis=0)
    mods = _ada_modulation(cc, ada_w, ada_b)
    mods = mods[:, :n_batch + 1].reshape(DEPTH, n_batch + 1, 9, 1, d).transpose(0, 2, 1, 3, 4)

    tm_ffn = _row_tile(t_all, 768)
    tm_ffn_lat = _row_tile(n_latent, 768)
    tm_row = _row_tile(t_all, 256)
    tm_row_lat = _row_tile(n_latent, 256)
    rope_c, rope_s = _rope_tables(n_latent, n_ctx)
    tq = _row_tile(n_latent, 1024)
    ck = next(cand for cand in (768, 512, 256, 128) if t_all % cand == 0)
    ck_ctx = next(cand for cand in (768, 512, 256, 128) if n_ctx % cand == 0)

    for l in range(DEPTH):
        last = l == DEPTH - 1
        i = l // 2
        m = mods[l]
        xa = _ffn(xa, (m[0], m[1], m[2]), ffn1_w_in[l].astype(BF16), ffn1_w_out[l].astype(BF16),
                  n_latent=n_latent, n_rows=t_all, tm=tm_ffn, n_chunks=2)
        n_rows = n_latent if last else t_all
        tm_out = tm_row_lat if last else tm_row
        if l % 2 == 0:
            wts = _even_weights(even_w_in[i], gla_wg2_f[i], gla_bg_f[i], gla_wg2_b[i], gla_bg_b[i],
                                mla_q_norm_g[i], mla_kv_norm_g[i], mla_w_uq[i], mla_w_ukv[i])
            gq, gk, gv, r, gf, gb, qm, km, vm = _even_proj(xa, (m[3], m[4]), wts, rope_c, rope_s,
                                                           n_latent=n_latent, tm=tm_row)
            o_f, o_b = _gla(gq, gk, gv, gf, gb, n_latent=n_latent, chunk=GLA_CHUNK)
            mla_o = _flash(qm, km, vm, None, n_heads=MLA_HEADS, dq=2 * LANES, dv=MLA_D_V, tq=tq, tk=t_all, ck=ck,
                           q_blk0=0, n_q=n_latent // tq, k_blk0=0, n_k=1)
            if not last:
                mla_o = _flash(qm, km, vm, mla_o, n_heads=MLA_HEADS, dq=2 * LANES, dv=MLA_D_V,
                               tq=n_ctx, tk=n_ctx, ck=ck_ctx, q_blk0=n_latent // n_ctx, n_q=1,
                               k_blk0=n_latent // n_ctx, n_k=1)
            w_out = even_w_out[i].astype(BF16)
            gv_w = GLA_HEADS * GLA_DV
            xa = _even_out(xa, o_f, o_b, r, gla_norm_g[i][None], mla_o, w_out[:gv_w], w_out[gv_w:], m[5],
                           n_latent=n_latent, n_rows=n_rows, tm=tm_out)
        else:
            qkv = _qkv_proj(xa, (m[3], m[4]), na_w_in[i].astype(BF16), n_latent=n_latent, tm=tm_row)
            bias = _na_bias_tables(na_rpb[i], n_latent // GRID_W)
            att = _na_attention(qkv, bias, n_latent=n_latent)
            if not last:
                att = _ctx_pair_attention(qkv, att, n_latent=n_latent)
            xa = _odd_out(xa, att, na_w_out[i].astype(BF16), m[5], n_latent=n_latent, n_rows=n_rows, tm=tm_out)
        xa = _ffn(xa, (m[6], m[7], m[8]), ffn2_w_in[l].astype(BF16), ffn2_w_out[l].astype(BF16),
                  n_latent=n_latent, n_rows=n_rows, tm=tm_ffn_lat if last else tm_ffn, n_chunks=2)
    return xa
```

```python
import functools

import numpy as np
import jax
import jax.numpy as jnp
from jax import lax
from jax.experimental import pallas as pl
from jax.experimental.pallas import tpu as pltpu

DEPTH = 4
GRID_W = 64
D_FF = 2816
GLA_HEADS = 4
GLA_DK = 64
GLA_DV = 128
GLA_RANK = 16
GLA_TAU = 16.0
MLA_HEADS = 4
MLA_D_NOPE = 128
MLA_D_ROPE = 64
MLA_D_V = 128
MLA_D_CQ = 384
MLA_D_CKV = 128
MLA_SCALE = (MLA_D_NOPE + MLA_D_ROPE) ** -0.5
LOG2_E = 1.4426950408889634
MLA_QSCALE = MLA_SCALE * LOG2_E
NA_HEADS = 16
NA_HEAD_DIM = 64
NA_KH = 8
NA_KW = 16
ROPE_BASE = 10000.0
EPS = 1e-6
ALPHA = (2 * DEPTH) ** 0.25

LANES = 128
VMEM_LIMIT = 56 * 1024 * 1024

GLA_CHUNK = 128
NA_ROWS = 8
NA_UNIT = 4
MASK_NEG = -1e30
NA_COL_GROUPS = ((0, 24, 0), (24, 16, 16), (40, 24, 32))
NA_GROUP_KEYS = 32

BF16 = jnp.bfloat16
F32 = jnp.float32


def _cparams(sem):
    return pltpu.CompilerParams(dimension_semantics=sem, vmem_limit_bytes=VMEM_LIMIT)


def _dot(a, b):
    return jnp.dot(a, b, preferred_element_type=F32)


def _dot_nt(a, b):
    return lax.dot_general(a, b, (((1,), (1,)), ((), ())), preferred_element_type=F32)


def _dot_tn(a, b):
    return lax.dot_general(a, b, (((0,), (0,)), ((), ())), preferred_element_type=F32)


def _silu(v):
    return v * (1.0 / (1.0 + jnp.exp(-v)))


def _row_select(is_ctx, lat_ref, ctx_ref):
    return jnp.where(is_ctx, ctx_ref[0], lat_ref[0])


def _is_ctx_rows(tile_idx, tm, n_latent):
    rows = tile_idx * tm + lax.broadcasted_iota(jnp.int32, (tm, 1), 0)
    return rows >= n_latent


def _modulate(x, is_ctx, sh_l, sh_c, sc_l, sc_c):
    shift = _row_select(is_ctx, sh_l, sh_c)
    scale = _row_select(is_ctx, sc_l, sc_c)
    return x * (1.0 + scale) + shift


def _post_norm(x, y, gate, coef):
    z = ALPHA * x + (coef * gate) * y
    mu = jnp.mean(z, axis=-1, keepdims=True)
    zc = z - mu
    var = jnp.mean(zc * zc, axis=-1, keepdims=True)
    return zc * lax.rsqrt(var + EPS)


def _mod_specs(n_batch, d, grid_rank):
    if grid_rank == 2:
        lat = pl.BlockSpec((1, 1, d), lambda b, i: (b, 0, 0))
        ctx = pl.BlockSpec((1, 1, d), lambda b, i: (n_batch, 0, 0))
    else:
        lat = pl.BlockSpec((1, 1, d), lambda b, i, j: (b, 0, 0))
        ctx = pl.BlockSpec((1, 1, d), lambda b, i, j: (n_batch, 0, 0))
    return lat, ctx


def _ada_kernel(c_ref, w_ref, b_ref, o_ref):
    a = _silu(c_ref[...]).astype(BF16)
    o_ref[0] = _dot(a, w_ref[0].astype(BF16)) + b_ref[0]


def _ada_modulation(cc, ada_w, ada_b):
    depth, d, n9 = ada_w.shape
    tn = n9 // 8
    return pl.pallas_call(
        _ada_kernel,
        grid=(depth, n9 // tn),
        in_specs=[pl.BlockSpec((8, d), lambda l, j: (0, 0)),
                  pl.BlockSpec((1, d, tn), lambda l, j: (l, 0, j)),
                  pl.BlockSpec((1, 1, tn), lambda l, j: (l, 0, j))],
        out_specs=pl.BlockSpec((1, 8, tn), lambda l, j: (l, 0, j)),
        out_shape=jax.ShapeDtypeStruct((depth, 8, n9), F32),
        compiler_params=_cparams(("arbitrary", "arbitrary")),
        name="ada_modulation",
    )(cc, ada_w, ada_b.reshape(depth, 1, n9))


def _ffn_kernel(x_ref, shl, shc, scl, scc, gl, gc, wi_ref, wo_ref, o_ref, *, tm, n_latent, n_chunks):
    i = pl.program_id(1)
    is_ctx = _is_ctx_rows(i, tm, n_latent)
    x = x_ref[0]
    h = _modulate(x, is_ctx, shl, shc, scl, scc).astype(BF16)
    d_ff = wo_ref.shape[0]
    tf = d_ff // n_chunks
    y = None
    for c in range(n_chunks):
        gate = _dot(h, wi_ref[:, c * tf:(c + 1) * tf])
        up = _dot(h, wi_ref[:, d_ff + c * tf:d_ff + (c + 1) * tf])
        act = (_silu(gate) * up).astype(BF16)
        part = _dot(act, wo_ref[c * tf:(c + 1) * tf, :])
        y = part if y is None else y + part
    o_ref[0] = _post_norm(x, y, _row_select(is_ctx, gl, gc), 0.5)


def _ffn(x, mods, w_in, w_out, *, n_latent, n_rows, tm, n_chunks):
    n_batch, _, d = x.shape
    sh, sc, g = mods
    lat, ctx = _mod_specs(n_batch, d, 2)
    resident = lambda w: pl.BlockSpec(w.shape, lambda b, i: (0, 0), pipeline_mode=pl.Buffered(1))
    return pl.pallas_call(
        functools.partial(_ffn_kernel, tm=tm, n_latent=n_latent, n_chunks=n_chunks),
        grid=(n_batch, n_rows // tm),
        in_specs=[pl.BlockSpec((1, tm, d), lambda b, i: (b, i, 0)),
                  lat, ctx, lat, ctx, lat, ctx, resident(w_in), resident(w_out)],
        out_specs=pl.BlockSpec((1, tm, d), lambda b, i: (b, i, 0)),
        out_shape=jax.ShapeDtypeStruct((n_batch, n_rows, d), F32),
        compiler_params=_cparams(("arbitrary", "arbitrary")),
        name="ffn_postnorm",
    )(x, sh, sh, sc, sc, g, g, w_in, w_out)


def _even_proj_kernel(x_ref, shl, shc, scl, scc, w1_ref, wg_ref, bg_ref, qng_ref, kvng_ref,
                      wqm_ref, wqs_ref, wkv_ref, cb_ref, sb_ref,
                      gq_ref, gk_ref, gv_ref, r_ref, gf_ref, gb_ref, qm_ref, km_ref, vm_ref,
                      *, tm, n_latent):
    i = pl.program_id(1)
    is_ctx = _is_ctx_rows(i, tm, n_latent)
    h = _modulate(x_ref[0], is_ctx, shl, shc, scl, scc).astype(BF16)
    p = _dot(h, w1_ref[...])
    hp = GLA_HEADS * GLA_DK
    gv_w = GLA_HEADS * GLA_DV
    o0 = 0
    gq_ref[0] = p[:, o0:o0 + hp] * (GLA_DK ** -0.5)
    o0 += hp
    gk_ref[0] = p[:, o0:o0 + hp]
    o0 += hp
    gv_ref[0] = p[:, o0:o0 + gv_w].astype(BF16)
    o0 += gv_w
    r_ref[0] = p[:, o0:o0 + gv_w]
    o0 += gv_w
    lr = p[:, o0:o0 + LANES].astype(BF16)
    o0 += LANES
    cq = p[:, o0:o0 + MLA_D_CQ]
    o0 += MLA_D_CQ
    ckv = p[:, o0:o0 + MLA_D_CKV]
    o0 += MLA_D_CKV
    kr = p[:, o0:o0 + LANES]
    o0 += LANES
    krs = p[:, o0:o0 + LANES]

    z = _dot(lr, wg_ref[...]) + bg_ref[...]
    logg = (jnp.minimum(z, 0.0) - jnp.log1p(jnp.exp(-jnp.abs(z)))) / GLA_TAU
    gf_ref[0] = logg[:, :hp]
    gb_ref[0] = logg[:, hp:]

    cb = cb_ref[...]
    sb = sb_ref[...]
    cqn = (cq * lax.rsqrt(jnp.mean(cq * cq, axis=-1, keepdims=True) + EPS) * qng_ref[...]).astype(BF16)
    qmain = _dot(cqn, wqm_ref[...])
    qswap = _dot(cqn, wqs_ref[...])
    ckn = (ckv * lax.rsqrt(jnp.mean(ckv * ckv, axis=-1, keepdims=True) + EPS) * kvng_ref[...]).astype(BF16)
    kv = _dot(ckn, wkv_ref[...])
    k_rope = (kr * cb + krs * sb).astype(BF16)
    ones_col = (lax.broadcasted_iota(jnp.int32, (1, LANES), 1) == 0).astype(BF16)
    for hd in range(MLA_HEADS):
        b0 = 2 * LANES * hd
        qm_ref[0, :, b0:b0 + LANES] = (qmain[:, b0:b0 + LANES] * MLA_QSCALE).astype(BF16)
        q_rope = qmain[:, b0 + LANES:b0 + 2 * LANES] * cb + qswap[:, LANES * hd:LANES * (hd + 1)] * sb
        qm_ref[0, :, b0 + LANES:b0 + 2 * LANES] = (q_rope * MLA_QSCALE).astype(BF16)
        km_ref[0, :, b0:b0 + LANES] = kv[:, LANES * hd:LANES * (hd + 1)].astype(BF16)
        km_ref[0, :, b0 + LANES:b0 + 2 * LANES] = k_rope
        v0 = MLA_HEADS * MLA_D_NOPE + MLA_D_V * hd
        vm_ref[0, :, b0:b0 + LANES] = kv[:, v0:v0 + MLA_D_V].astype(BF16)
        vm_ref[0, :, b0 + LANES:b0 + 2 * LANES] = jnp.broadcast_to(ones_col, (tm, LANES))


def _even_proj(x, mods, wts, rope_c, rope_s, *, n_latent, tm):
    n_batch, t, d = x.shape
    sh, sc = mods
    lat, ctx = _mod_specs(n_batch, d, 2)
    full = lambda a: pl.BlockSpec(a.shape, lambda b, i: (0,) * a.ndim)
    row = lambda w: pl.BlockSpec((1, tm, w), lambda b, i: (b, i, 0))
    hp = GLA_HEADS * GLA_DK
    gv_w = GLA_HEADS * GLA_DV
    out_w = [(hp, F32), (hp, F32), (gv_w, BF16), (gv_w, F32), (hp, F32), (hp, F32),
             (MLA_HEADS * 2 * LANES, BF16), (MLA_HEADS * 2 * LANES, BF16), (MLA_HEADS * 2 * MLA_D_V, BF16)]
    return pl.pallas_call(
        functools.partial(_even_proj_kernel, tm=tm, n_latent=n_latent),
        grid=(n_batch, t // tm),
        in_specs=[row(d), lat, ctx, lat, ctx] + [full(w) for w in wts]
                 + [pl.BlockSpec((tm, LANES), lambda b, i: (i, 0))] * 2,
        out_specs=[row(w) for w, _ in out_w],
        out_shape=[jax.ShapeDtypeStruct((n_batch, t, w), dt) for w, dt in out_w],
        compiler_params=_cparams(("arbitrary", "arbitrary")),
        name="even_project",
    )(x, sh, sh, sc, sc, *wts, rope_c, rope_s)


def _gla_constants(chunk, reverse):
    n_lvl = int(np.log2(chunk))
    pos = np.arange(chunk)
    src = pos[None, :]
    tri = src <= pos[:, None]
    cum, mask = [], []
    for lvl in range(n_lvl):
        size = chunk >> lvl
        half = size // 2
        ref = ((pos // size) * size + half - 1)[:, None]
        upper = (pos % size) >= half
        a_up = (src > ref) & (src <= pos[:, None])
        a_lo = (src > pos[:, None]) & (src <= ref)
        cum.append(np.where(upper[:, None], a_up, a_lo))
        same = (pos[:, None] // size) == (pos[None, :] // size)
        mask.append(same & upper[:, None] & ~upper[None, :])
    cum += [tri, ~tri]
    cum = np.stack(cum).astype(np.float32)
    mask = np.stack(mask).astype(np.float32)
    if reverse:
        cum = cum[:, ::-1, ::-1]
        mask = mask[:, ::-1, ::-1]
    return np.ascontiguousarray(cum).reshape(-1, chunk), np.ascontiguousarray(mask)


def _gla_direction(q_ref, k_ref, v_ref, g_ref, cum_ref, mask_ref, st_ref, o_ref, *, chunk, reverse):
    n_lvl = mask_ref.shape[0]
    g = g_ref[0]
    g_hi = g.astype(BF16)
    g_lo = (g - g_hi.astype(F32)).astype(BF16)
    cum = cum_ref[...]
    sums = _dot(cum, g_hi) + _dot(cum, g_lo)
    b_rows = slice(n_lvl * chunk, (n_lvl + 1) * chunk)
    rest_rows = slice((n_lvl + 1) * chunk, (n_lvl + 2) * chunk)
    last = 0 if reverse else chunk - 1
    lane_lo = lax.broadcasted_iota(jnp.int32, (1, LANES), 1) < GLA_DK
    heads_per_group = LANES // GLA_DK
    eye = ((lax.broadcasted_iota(jnp.int32, (heads_per_group * chunk, chunk), 0) & (chunk - 1))
           == lax.broadcasted_iota(jnp.int32, (heads_per_group * chunk, chunk), 1)).astype(F32)

    def head_only(x, hh):
        return jnp.where(lane_lo if hh == 0 else jnp.logical_not(lane_lo), x, jnp.zeros_like(x))

    def stack_heads(x):
        return jnp.concatenate([head_only(x, hh) for hh in range(heads_per_group)], axis=0)

    for grp in range(GLA_HEADS // heads_per_group):
        sl = slice(grp * LANES, (grp + 1) * LANES)
        q = q_ref[0, :, sl]
        k = k_ref[0, :, sl]
        att = jnp.zeros((heads_per_group * chunk, chunk), F32)
        for lvl in range(n_lvl):
            w = jnp.exp(sums[lvl * chunk:(lvl + 1) * chunk, sl])
            qt = (q * w).astype(BF16)
            kt = (k * w).astype(BF16)
            att = att + _dot_nt(stack_heads(qt), kt) * mask_ref[lvl]
        if not reverse:
            att = att + _dot_nt(stack_heads(q.astype(BF16)), k.astype(BF16)) * eye
        att = att.astype(BF16)
        b = sums[b_rows, sl]
        qe = (q * jnp.exp(b)).astype(BF16)
        kd = (k * jnp.exp(sums[rest_rows, sl])).astype(BF16)
        decay = jnp.exp(b[last:last + 1, :])
        for hh in range(heads_per_group):
            hd = grp * heads_per_group + hh
            vs = slice(hd * GLA_DV, (hd + 1) * GLA_DV)
            v = v_ref[0, :, vs]
            st = st_ref[hd]
            o = _dot(att[hh * chunk:(hh + 1) * chunk], v) + _dot_nt(head_only(qe, hh), st.astype(BF16))
            o_ref[0, :, vs] = o
            st_ref[hd] = st * decay + _dot_tn(v, kd)


def _gla_kernel(qf, kf, vf, gf, qb, kb, vb, gb, cumf, maskf, cumb, maskb, of_ref, ob_ref, stf, stb,
                *, chunk):
    @pl.when(pl.program_id(1) == 0)
    def _():
        stf[...] = jnp.zeros_like(stf)
        stb[...] = jnp.zeros_like(stb)

    _gla_direction(qf, kf, vf, gf, cumf, maskf, stf, of_ref, chunk=chunk, reverse=False)
    _gla_direction(qb, kb, vb, gb, cumb, maskb, stb, ob_ref, chunk=chunk, reverse=True)


def _gla(gq, gk, gv, gf, gb, *, n_latent, chunk):
    n_batch, t, hp = gq.shape
    n_lat = n_latent // chunk
    n_ctx = (t - n_latent) // chunk
    n_steps = n_lat + n_ctx

    def fwd(b, s):
        return (b, jnp.where(s < n_ctx, n_lat + s, s - n_ctx), 0)

    def bwd(b, s):
        return (b, n_steps - 1 - s, 0)

    cumf, maskf = _gla_constants(chunk, False)
    cumb, maskb = _gla_constants(chunk, True)
    tile_heads = lambda m: jnp.asarray(np.tile(m, (1, LANES // GLA_DK, 1)))
    consts = [jnp.asarray(cumf, BF16), tile_heads(maskf), jnp.asarray(cumb, BF16), tile_heads(maskb)]
    full = lambda a: pl.BlockSpec(a.shape, lambda b, s: (0,) * a.ndim)
    blk = lambda w, im: pl.BlockSpec((1, chunk, w), im)
    gv_w = gv.shape[-1]
    return pl.pallas_call(
        functools.partial(_gla_kernel, chunk=chunk),
        grid=(n_batch, n_steps),
        in_specs=[blk(hp, fwd), blk(hp, fwd), blk(gv_w, fwd), blk(hp, fwd),
                  blk(hp, bwd), blk(hp, bwd), blk(gv_w, bwd), blk(hp, bwd)] + [full(a) for a in consts],
        out_specs=[blk(gv_w, fwd), blk(gv_w, bwd)],
        out_shape=[jax.ShapeDtypeStruct((n_batch, t, gv_w), F32)] * 2,
        scratch_shapes=[pltpu.VMEM((GLA_HEADS, GLA_DV, LANES), F32)] * 2,
        compiler_params=_cparams(("arbitrary", "arbitrary")),
        name="gla_bidirectional",
    )(gq, gk, gv, gf, gq, gk, gv, gb, *consts)


def _flash_kernel(q_ref, k_ref, v_ref, o_ref, m_ref, acc_ref, *, ck, dv):
    j = pl.program_id(3)
    n_sub = k_ref.shape[1] // ck

    @pl.when(j == 0)
    def _():
        m_ref[...] = jnp.full_like(m_ref, -jnp.inf)
        acc_ref[...] = jnp.zeros_like(acc_ref)

    q = q_ref[0]

    for c in range(n_sub):
        off = c * ck
        s = _dot_nt(q, k_ref[0, pl.ds(off, ck), :])
        m_prev = m_ref[...]
        m_new = jnp.maximum(m_prev, jnp.max(s, axis=-1, keepdims=True))
        alpha = jnp.exp2(m_prev - m_new)
        p = jnp.concatenate([jnp.exp2(s[:, u * LANES:(u + 1) * LANES] - m_new)
                             for u in range(ck // LANES)], axis=-1).astype(BF16)
        pv = _dot(p, v_ref[0, pl.ds(off, ck), :])
        acc_ref[...] = jnp.concatenate([alpha] * (acc_ref.shape[1] // LANES), axis=-1) * acc_ref[...] + pv
        m_ref[...] = m_new

    @pl.when(j == pl.num_programs(3) - 1)
    def _():
        acc = acc_ref[...]
        denom = jnp.sum(acc[:, dv:], axis=-1, keepdims=True)
        o_ref[0] = (acc[:, :dv] / denom).astype(o_ref.dtype)


def _flash(q, k, v, prev_out, *, n_heads, dq, dv, tq, tk, ck, q_blk0, n_q, k_blk0, n_k):
    n_batch, t, _ = q.shape
    in_specs = [pl.BlockSpec((1, tq, dq), lambda b, h, i, j: (b, q_blk0 + i, h)),
                pl.BlockSpec((1, tk, dq), lambda b, h, i, j: (b, k_blk0 + j, h)),
                pl.BlockSpec((1, tk, 2 * dv), lambda b, h, i, j: (b, k_blk0 + j, h))]
    args = [q, k, v]
    aliases = {}
    base = functools.partial(_flash_kernel, ck=ck, dv=dv)
    kern = base
    if prev_out is not None:
        in_specs.append(pl.BlockSpec(memory_space=pl.ANY))
        args.append(prev_out)
        aliases = {3: 0}
        kern = lambda q_, k_, v_, _prev, *rest: base(q_, k_, v_, *rest)
    return pl.pallas_call(
        kern,
        grid=(n_batch, n_heads, n_q, n_k),
        in_specs=in_specs,
        out_specs=pl.BlockSpec((1, tq, dv), lambda b, h, i, j: (b, q_blk0 + i, h)),
        out_shape=jax.ShapeDtypeStruct((n_batch, t, n_heads * dv), BF16),
        scratch_shapes=[pltpu.VMEM((tq, LANES), F32), pltpu.VMEM((tq, 2 * dv), F32)],
        input_output_aliases=aliases,
        compiler_params=_cparams(("arbitrary",) * 4),
        name="flash_attention",
    )(*args)


def _even_out_kernel(x_ref, of_ref, ob_ref, r_ref, ng_ref, mo_ref, wa_ref, wb_ref, gl, gc, o_ref,
                     *, tm, n_latent):
    i = pl.program_id(1)
    is_ctx = _is_ctx_rows(i, tm, n_latent)
    o = of_ref[0] + ob_ref[0]
    r = r_ref[0]
    parts = []
    for hd in range(GLA_HEADS):
        sl = slice(hd * GLA_DV, (hd + 1) * GLA_DV)
        oh = o[:, sl]
        y = oh * lax.rsqrt(jnp.mean(oh * oh, axis=-1, keepdims=True) + EPS) * ng_ref[...]
        parts.append((y * _silu(r[:, sl])).astype(BF16))
    a = jnp.concatenate(parts, axis=-1)
    y = _dot(a, wa_ref[...]) + _dot(mo_ref[0], wb_ref[...])
    o_ref[0] = _post_norm(x_ref[0], y, _row_select(is_ctx, gl, gc), 1.0)


def _even_out(x, o_f, o_b, r, norm_g, mla_o, wa, wb, gate, *, n_latent, n_rows, tm):
    n_batch, _, d = x.shape
    lat, ctx = _mod_specs(n_batch, d, 2)
    row = lambda w: pl.BlockSpec((1, tm, w), lambda b, i: (b, i, 0))
    full = lambda a: pl.BlockSpec(a.shape, lambda b, i: (0,) * a.ndim)
    gv_w = o_f.shape[-1]
    return pl.pallas_call(
        functools.partial(_even_out_kernel, tm=tm, n_latent=n_latent),
        grid=(n_batch, n_rows // tm),
        in_specs=[row(d), row(gv_w), row(gv_w), row(gv_w), full(norm_g), row(mla_o.shape[-1]),
                  full(wa), full(wb), lat, ctx],
        out_specs=row(d),
        out_shape=jax.ShapeDtypeStruct((n_batch, n_rows, d), F32),
        compiler_params=_cparams(("arbitrary", "arbitrary")),
        name="even_out_postnorm",
    )(x, o_f, o_b, r, norm_g, mla_o, wa, wb, gate, gate)


def _qkv_kernel(x_ref, shl, shc, scl, scc, w_ref, o_ref, *, tm, n_latent, q_width, q_scale):
    i = pl.program_id(1)
    is_ctx = _is_ctx_rows(i, tm, n_latent)
    h = _modulate(x_ref[0], is_ctx, shl, shc, scl, scc).astype(BF16)
    p = _dot(h, w_ref[...])
    o_ref[0, :, :q_width] = (p[:, :q_width] * q_scale).astype(BF16)
    o_ref[0, :, q_width:] = p[:, q_width:].astype(BF16)


def _qkv_proj(x, mods, w, *, n_latent, tm):
    n_batch, t, d = x.shape
    sh, sc = mods
    lat, ctx = _mod_specs(n_batch, d, 2)
    n_out = w.shape[1]
    return pl.pallas_call(
        functools.partial(_qkv_kernel, tm=tm, n_latent=n_latent, q_width=n_out // 3,
                          q_scale=NA_HEAD_DIM ** -0.5 * LOG2_E),
        grid=(n_batch, t // tm),
        in_specs=[pl.BlockSpec((1, tm, d), lambda b, i: (b, i, 0)), lat, ctx, lat, ctx,
                  pl.BlockSpec(w.shape, lambda b, i: (0, 0))],
        out_specs=pl.BlockSpec((1, tm, n_out), lambda b, i: (b, i, 0)),
        out_shape=jax.ShapeDtypeStruct((n_batch, t, n_out), BF16),
        compiler_params=_cparams(("arbitrary", "arbitrary")),
        name="qkv_project",
    )(x, sh, sh, sc, sc, w)


def _pair_softmax_av(q, lane_lo, scores_fn, av_fn):
    outs = []
    for half in range(2):
        keep = lane_lo if half == 0 else jnp.logical_not(lane_lo)
        qh = jnp.where(keep, q, jnp.zeros_like(q))
        s_list = scores_fn(qh, half)
        m = s_list[0].max(axis=-1, keepdims=True)
        for s in s_list[1:]:
            m = jnp.maximum(m, s.max(axis=-1, keepdims=True))
        p_list = [jnp.exp2(s - m) for s in s_list]
        denom = p_list[0].sum(axis=-1, keepdims=True)
        for p in p_list[1:]:
            denom = denom + p.sum(axis=-1, keepdims=True)
        outs.append(av_fn([p.astype(BF16) for p in p_list]) / denom)
    return jnp.where(lane_lo, outs[0], outs[1])


def _na_kernel(q_ref, k0, k1, k2, k3, v0, v1, v2, v3, kc_ref, vc_ref, bias_ref, o_ref, *, n_blocks):
    rb = pl.program_id(2)
    kind = jnp.where(rb == 0, 0, jnp.where(rb == n_blocks - 1, 2, 1))
    q = q_ref[0].astype(F32)
    k_win = jnp.concatenate([k0[0], k1[0], k2[0], k3[0]], axis=0)
    v_win = jnp.concatenate([v0[0], v1[0], v2[0], v3[0]], axis=0)
    n_keyrows = k_win.shape[0] // GRID_W
    lane_lo = lax.broadcasted_iota(jnp.int32, (1, LANES), 1) < NA_HEAD_DIM
    ones_col = (lax.broadcasted_iota(jnp.int32, (1, LANES), 1) == 0).astype(BF16)

    def with_ones(v):
        return jnp.concatenate([v, jnp.broadcast_to(ones_col, v.shape)], axis=-1)

    k_ctx = kc_ref[0]
    v_ctx = with_ones(vc_ref[0])
    pieces = [[None] * len(NA_COL_GROUPS) for _ in range(NA_ROWS)]
    row0 = 0
    for g, (c0, gw, kc0) in enumerate(NA_COL_GROUPS):
        mg = NA_ROWS * gw
        qg = jnp.concatenate([q[i * GRID_W + c0:i * GRID_W + c0 + gw] for i in range(NA_ROWS)], axis=0)
        qs = jnp.concatenate([jnp.where(lane_lo, qg, 0.0), jnp.where(lane_lo, 0.0, qg)], axis=0).astype(BF16)
        kg = jnp.concatenate([k_win[t * GRID_W + kc0:t * GRID_W + kc0 + NA_GROUP_KEYS] for t in range(n_keyrows)], axis=0)
        vg = jnp.concatenate([v_win[t * GRID_W + kc0:t * GRID_W + kc0 + NA_GROUP_KEYS] for t in range(n_keyrows)], axis=0)
        s_w = _dot_nt(qs, kg) + bias_ref[kind, 0, pl.ds(row0, 2 * mg), :]
        s_c = _dot_nt(qs, k_ctx)
        m = jnp.maximum(jnp.max(s_w, axis=-1, keepdims=True), jnp.max(s_c, axis=-1, keepdims=True))
        pv = _dot(jnp.exp2(s_w - m).astype(BF16), with_ones(vg)) + _dot(jnp.exp2(s_c - m).astype(BF16), v_ctx)
        o = pv[:, :LANES] / jnp.sum(pv[:, LANES:], axis=-1, keepdims=True)
        og = jnp.where(lane_lo, o[:mg], o[mg:])
        for i in range(NA_ROWS):
            pieces[i][g] = og[i * gw:(i + 1) * gw]
        row0 += 2 * mg
    o_ref[0] = jnp.concatenate([pc for row in pieces for pc in row], axis=0).astype(o_ref.dtype)


def _na_bias_tables(rpb, rows):
    n_keyrows = NA_ROWS + 2 * NA_UNIT
    col = np.arange(GRID_W)
    col_start = np.clip(col - NA_KW // 2, 0, GRID_W - NA_KW)
    kc = np.arange(GRID_W)
    col_ok = (kc[None, :] >= col_start[:, None]) & (kc[None, :] < col_start[:, None] + NA_KW)
    rel_col = np.clip(kc[None, :] - col[:, None] + (NA_KW - 1), 0, 2 * NA_KW - 2)
    col_sel = ((rel_col[:, :, None] == np.arange(2 * NA_KW - 1)) & col_ok[:, :, None]).astype(np.float32)
    n_heads = rpb.shape[0]
    by_col = jnp.einsum('hab,ckb->hack', rpb, jnp.asarray(col_sel), precision=lax.Precision.HIGHEST)
    by_col = jnp.where(jnp.asarray(col_ok)[None, None], by_col * LOG2_E, MASK_NEG)
    by_col = jnp.pad(by_col, ((0, 0), (NA_ROWS, n_keyrows - NA_ROWS), (0, 0), (0, 0)), constant_values=MASK_NEG)
    n_blocks = rows // NA_ROWS
    offs, row_mask = [], []
    for blk in (0, min(1, n_blocks - 1), n_blocks - 1):
        r = blk * NA_ROWS + np.arange(NA_ROWS)
        kh = min(NA_KH, rows)
        r_start = np.clip(r - kh // 2, 0, rows - kh)
        key_row0 = int(np.clip(blk * NA_ROWS - NA_UNIT, 0, rows - n_keyrows))
        kr = key_row0 + np.arange(n_keyrows)
        row_ok = (kr[None, :] >= r_start[:, None]) & (kr[None, :] < r_start[:, None] + kh)
        offs.append(key_row0 - blk * NA_ROWS + NA_KH - 1)
        row_mask.append(np.repeat(np.where(row_ok, 0.0, MASK_NEG), NA_GROUP_KEYS, axis=1))
    row_mask = jnp.asarray(np.stack(row_mask), F32)
    width = n_keyrows * NA_GROUP_KEYS
    parts = []
    for c0, gw, kc0 in NA_COL_GROUPS:
        strip = by_col[:, :, c0:c0 + gw, kc0:kc0 + NA_GROUP_KEYS]
        strip = strip.transpose(0, 2, 1, 3).reshape(n_heads, gw, -1)
        tiles = jnp.stack([jnp.stack([strip[:, :, (NA_ROWS + off - i) * NA_GROUP_KEYS:][:, :, :width]
                                      for i in range(NA_ROWS)], axis=1) for off in offs], axis=0)
        tiles = tiles + row_mask[:, None, :, None, :]
        parts.append(tiles.reshape(3, n_heads // 2, 2 * NA_ROWS * gw, width))
    return jnp.concatenate(parts, axis=2)


def _na_attention(qkv, bias, *, n_latent):
    n_batch, t, _ = qkv.shape
    rows = n_latent // GRID_W
    n_blocks = rows // NA_ROWS
    n_pairs = NA_HEADS // 2
    tq = NA_ROWS * GRID_W
    tu = NA_UNIT * GRID_W
    n_units = rows // NA_UNIT
    ctx_blk = n_latent // tu

    def unit(u):
        def im(p, b, rb):
            u0 = jnp.clip(rb * (NA_ROWS // NA_UNIT) - 1, 0, n_units - 4)
            return u0 + u
        return im

    k_specs = [pl.BlockSpec((1, tu, LANES), lambda p, b, rb, f=unit(u): (b, f(p, b, rb), n_pairs + p))
               for u in range(4)]
    v_specs = [pl.BlockSpec((1, tu, LANES), lambda p, b, rb, f=unit(u): (b, f(p, b, rb), 2 * n_pairs + p))
               for u in range(4)]
    return pl.pallas_call(
        functools.partial(_na_kernel, n_blocks=n_blocks),
        grid=(n_pairs, n_batch, n_blocks),
        in_specs=[pl.BlockSpec((1, tq, LANES), lambda p, b, rb: (b, rb, p))] + k_specs + v_specs
                 + [pl.BlockSpec((1, tu, LANES), lambda p, b, rb: (b, ctx_blk, n_pairs + p)),
                    pl.BlockSpec((1, tu, LANES), lambda p, b, rb: (b, ctx_blk, 2 * n_pairs + p)),
                    pl.BlockSpec((3, 1) + bias.shape[2:], lambda p, b, rb: (0, p, 0, 0))],
        out_specs=pl.BlockSpec((1, tq, LANES), lambda p, b, rb: (b, rb, p)),
        out_shape=jax.ShapeDtypeStruct((n_batch, t, NA_HEADS * NA_HEAD_DIM), BF16),
        compiler_params=_cparams(("arbitrary",) * 3),
        name="neighbourhood_attention",
    )(qkv, *([qkv] * 10), bias)


def _ctx_pair_kernel(q_ref, k_ref, v_ref, _prev, o_ref):
    k = k_ref[0]
    v = v_ref[0]
    lane_lo = lax.broadcasted_iota(jnp.int32, (1, LANES), 1) < NA_HEAD_DIM
    o_ref[0] = _pair_softmax_av(q_ref[0], lane_lo, lambda qh, half: [_dot_nt(qh, k)],
                                lambda ps: _dot(ps[0], v)).astype(o_ref.dtype)


def _ctx_pair_attention(qkv, prev_out, *, n_latent):
    n_batch, t, _ = qkv.shape
    n_ctx = t - n_latent
    blk = n_latent // n_ctx
    n_pairs = NA_HEADS // 2
    spec = lambda off: pl.BlockSpec((1, n_ctx, LANES), lambda b, p: (b, blk, off + p))
    return pl.pallas_call(
        _ctx_pair_kernel,
        grid=(n_batch, n_pairs),
        in_specs=[spec(0), spec(n_pairs), spec(2 * n_pairs), pl.BlockSpec(memory_space=pl.ANY)],
        out_specs=spec(0),
        out_shape=jax.ShapeDtypeStruct(prev_out.shape, prev_out.dtype),
        input_output_aliases={3: 0},
        compiler_params=_cparams(("arbitrary", "arbitrary")),
        name="context_pair_attention",
    )(qkv, qkv, qkv, prev_out)


def _odd_out_kernel(x_ref, a_ref, w_ref, gl, gc, o_ref, *, tm, n_latent):
    i = pl.program_id(1)
    is_ctx = _is_ctx_rows(i, tm, n_latent)
    y = _dot(a_ref[0], w_ref[...])
    o_ref[0] = _post_norm(x_ref[0], y, _row_select(is_ctx, gl, gc), 1.0)


def _odd_out(x, a, w, gate, *, n_latent, n_rows, tm):
    n_batch, _, d = x.shape
    lat, ctx = _mod_specs(n_batch, d, 2)
    row = lambda wd: pl.BlockSpec((1, tm, wd), lambda b, i: (b, i, 0))
    return pl.pallas_call(
        functools.partial(_odd_out_kernel, tm=tm, n_latent=n_latent),
        grid=(n_batch, n_rows // tm),
        in_specs=[row(d), row(a.shape[-1]), pl.BlockSpec(w.shape, lambda b, i: (0, 0)), lat, ctx],
        out_specs=row(d),
        out_shape=jax.ShapeDtypeStruct((n_batch, n_rows, d), F32),
        compiler_params=_cparams(("arbitrary", "arbitrary")),
        name="odd_out_postnorm",
    )(x, a, w, gate, gate)


def _even_weights(w_in, wg2_f, bg_f, wg2_b, bg_b, q_norm_g, kv_norm_g, w_uq, w_ukv):
    sizes = (GLA_HEADS * GLA_DK, GLA_HEADS * GLA_DK, GLA_HEADS * GLA_DV, GLA_HEADS * GLA_DV,
             GLA_RANK, GLA_RANK, MLA_D_CQ, MLA_D_CKV, MLA_D_ROPE)
    q_g, k_g, v_g, r_g, lr_f, lr_b, c_q, c_kv, k_r = jnp.split(w_in, np.cumsum(sizes)[:-1].tolist(), axis=-1)
    swap = np.arange(MLA_D_ROPE) ^ 1
    d = w_in.shape[0]
    zeros = lambda n: jnp.zeros((d, n), w_in.dtype)
    w1 = jnp.concatenate([
        q_g, k_g, v_g, r_g,
        lr_f, lr_b, zeros(LANES - 2 * GLA_RANK), c_q, c_kv,
        k_r, zeros(LANES - MLA_D_ROPE), k_r[:, swap], zeros(LANES - MLA_D_ROPE)], axis=-1).astype(BF16)
    hp = GLA_HEADS * GLA_DK
    wg = jnp.zeros((LANES, 2 * hp), F32)
    wg = wg.at[:GLA_RANK, :hp].set(wg2_f)
    wg = wg.at[GLA_RANK:2 * GLA_RANK, hp:].set(wg2_b)
    bg = jnp.concatenate([bg_f, bg_b])[None]
    uq = w_uq.reshape(MLA_D_CQ, MLA_HEADS, MLA_D_NOPE + MLA_D_ROPE)
    nope, rope = uq[..., :MLA_D_NOPE], uq[..., MLA_D_NOPE:]
    zr = jnp.zeros((MLA_D_CQ, MLA_HEADS, LANES - MLA_D_ROPE), w_uq.dtype)
    wq_main = jnp.concatenate([nope, rope, zr], axis=-1).reshape(MLA_D_CQ, -1).astype(BF16)
    wq_swap = jnp.concatenate([rope[..., swap], zr], axis=-1).reshape(MLA_D_CQ, -1).astype(BF16)
    ukv = w_ukv.reshape(MLA_D_CKV, MLA_HEADS, MLA_D_NOPE + MLA_D_V)
    wkv = jnp.concatenate([ukv[..., :MLA_D_NOPE].reshape(MLA_D_CKV, -1),
                           ukv[..., MLA_D_NOPE:].reshape(MLA_D_CKV, -1)], axis=-1).astype(BF16)
    return (w1, wg.astype(BF16), bg, q_norm_g[None], kv_norm_g[None], wq_main, wq_swap, wkv)


def _rope_tables(n_latent, n_ctx):
    t = jnp.arange(n_latent)
    row = (t // GRID_W).astype(F32)
    col = (t % GRID_W).astype(F32)
    n_freq = MLA_D_ROPE // 4
    inv = ROPE_BASE ** (-jnp.arange(n_freq, dtype=F32) / n_freq)
    ang = jnp.concatenate([row[:, None] * inv, col[:, None] * inv], axis=-1)
    cos = jnp.repeat(jnp.cos(ang), 2, axis=-1)
    sin = jnp.repeat(jnp.sin(ang), 2, axis=-1) * jnp.tile(jnp.array([-1.0, 1.0], F32), MLA_D_ROPE // 2)
    cos = jnp.concatenate([cos, jnp.ones((n_ctx, MLA_D_ROPE), F32)])
    sin = jnp.concatenate([sin, jnp.zeros((n_ctx, MLA_D_ROPE), F32)])
    pad = jnp.zeros((n_latent + n_ctx, LANES - MLA_D_ROPE), F32)
    return jnp.concatenate([cos, pad], axis=-1), jnp.concatenate([sin, pad], axis=-1)


def _row_tile(n_rows, target):
    best = 8
    for cand in range(8, target + 1, 8):
        if n_rows % cand == 0:
            best = cand
    return best


def kernel(x, c, ctx, c_ctx, ada_w, ada_b, ffn1_w_in, ffn1_w_out, ffn2_w_in, ffn2_w_out, even_w_in, gla_wg2_f, gla_bg_f, gla_wg2_b, gla_bg_b, gla_norm_g, mla_q_norm_g, mla_kv_norm_g, mla_w_uq, mla_w_ukv, even_w_out, na_w_in, na_rpb, na_w_out):
    n_batch, n_latent, d = x.shape
    n_ctx = ctx.shape[1]
    t_all = n_latent + n_ctx
    assert n_batch + 1 <= 8 and n_latent % (NA_ROWS * GRID_W) == 0 and n_ctx == NA_UNIT * GRID_W
    assert n_latent % GLA_CHUNK == 0 and n_ctx % GLA_CHUNK == 0

    xa = jnp.concatenate([x, ctx], axis=1)
    cc = jnp.concatenate([c, c_ctx[None], jnp.zeros((7 - n_batch, d), F32)], axis=0)
    mods = _ada_modulation(cc, ada_w, ada_b)
    mods = mods[:, :n_batch + 1].reshape(DEPTH, n_batch + 1, 9, 1, d).transpose(0, 2, 1, 3, 4)

    tm_ffn = _row_tile(t_all, 768)
    tm_ffn_lat = _row_tile(n_latent, 768)
    tm_row = _row_tile(t_all, 384)
    tm_row_lat = _row_tile(n_latent, 512)
    rope_c, rope_s = _rope_tables(n_latent, n_ctx)
    tq = _row_tile(n_latent, 1024)
    ck = next(cand for cand in (768, 512, 256, 128) if t_all % cand == 0)
    ck_ctx = next(cand for cand in (768, 512, 256, 128) if n_ctx % cand == 0)

    for l in range(DEPTH):
        last = l == DEPTH - 1
        i = l // 2
        m = mods[l]
        xa = _ffn(xa, (m[0], m[1], m[2]), ffn1_w_in[l].astype(BF16), ffn1_w_out[l].astype(BF16),
                  n_latent=n_latent, n_rows=t_all, tm=tm_ffn, n_chunks=2)
        n_rows = n_latent if last else t_all
        tm_out = tm_row_lat if last else tm_row
        if l % 2 == 0:
            wts = _even_weights(even_w_in[i], gla_wg2_f[i], gla_bg_f[i], gla_wg2_b[i], gla_bg_b[i],
                                mla_q_norm_g[i], mla_kv_norm_g[i], mla_w_uq[i], mla_w_ukv[i])
            gq, gk, gv, r, gf, gb, qm, km, vm = _even_proj(xa, (m[3], m[4]), wts, rope_c, rope_s,
                                                           n_latent=n_latent, tm=tm_row)
            o_f, o_b = _gla(gq, gk, gv, gf, gb, n_latent=n_latent, chunk=GLA_CHUNK)
            mla_o = _flash(qm, km, vm, None, n_heads=MLA_HEADS, dq=2 * LANES, dv=MLA_D_V, tq=tq, tk=t_all, ck=ck,
                           q_blk0=0, n_q=n_latent // tq, k_blk0=0, n_k=1)
            if not last:
                mla_o = _flash(qm, km, vm, mla_o, n_heads=MLA_HEADS, dq=2 * LANES, dv=MLA_D_V,
                               tq=n_ctx, tk=n_ctx, ck=ck_ctx, q_blk0=n_latent // n_ctx, n_q=1,
                               k_blk0=n_latent // n_ctx, n_k=1)
            w_out = even_w_out[i].astype(BF16)
            gv_w = GLA_HEADS * GLA_DV
            xa = _even_out(xa, o_f, o_b, r, gla_norm_g[i][None], mla_o, w_out[:gv_w], w_out[gv_w:], m[5],
                           n_latent=n_latent, n_rows=n_rows, tm=tm_out)
        else:
            qkv = _qkv_proj(xa, (m[3], m[4]), na_w_in[i].astype(BF16), n_latent=n_latent, tm=tm_row)
            bias = _na_bias_tables(na_rpb[i], n_latent // GRID_W)
            att = _na_attention(qkv, bias, n_latent=n_latent)
            if not last:
                att = _ctx_pair_attention(qkv, att, n_latent=n_latent)
            xa = _odd_out(xa, att, na_w_out[i].astype(BF16), m[5], n_latent=n_latent, n_rows=n_rows, tm=tm_out)
        xa = _ffn(xa, (m[6], m[7], m[8]), ffn2_w_in[l].astype(BF16), ffn2_w_out[l].astype(BF16),
                  n_latent=n_latent, n_rows=n_rows, tm=tm_ffn_lat if last else tm_ffn, n_chunks=2)
    return xa
```

```python
import functools

import numpy as np
import jax
import jax.numpy as jnp
from jax import lax
from jax.experimental import pallas as pl
from jax.experimental.pallas import tpu as pltpu

DEPTH = 4
GRID_W = 64
D_FF = 2816
GLA_HEADS = 4
GLA_DK = 64
GLA_DV = 128
GLA_RANK = 16
GLA_TAU = 16.0
MLA_HEADS = 4
MLA_D_NOPE = 128
MLA_D_ROPE = 64
MLA_D_V = 128
MLA_D_CQ = 384
MLA_D_CKV = 128
MLA_SCALE = (MLA_D_NOPE + MLA_D_ROPE) ** -0.5
LOG2_E = 1.4426950408889634
MLA_QSCALE = MLA_SCALE * LOG2_E
NA_HEADS = 16
NA_HEAD_DIM = 64
NA_KH = 8
NA_KW = 16
ROPE_BASE = 10000.0
EPS = 1e-6
ALPHA = (2 * DEPTH) ** 0.25

LANES = 128
VMEM_LIMIT = 56 * 1024 * 1024

GLA_CHUNK = 256
NA_ROWS = 8
NA_UNIT = 4
MASK_NEG = -1e30
GLA_FAST_MAX_DECAY = 60.0
GLA_FAST_MAX_KEY = 1e10
NA_COL_GROUPS = ((0, 24, 0), (24, 16, 16), (40, 24, 32))
NA_GROUP_KEYS = 32

BF16 = jnp.bfloat16
F32 = jnp.float32


def _cparams(sem):
    return pltpu.CompilerParams(dimension_semantics=sem, vmem_limit_bytes=VMEM_LIMIT)


def _dot(a, b):
    return jnp.dot(a, b, preferred_element_type=F32)


def _dot_nt(a, b):
    return lax.dot_general(a, b, (((1,), (1,)), ((), ())), preferred_element_type=F32)


def _dot_tn(a, b):
    return lax.dot_general(a, b, (((0,), (0,)), ((), ())), preferred_element_type=F32)


def _silu(v):
    return v * (1.0 / (1.0 + jnp.exp(-v)))


def _row_select(is_ctx, lat_ref, ctx_ref):
    return jnp.where(is_ctx, ctx_ref[0], lat_ref[0])


def _is_ctx_rows(tile_idx, tm, n_latent):
    rows = tile_idx * tm + lax.broadcasted_iota(jnp.int32, (tm, 1), 0)
    return rows >= n_latent


def _modulate(x, is_ctx, sh_l, sh_c, sc_l, sc_c):
    shift = _row_select(is_ctx, sh_l, sh_c)
    scale = _row_select(is_ctx, sc_l, sc_c)
    return x * (1.0 + scale) + shift


def _post_norm(x, y, gate, coef):
    z = ALPHA * x + (coef * gate) * y
    mu = jnp.mean(z, axis=-1, keepdims=True)
    zc = z - mu
    var = jnp.mean(zc * zc, axis=-1, keepdims=True)
    return zc * lax.rsqrt(var + EPS)


def _mod_specs(n_batch, d, grid_rank):
    if grid_rank == 2:
        lat = pl.BlockSpec((1, 1, d), lambda b, i: (b, 0, 0))
        ctx = pl.BlockSpec((1, 1, d), lambda b, i: (n_batch, 0, 0))
    else:
        lat = pl.BlockSpec((1, 1, d), lambda b, i, j: (b, 0, 0))
        ctx = pl.BlockSpec((1, 1, d), lambda b, i, j: (n_batch, 0, 0))
    return lat, ctx


def _ada_kernel(c_ref, w_ref, b_ref, o_ref):
    a = _silu(c_ref[...]).astype(BF16)
    o_ref[0] = _dot(a, w_ref[0].astype(BF16)) + b_ref[0]


def _ada_modulation(cc, ada_w, ada_b):
    depth, d, n9 = ada_w.shape
    tn = n9 // 8
    return pl.pallas_call(
        _ada_kernel,
        grid=(depth, n9 // tn),
        in_specs=[pl.BlockSpec((8, d), lambda l, j: (0, 0)),
                  pl.BlockSpec((1, d, tn), lambda l, j: (l, 0, j)),
                  pl.BlockSpec((1, 1, tn), lambda l, j: (l, 0, j))],
        out_specs=pl.BlockSpec((1, 8, tn), lambda l, j: (l, 0, j)),
        out_shape=jax.ShapeDtypeStruct((depth, 8, n9), F32),
        compiler_params=_cparams(("arbitrary", "arbitrary")),
        name="ada_modulation",
    )(cc, ada_w, ada_b.reshape(depth, 1, n9))


def _ffn_kernel(x_ref, shl, shc, scl, scc, gl, gc, wi_ref, wo_ref, o_ref, *, tm, n_latent, n_chunks):
    i = pl.program_id(1)
    is_ctx = _is_ctx_rows(i, tm, n_latent)
    x = x_ref[0]
    h = _modulate(x, is_ctx, shl, shc, scl, scc).astype(BF16)
    d_ff = wo_ref.shape[0]
    tf = d_ff // n_chunks
    y = None
    for c in range(n_chunks):
        gate = _dot(h, wi_ref[:, c * tf:(c + 1) * tf])
        up = _dot(h, wi_ref[:, d_ff + c * tf:d_ff + (c + 1) * tf])
        act = (_silu(gate) * up).astype(BF16)
        part = _dot(act, wo_ref[c * tf:(c + 1) * tf, :])
        y = part if y is None else y + part
    o_ref[0] = _post_norm(x, y, _row_select(is_ctx, gl, gc), 0.5)


def _ffn(x, mods, w_in, w_out, *, n_latent, n_rows, tm, n_chunks):
    n_batch, _, d = x.shape
    sh, sc, g = mods
    lat, ctx = _mod_specs(n_batch, d, 2)
    resident = lambda w: pl.BlockSpec(w.shape, lambda b, i: (0, 0), pipeline_mode=pl.Buffered(1))
    return pl.pallas_call(
        functools.partial(_ffn_kernel, tm=tm, n_latent=n_latent, n_chunks=n_chunks),
        grid=(n_batch, n_rows // tm),
        in_specs=[pl.BlockSpec((1, tm, d), lambda b, i: (b, i, 0)),
                  lat, ctx, lat, ctx, lat, ctx, resident(w_in), resident(w_out)],
        out_specs=pl.BlockSpec((1, tm, d), lambda b, i: (b, i, 0)),
        out_shape=jax.ShapeDtypeStruct((n_batch, n_rows, d), F32),
        compiler_params=_cparams(("arbitrary", "arbitrary")),
        name="ffn_postnorm",
    )(x, sh, sh, sc, sc, g, g, w_in, w_out)


def _even_proj_kernel(x_ref, shl, shc, scl, scc, w1_ref, wg_ref, bg_ref, qng_ref, kvng_ref,
                      wqm_ref, wqs_ref, wkv_ref, cb_ref, sb_ref,
                      gq_ref, gk_ref, gv_ref, r_ref, gf_ref, gb_ref, qm_ref, km_ref, vm_ref,
                      *, tm, n_latent):
    i = pl.program_id(1)
    is_ctx = _is_ctx_rows(i, tm, n_latent)
    h = _modulate(x_ref[0], is_ctx, shl, shc, scl, scc).astype(BF16)
    p = _dot(h, w1_ref[...])
    hp = GLA_HEADS * GLA_DK
    gv_w = GLA_HEADS * GLA_DV
    o0 = 0
    gq_ref[0] = p[:, o0:o0 + hp] * (GLA_DK ** -0.5)
    o0 += hp
    gk_ref[0] = p[:, o0:o0 + hp]
    o0 += hp
    gv_ref[0] = p[:, o0:o0 + gv_w].astype(BF16)
    o0 += gv_w
    r_ref[0] = p[:, o0:o0 + gv_w]
    o0 += gv_w
    lr = p[:, o0:o0 + LANES].astype(BF16)
    o0 += LANES
    cq = p[:, o0:o0 + MLA_D_CQ]
    o0 += MLA_D_CQ
    ckv = p[:, o0:o0 + MLA_D_CKV]
    o0 += MLA_D_CKV
    kr = p[:, o0:o0 + LANES]
    o0 += LANES
    krs = p[:, o0:o0 + LANES]

    z = _dot(lr, wg_ref[...]) + bg_ref[...]
    logg = (jnp.minimum(z, 0.0) - jnp.log1p(jnp.exp(-jnp.abs(z)))) / GLA_TAU
    gf_ref[0] = logg[:, :hp]
    gb_ref[0] = logg[:, hp:]

    cb = cb_ref[...]
    sb = sb_ref[...]
    cqn = (cq * lax.rsqrt(jnp.mean(cq * cq, axis=-1, keepdims=True) + EPS) * qng_ref[...]).astype(BF16)
    qmain = _dot(cqn, wqm_ref[...])
    qswap = _dot(cqn, wqs_ref[...])
    ckn = (ckv * lax.rsqrt(jnp.mean(ckv * ckv, axis=-1, keepdims=True) + EPS) * kvng_ref[...]).astype(BF16)
    kv = _dot(ckn, wkv_ref[...])
    k_rope = (kr * cb + krs * sb).astype(BF16)
    ones_col = (lax.broadcasted_iota(jnp.int32, (1, LANES), 1) == 0).astype(BF16)
    for hd in range(MLA_HEADS):
        b0 = 2 * LANES * hd
        qm_ref[0, :, b0:b0 + LANES] = (qmain[:, b0:b0 + LANES] * MLA_QSCALE).astype(BF16)
        q_rope = qmain[:, b0 + LANES:b0 + 2 * LANES] * cb + qswap[:, LANES * hd:LANES * (hd + 1)] * sb
        qm_ref[0, :, b0 + LANES:b0 + 2 * LANES] = (q_rope * MLA_QSCALE).astype(BF16)
        km_ref[0, :, b0:b0 + LANES] = kv[:, LANES * hd:LANES * (hd + 1)].astype(BF16)
        km_ref[0, :, b0 + LANES:b0 + 2 * LANES] = k_rope
        v0 = MLA_HEADS * MLA_D_NOPE + MLA_D_V * hd
        vm_ref[0, :, b0:b0 + LANES] = kv[:, v0:v0 + MLA_D_V].astype(BF16)
        vm_ref[0, :, b0 + LANES:b0 + 2 * LANES] = jnp.broadcast_to(ones_col, (tm, LANES))


def _even_proj(x, mods, wts, rope_c, rope_s, *, n_latent, tm):
    n_batch, t, d = x.shape
    sh, sc = mods
    lat, ctx = _mod_specs(n_batch, d, 2)
    full = lambda a: pl.BlockSpec(a.shape, lambda b, i: (0,) * a.ndim)
    row = lambda w: pl.BlockSpec((1, tm, w), lambda b, i: (b, i, 0))
    hp = GLA_HEADS * GLA_DK
    gv_w = GLA_HEADS * GLA_DV
    out_w = [(hp, F32), (hp, F32), (gv_w, BF16), (gv_w, F32), (hp, F32), (hp, F32),
             (MLA_HEADS * 2 * LANES, BF16), (MLA_HEADS * 2 * LANES, BF16), (MLA_HEADS * 2 * MLA_D_V, BF16)]
    return pl.pallas_call(
        functools.partial(_even_proj_kernel, tm=tm, n_latent=n_latent),
        grid=(n_batch, t // tm),
        in_specs=[row(d), lat, ctx, lat, ctx] + [full(w) for w in wts]
                 + [pl.BlockSpec((tm, LANES), lambda b, i: (i, 0))] * 2,
        out_specs=[row(w) for w, _ in out_w],
        out_shape=[jax.ShapeDtypeStruct((n_batch, t, w), dt) for w, dt in out_w],
        compiler_params=_cparams(("arbitrary", "arbitrary")),
        name="even_project",
    )(x, sh, sh, sc, sc, *wts, rope_c, rope_s)


def _gla_constants(chunk, reverse):
    n_lvl = int(np.log2(chunk))
    pos = np.arange(chunk)
    src = pos[None, :]
    tri = src <= pos[:, None]
    cum, mask = [], []
    for lvl in range(n_lvl):
        size = chunk >> lvl
        half = size // 2
        ref = ((pos // size) * size + half - 1)[:, None]
        upper = (pos % size) >= half
        a_up = (src > ref) & (src <= pos[:, None])
        a_lo = (src > pos[:, None]) & (src <= ref)
        cum.append(np.where(upper[:, None], a_up, a_lo))
        same = (pos[:, None] // size) == (pos[None, :] // size)
        mask.append(same & upper[:, None] & ~upper[None, :])
    mask.append(np.any(mask, axis=0) | (np.eye(chunk, dtype=bool) & (not reverse)))
    cum = np.stack([tri, ~tri] + cum).astype(np.float32)
    mask = np.stack(mask).astype(np.float32)
    if reverse:
        cum = cum[:, ::-1, ::-1]
        mask = mask[:, ::-1, ::-1]
    return np.ascontiguousarray(cum).reshape(-1, chunk), np.ascontiguousarray(mask)


class _GlaDirection:
    def __init__(self, q_ref, k_ref, v_ref, g_ref, cum_ref, mask_ref, st_ref, att_ref, o_ref, *, chunk, reverse):
        self.q_ref, self.k_ref, self.v_ref = q_ref, k_ref, v_ref
        self.cum_ref, self.mask_ref, self.st_ref, self.att_ref, self.o_ref = cum_ref, mask_ref, st_ref, att_ref, o_ref
        self.chunk, self.reverse = chunk, reverse
        self.n_lvl = mask_ref.shape[0] - 1
        self.heads_per_group = LANES // GLA_DK
        self.groups = [slice(grp * LANES, (grp + 1) * LANES) for grp in range(GLA_HEADS * GLA_DK // LANES)]
        g = g_ref[0]
        self.g_hi = g.astype(BF16)
        self.g_lo = (g - self.g_hi.astype(F32)).astype(BF16)
        base = self.partial_sums(slice(0, 2 * chunk))
        self.b = base[:chunk]
        self.rest = base[chunk:]
        last = 0 if reverse else chunk - 1
        self.b_last = self.b[last:last + 1, :]
        self.lane_lo = lax.broadcasted_iota(jnp.int32, (1, LANES), 1) < GLA_DK

    def partial_sums(self, rows):
        return _dot(self.cum_ref[rows, :], self.g_hi) + _dot(self.cum_ref[rows, :], self.g_lo)

    def head_only(self, x, hh):
        return jnp.where(self.lane_lo if hh == 0 else jnp.logical_not(self.lane_lo), x, jnp.zeros_like(x))

    def stack_heads(self, x):
        return jnp.concatenate([self.head_only(x, hh) for hh in range(self.heads_per_group)], axis=0)

    def single_reference_ok(self):
        return jnp.logical_and(jnp.max(-self.b_last) <= GLA_FAST_MAX_DECAY,
                               jnp.max(jnp.abs(self.k_ref[0])) <= GLA_FAST_MAX_KEY)

    def scores_single_reference(self):
        for grp, sl in enumerate(self.groups):
            b = self.b[:, sl]
            qe = (self.q_ref[0, :, sl] * jnp.exp(b)).astype(BF16)
            ki = (self.k_ref[0, :, sl] * jnp.exp(-b)).astype(BF16)
            att = jnp.where(self.mask_ref[self.n_lvl] > 0.0, _dot_nt(self.stack_heads(qe), ki), 0.0)
            self.att_ref[grp] = att.astype(BF16)

    def scores_by_level(self):
        chunk, n_lvl = self.chunk, self.n_lvl
        rows = self.heads_per_group * chunk
        sums = self.partial_sums(slice(2 * chunk, (n_lvl + 2) * chunk))
        eye = ((lax.broadcasted_iota(jnp.int32, (rows, chunk), 0) & (chunk - 1))
               == lax.broadcasted_iota(jnp.int32, (rows, chunk), 1)).astype(F32)
        for grp, sl in enumerate(self.groups):
            q = self.q_ref[0, :, sl]
            k = self.k_ref[0, :, sl]
            att = jnp.zeros((rows, chunk), F32)
            for lvl in range(n_lvl):
                w = jnp.exp(sums[lvl * chunk:(lvl + 1) * chunk, sl])
                att = att + _dot_nt(self.stack_heads((q * w).astype(BF16)), (k * w).astype(BF16)) * self.mask_ref[lvl]
            if not self.reverse:
                att = att + _dot_nt(self.stack_heads(q.astype(BF16)), k.astype(BF16)) * eye
            self.att_ref[grp] = att.astype(BF16)

    def outputs_and_state(self):
        chunk = self.chunk
        for grp, sl in enumerate(self.groups):
            q = self.q_ref[0, :, sl]
            k = self.k_ref[0, :, sl]
            qe = (q * jnp.exp(self.b[:, sl])).astype(BF16)
            kd = (k * jnp.exp(self.rest[:, sl])).astype(BF16)
            decay = jnp.exp(self.b_last[:, sl])
            att = self.att_ref[grp]
            for hh in range(self.heads_per_group):
                hd = grp * self.heads_per_group + hh
                vs = slice(hd * GLA_DV, (hd + 1) * GLA_DV)
                v = self.v_ref[0, :, vs]
                st = self.st_ref[hd]
                o = _dot(att[hh * chunk:(hh + 1) * chunk], v) + _dot_nt(self.head_only(qe, hh), st.astype(BF16))
                self.o_ref[0, :, vs] = o
                self.st_ref[hd] = st * decay + _dot_tn(v, kd)


def _gla_kernel(qf, kf, vf, gf, qb, kb, vb, gb, cumf, maskf, cumb, maskb, of_ref, ob_ref, stf, stb, attf, attb,
                *, chunk):
    @pl.when(pl.program_id(1) == 0)
    def _():
        stf[...] = jnp.zeros_like(stf)
        stb[...] = jnp.zeros_like(stb)

    dirs = [_GlaDirection(qf, kf, vf, gf, cumf, maskf, stf, attf, of_ref, chunk=chunk, reverse=False),
            _GlaDirection(qb, kb, vb, gb, cumb, maskb, stb, attb, ob_ref, chunk=chunk, reverse=True)]
    ok = jnp.logical_and(dirs[0].single_reference_ok(), dirs[1].single_reference_ok())

    @pl.when(ok)
    def _():
        for d in dirs:
            d.scores_single_reference()

    @pl.when(jnp.logical_not(ok))
    def _():
        for d in dirs:
            d.scores_by_level()

    for d in dirs:
        d.outputs_and_state()


def _gla(gq, gk, gv, gf, gb, *, n_latent, chunk):
    n_batch, t, hp = gq.shape
    n_lat = n_latent // chunk
    n_ctx = (t - n_latent) // chunk
    n_steps = n_lat + n_ctx

    def fwd(b, s):
        return (b, jnp.where(s < n_ctx, n_lat + s, s - n_ctx), 0)

    def bwd(b, s):
        return (b, n_steps - 1 - s, 0)

    cumf, maskf = _gla_constants(chunk, False)
    cumb, maskb = _gla_constants(chunk, True)
    tile_heads = lambda m: jnp.asarray(np.tile(m, (1, LANES // GLA_DK, 1)))
    consts = [jnp.asarray(cumf, BF16), tile_heads(maskf), jnp.asarray(cumb, BF16), tile_heads(maskb)]
    full = lambda a: pl.BlockSpec(a.shape, lambda b, s: (0,) * a.ndim)
    blk = lambda w, im: pl.BlockSpec((1, chunk, w), im)
    gv_w = gv.shape[-1]
    return pl.pallas_call(
        functools.partial(_gla_kernel, chunk=chunk),
        grid=(n_batch, n_steps),
        in_specs=[blk(hp, fwd), blk(hp, fwd), blk(gv_w, fwd), blk(hp, fwd),
                  blk(hp, bwd), blk(hp, bwd), blk(gv_w, bwd), blk(hp, bwd)] + [full(a) for a in consts],
        out_specs=[blk(gv_w, fwd), blk(gv_w, bwd)],
        out_shape=[jax.ShapeDtypeStruct((n_batch, t, gv_w), F32)] * 2,
        scratch_shapes=[pltpu.VMEM((GLA_HEADS, GLA_DV, LANES), F32)] * 2
                       + [pltpu.VMEM((GLA_HEADS * GLA_DK // LANES, chunk * LANES // GLA_DK, chunk), BF16)] * 2,
        compiler_params=_cparams(("arbitrary", "arbitrary")),
        name="gla_bidirectional",
    )(gq, gk, gv, gf, gq, gk, gv, gb, *consts)


def _flash_kernel(q_ref, k_ref, v_ref, o_ref, m_ref, acc_ref, *, ck, dv):
    j = pl.program_id(3)
    n_sub = k_ref.shape[1] // ck

    @pl.when(j == 0)
    def _():
        m_ref[...] = jnp.full_like(m_ref, -jnp.inf)
        acc_ref[...] = jnp.zeros_like(acc_ref)

    q = q_ref[0]

    for c in range(n_sub):
        off = c * ck
        s = _dot_nt(q, k_ref[0, pl.ds(off, ck), :])
        m_prev = m_ref[...]
        m_new = jnp.maximum(m_prev, jnp.max(s, axis=-1, keepdims=True))
        alpha = jnp.exp2(m_prev - m_new)
        p = jnp.concatenate([jnp.exp2(s[:, u * LANES:(u + 1) * LANES] - m_new)
                             for u in range(ck // LANES)], axis=-1).astype(BF16)
        pv = _dot(p, v_ref[0, pl.ds(off, ck), :])
        acc_ref[...] = jnp.concatenate([alpha] * (acc_ref.shape[1] // LANES), axis=-1) * acc_ref[...] + pv
        m_ref[...] = m_new

    @pl.when(j == pl.num_programs(3) - 1)
    def _():
        acc = acc_ref[...]
        denom = jnp.sum(acc[:, dv:], axis=-1, keepdims=True)
        o_ref[0] = (acc[:, :dv] / denom).astype(o_ref.dtype)


def _flash(q, k, v, prev_out, *, n_heads, dq, dv, tq, tk, ck, q_blk0, n_q, k_blk0, n_k):
    n_batch, t, _ = q.shape
    in_specs = [pl.BlockSpec((1, tq, dq), lambda b, h, i, j: (b, q_blk0 + i, h)),
                pl.BlockSpec((1, tk, dq), lambda b, h, i, j: (b, k_blk0 + j, h)),
                pl.BlockSpec((1, tk, 2 * dv), lambda b, h, i, j: (b, k_blk0 + j, h))]
    args = [q, k, v]
    aliases = {}
    base = functools.partial(_flash_kernel, ck=ck, dv=dv)
    kern = base
    if prev_out is not None:
        in_specs.append(pl.BlockSpec(memory_space=pl.ANY))
        args.append(prev_out)
        aliases = {3: 0}
        kern = lambda q_, k_, v_, _prev, *rest: base(q_, k_, v_, *rest)
    return pl.pallas_call(
        kern,
        grid=(n_batch, n_heads, n_q, n_k),
        in_specs=in_specs,
        out_specs=pl.BlockSpec((1, tq, dv), lambda b, h, i, j: (b, q_blk0 + i, h)),
        out_shape=jax.ShapeDtypeStruct((n_batch, t, n_heads * dv), BF16),
        scratch_shapes=[pltpu.VMEM((tq, LANES), F32), pltpu.VMEM((tq, 2 * dv), F32)],
        input_output_aliases=aliases,
        compiler_params=_cparams(("arbitrary",) * 4),
        name="flash_attention",
    )(*args)


def _even_out_kernel(x_ref, of_ref, ob_ref, r_ref, ng_ref, mo_ref, wa_ref, wb_ref, gl, gc, o_ref,
                     *, tm, n_latent):
    i = pl.program_id(1)
    is_ctx = _is_ctx_rows(i, tm, n_latent)
    o = of_ref[0] + ob_ref[0]
    r = r_ref[0]
    parts = []
    for hd in range(GLA_HEADS):
        sl = slice(hd * GLA_DV, (hd + 1) * GLA_DV)
        oh = o[:, sl]
        y = oh * lax.rsqrt(jnp.mean(oh * oh, axis=-1, keepdims=True) + EPS) * ng_ref[...]
        parts.append((y * _silu(r[:, sl])).astype(BF16))
    a = jnp.concatenate(parts, axis=-1)
    y = _dot(a, wa_ref[...]) + _dot(mo_ref[0], wb_ref[...])
    o_ref[0] = _post_norm(x_ref[0], y, _row_select(is_ctx, gl, gc), 1.0)


def _even_out(x, o_f, o_b, r, norm_g, mla_o, wa, wb, gate, *, n_latent, n_rows, tm):
    n_batch, _, d = x.shape
    lat, ctx = _mod_specs(n_batch, d, 2)
    row = lambda w: pl.BlockSpec((1, tm, w), lambda b, i: (b, i, 0))
    full = lambda a: pl.BlockSpec(a.shape, lambda b, i: (0,) * a.ndim)
    gv_w = o_f.shape[-1]
    return pl.pallas_call(
        functools.partial(_even_out_kernel, tm=tm, n_latent=n_latent),
        grid=(n_batch, n_rows // tm),
        in_specs=[row(d), row(gv_w), row(gv_w), row(gv_w), full(norm_g), row(mla_o.shape[-1]),
                  full(wa), full(wb), lat, ctx],
        out_specs=row(d),
        out_shape=jax.ShapeDtypeStruct((n_batch, n_rows, d), F32),
        compiler_params=_cparams(("arbitrary", "arbitrary")),
        name="even_out_postnorm",
    )(x, o_f, o_b, r, norm_g, mla_o, wa, wb, gate, gate)


def _qkv_kernel(x_ref, shl, shc, scl, scc, w_ref, o_ref, *, tm, n_latent, q_width, q_scale):
    i = pl.program_id(1)
    is_ctx = _is_ctx_rows(i, tm, n_latent)
    h = _modulate(x_ref[0], is_ctx, shl, shc, scl, scc).astype(BF16)
    p = _dot(h, w_ref[...])
    o_ref[0, :, :q_width] = (p[:, :q_width] * q_scale).astype(BF16)
    o_ref[0, :, q_width:] = p[:, q_width:].astype(BF16)


def _qkv_proj(x, mods, w, *, n_latent, tm):
    n_batch, t, d = x.shape
    sh, sc = mods
    lat, ctx = _mod_specs(n_batch, d, 2)
    n_out = w.shape[1]
    return pl.pallas_call(
        functools.partial(_qkv_kernel, tm=tm, n_latent=n_latent, q_width=n_out // 3,
                          q_scale=NA_HEAD_DIM ** -0.5 * LOG2_E),
        grid=(n_batch, t // tm),
        in_specs=[pl.BlockSpec((1, tm, d), lambda b, i: (b, i, 0)), lat, ctx, lat, ctx,
                  pl.BlockSpec(w.shape, lambda b, i: (0, 0))],
        out_specs=pl.BlockSpec((1, tm, n_out), lambda b, i: (b, i, 0)),
        out_shape=jax.ShapeDtypeStruct((n_batch, t, n_out), BF16),
        compiler_params=_cparams(("arbitrary", "arbitrary")),
        name="qkv_project",
    )(x, sh, sh, sc, sc, w)


def _pair_softmax_av(q, lane_lo, scores_fn, av_fn):
    outs = []
    for half in range(2):
        keep = lane_lo if half == 0 else jnp.logical_not(lane_lo)
        qh = jnp.where(keep, q, jnp.zeros_like(q))
        s_list = scores_fn(qh, half)
        m = s_list[0].max(axis=-1, keepdims=True)
        for s in s_list[1:]:
            m = jnp.maximum(m, s.max(axis=-1, keepdims=True))
        p_list = [jnp.exp2(s - m) for s in s_list]
        denom = p_list[0].sum(axis=-1, keepdims=True)
        for p in p_list[1:]:
            denom = denom + p.sum(axis=-1, keepdims=True)
        outs.append(av_fn([p.astype(BF16) for p in p_list]) / denom)
    return jnp.where(lane_lo, outs[0], outs[1])


def _na_kernel(q_ref, k0, k1, k2, k3, v0, v1, v2, v3, kc_ref, vc_ref, bias_ref, o_ref, *, n_blocks):
    rb = pl.program_id(2)
    kind = jnp.where(rb == 0, 0, jnp.where(rb == n_blocks - 1, 2, 1))
    q = q_ref[0].astype(F32)
    k_win = jnp.concatenate([k0[0], k1[0], k2[0], k3[0]], axis=0)
    v_win = jnp.concatenate([v0[0], v1[0], v2[0], v3[0]], axis=0)
    n_keyrows = k_win.shape[0] // GRID_W
    lane_lo = lax.broadcasted_iota(jnp.int32, (1, LANES), 1) < NA_HEAD_DIM
    ones_col = (lax.broadcasted_iota(jnp.int32, (1, LANES), 1) == 0).astype(BF16)

    def with_ones(v):
        return jnp.concatenate([v, jnp.broadcast_to(ones_col, v.shape)], axis=-1)

    k_ctx = kc_ref[0]
    v_ctx = with_ones(vc_ref[0])
    pieces = [[None] * len(NA_COL_GROUPS) for _ in range(NA_ROWS)]
    row0 = 0
    for g, (c0, gw, kc0) in enumerate(NA_COL_GROUPS):
        mg = NA_ROWS * gw
        qg = jnp.concatenate([q[i * GRID_W + c0:i * GRID_W + c0 + gw] for i in range(NA_ROWS)], axis=0)
        qs = jnp.concatenate([jnp.where(lane_lo, qg, 0.0), jnp.where(lane_lo, 0.0, qg)], axis=0).astype(BF16)
        kg = jnp.concatenate([k_win[t * GRID_W + kc0:t * GRID_W + kc0 + NA_GROUP_KEYS] for t in range(n_keyrows)], axis=0)
        vg = jnp.concatenate([v_win[t * GRID_W + kc0:t * GRID_W + kc0 + NA_GROUP_KEYS] for t in range(n_keyrows)], axis=0)
        s_w = _dot_nt(qs, kg) + bias_ref[kind, 0, pl.ds(row0, 2 * mg), :]
        s_c = _dot_nt(qs, k_ctx)
        m = jnp.maximum(jnp.max(s_w, axis=-1, keepdims=True), jnp.max(s_c, axis=-1, keepdims=True))
        pv = _dot(jnp.exp2(s_w - m).astype(BF16), with_ones(vg)) + _dot(jnp.exp2(s_c - m).astype(BF16), v_ctx)
        o = pv[:, :LANES] / jnp.sum(pv[:, LANES:], axis=-1, keepdims=True)
        og = jnp.where(lane_lo, o[:mg], o[mg:])
        for i in range(NA_ROWS):
            pieces[i][g] = og[i * gw:(i + 1) * gw]
        row0 += 2 * mg
    o_ref[0] = jnp.concatenate([pc for row in pieces for pc in row], axis=0).astype(o_ref.dtype)


def _na_bias_tables(rpb, rows):
    n_keyrows = NA_ROWS + 2 * NA_UNIT
    col = np.arange(GRID_W)
    col_start = np.clip(col - NA_KW // 2, 0, GRID_W - NA_KW)
    kc = np.arange(GRID_W)
    col_ok = (kc[None, :] >= col_start[:, None]) & (kc[None, :] < col_start[:, None] + NA_KW)
    rel_col = np.clip(kc[None, :] - col[:, None] + (NA_KW - 1), 0, 2 * NA_KW - 2)
    col_sel = ((rel_col[:, :, None] == np.arange(2 * NA_KW - 1)) & col_ok[:, :, None]).astype(np.float32)
    n_heads = rpb.shape[0]
    by_col = jnp.einsum('hab,ckb->hack', rpb, jnp.asarray(col_sel), precision=lax.Precision.HIGHEST)
    by_col = jnp.where(jnp.asarray(col_ok)[None, None], by_col * LOG2_E, MASK_NEG)
    by_col = jnp.pad(by_col, ((0, 0), (NA_ROWS, n_keyrows - NA_ROWS), (0, 0), (0, 0)), constant_values=MASK_NEG)
    n_blocks = rows // NA_ROWS
    offs, row_mask = [], []
    for blk in (0, min(1, n_blocks - 1), n_blocks - 1):
        r = blk * NA_ROWS + np.arange(NA_ROWS)
        kh = min(NA_KH, rows)
        r_start = np.clip(r - kh // 2, 0, rows - kh)
        key_row0 = int(np.clip(blk * NA_ROWS - NA_UNIT, 0, rows - n_keyrows))
        kr = key_row0 + np.arange(n_keyrows)
        row_ok = (kr[None, :] >= r_start[:, None]) & (kr[None, :] < r_start[:, None] + kh)
        offs.append(key_row0 - blk * NA_ROWS + NA_KH - 1)
        row_mask.append(np.repeat(np.where(row_ok, 0.0, MASK_NEG), NA_GROUP_KEYS, axis=1))
    row_mask = jnp.asarray(np.stack(row_mask), F32)
    width = n_keyrows * NA_GROUP_KEYS
    parts = []
    for c0, gw, kc0 in NA_COL_GROUPS:
        strip = by_col[:, :, c0:c0 + gw, kc0:kc0 + NA_GROUP_KEYS]
        strip = strip.transpose(0, 2, 1, 3).reshape(n_heads, gw, -1)
        tiles = jnp.stack([jnp.stack([strip[:, :, (NA_ROWS + off - i) * NA_GROUP_KEYS:][:, :, :width]
                                      for i in range(NA_ROWS)], axis=1) for off in offs], axis=0)
        tiles = tiles + row_mask[:, None, :, None, :]
        parts.append(tiles.reshape(3, n_heads // 2, 2 * NA_ROWS * gw, width))
    return jnp.concatenate(parts, axis=2)


def _na_attention(qkv, bias, *, n_latent):
    n_batch, t, _ = qkv.shape
    rows = n_latent // GRID_W
    n_blocks = rows // NA_ROWS
    n_pairs = NA_HEADS // 2
    tq = NA_ROWS * GRID_W
    tu = NA_UNIT * GRID_W
    n_units = rows // NA_UNIT
    ctx_blk = n_latent // tu

    def unit(u):
        def im(p, b, rb):
            u0 = jnp.clip(rb * (NA_ROWS // NA_UNIT) - 1, 0, n_units - 4)
            return u0 + u
        return im

    k_specs = [pl.BlockSpec((1, tu, LANES), lambda p, b, rb, f=unit(u): (b, f(p, b, rb), n_pairs + p))
               for u in range(4)]
    v_specs = [pl.BlockSpec((1, tu, LANES), lambda p, b, rb, f=unit(u): (b, f(p, b, rb), 2 * n_pairs + p))
               for u in range(4)]
    return pl.pallas_call(
        functools.partial(_na_kernel, n_blocks=n_blocks),
        grid=(n_pairs, n_batch, n_blocks),
        in_specs=[pl.BlockSpec((1, tq, LANES), lambda p, b, rb: (b, rb, p))] + k_specs + v_specs
                 + [pl.BlockSpec((1, tu, LANES), lambda p, b, rb: (b, ctx_blk, n_pairs + p)),
                    pl.BlockSpec((1, tu, LANES), lambda p, b, rb: (b, ctx_blk, 2 * n_pairs + p)),
                    pl.BlockSpec((3, 1) + bias.shape[2:], lambda p, b, rb: (0, p, 0, 0))],
        out_specs=pl.BlockSpec((1, tq, LANES), lambda p, b, rb: (b, rb, p)),
        out_shape=jax.ShapeDtypeStruct((n_batch, t, NA_HEADS * NA_HEAD_DIM), BF16),
        compiler_params=_cparams(("arbitrary",) * 3),
        name="neighbourhood_attention",
    )(qkv, *([qkv] * 10), bias)


def _ctx_pair_kernel(q_ref, k_ref, v_ref, _prev, o_ref):
    k = k_ref[0]
    v = v_ref[0]
    lane_lo = lax.broadcasted_iota(jnp.int32, (1, LANES), 1) < NA_HEAD_DIM
    o_ref[0] = _pair_softmax_av(q_ref[0], lane_lo, lambda qh, half: [_dot_nt(qh, k)],
                                lambda ps: _dot(ps[0], v)).astype(o_ref.dtype)


def _ctx_pair_attention(qkv, prev_out, *, n_latent):
    n_batch, t, _ = qkv.shape
    n_ctx = t - n_latent
    blk = n_latent // n_ctx
    n_pairs = NA_HEADS // 2
    spec = lambda off: pl.BlockSpec((1, n_ctx, LANES), lambda b, p: (b, blk, off + p))
    return pl.pallas_call(
        _ctx_pair_kernel,
        grid=(n_batch, n_pairs),
        in_specs=[spec(0), spec(n_pairs), spec(2 * n_pairs), pl.BlockSpec(memory_space=pl.ANY)],
        out_specs=spec(0),
        out_shape=jax.ShapeDtypeStruct(prev_out.shape, prev_out.dtype),
        input_output_aliases={3: 0},
        compiler_params=_cparams(("arbitrary", "arbitrary")),
        name="context_pair_attention",
    )(qkv, qkv, qkv, prev_out)


def _odd_out_kernel(x_ref, a_ref, w_ref, gl, gc, o_ref, *, tm, n_latent):
    i = pl.program_id(1)
    is_ctx = _is_ctx_rows(i, tm, n_latent)
    y = _dot(a_ref[0], w_ref[...])
    o_ref[0] = _post_norm(x_ref[0], y, _row_select(is_ctx, gl, gc), 1.0)


def _odd_out(x, a, w, gate, *, n_latent, n_rows, tm):
    n_batch, _, d = x.shape
    lat, ctx = _mod_specs(n_batch, d, 2)
    row = lambda wd: pl.BlockSpec((1, tm, wd), lambda b, i: (b, i, 0))
    return pl.pallas_call(
        functools.partial(_odd_out_kernel, tm=tm, n_latent=n_latent),
        grid=(n_batch, n_rows // tm),
        in_specs=[row(d), row(a.shape[-1]), pl.BlockSpec(w.shape, lambda b, i: (0, 0)), lat, ctx],
        out_specs=row(d),
        out_shape=jax.ShapeDtypeStruct((n_batch, n_rows, d), F32),
        compiler_params=_cparams(("arbitrary", "arbitrary")),
        name="odd_out_postnorm",
    )(x, a, w, gate, gate)


def _even_weights(w_in, wg2_f, bg_f, wg2_b, bg_b, q_norm_g, kv_norm_g, w_uq, w_ukv):
    sizes = (GLA_HEADS * GLA_DK, GLA_HEADS * GLA_DK, GLA_HEADS * GLA_DV, GLA_HEADS * GLA_DV,
             GLA_RANK, GLA_RANK, MLA_D_CQ, MLA_D_CKV, MLA_D_ROPE)
    q_g, k_g, v_g, r_g, lr_f, lr_b, c_q, c_kv, k_r = jnp.split(w_in, np.cumsum(sizes)[:-1].tolist(), axis=-1)
    swap = np.arange(MLA_D_ROPE) ^ 1
    d = w_in.shape[0]
    zeros = lambda n: jnp.zeros((d, n), w_in.dtype)
    w1 = jnp.concatenate([
        q_g, k_g, v_g, r_g,
        lr_f, lr_b, zeros(LANES - 2 * GLA_RANK), c_q, c_kv,
        k_r, zeros(LANES - MLA_D_ROPE), k_r[:, swap], zeros(LANES - MLA_D_ROPE)], axis=-1).astype(BF16)
    hp = GLA_HEADS * GLA_DK
    wg = jnp.zeros((LANES, 2 * hp), F32)
    wg = wg.at[:GLA_RANK, :hp].set(wg2_f)
    wg = wg.at[GLA_RANK:2 * GLA_RANK, hp:].set(wg2_b)
    bg = jnp.concatenate([bg_f, bg_b])[None]
    uq = w_uq.reshape(MLA_D_CQ, MLA_HEADS, MLA_D_NOPE + MLA_D_ROPE)
    nope, rope = uq[..., :MLA_D_NOPE], uq[..., MLA_D_NOPE:]
    zr = jnp.zeros((MLA_D_CQ, MLA_HEADS, LANES - MLA_D_ROPE), w_uq.dtype)
    wq_main = jnp.concatenate([nope, rope, zr], axis=-1).reshape(MLA_D_CQ, -1).astype(BF16)
    wq_swap = jnp.concatenate([rope[..., swap], zr], axis=-1).reshape(MLA_D_CQ, -1).astype(BF16)
    ukv = w_ukv.reshape(MLA_D_CKV, MLA_HEADS, MLA_D_NOPE + MLA_D_V)
    wkv = jnp.concatenate([ukv[..., :MLA_D_NOPE].reshape(MLA_D_CKV, -1),
                           ukv[..., MLA_D_NOPE:].reshape(MLA_D_CKV, -1)], axis=-1).astype(BF16)
    return (w1, wg.astype(BF16), bg, q_norm_g[None], kv_norm_g[None], wq_main, wq_swap, wkv)


def _rope_tables(n_latent, n_ctx):
    t = jnp.arange(n_latent)
    row = (t // GRID_W).astype(F32)
    col = (t % GRID_W).astype(F32)
    n_freq = MLA_D_ROPE // 4
    inv = ROPE_BASE ** (-jnp.arange(n_freq, dtype=F32) / n_freq)
    ang = jnp.concatenate([row[:, None] * inv, col[:, None] * inv], axis=-1)
    cos = jnp.repeat(jnp.cos(ang), 2, axis=-1)
    sin = jnp.repeat(jnp.sin(ang), 2, axis=-1) * jnp.tile(jnp.array([-1.0, 1.0], F32), MLA_D_ROPE // 2)
    cos = jnp.concatenate([cos, jnp.ones((n_ctx, MLA_D_ROPE), F32)])
    sin = jnp.concatenate([sin, jnp.zeros((n_ctx, MLA_D_ROPE), F32)])
    pad = jnp.zeros((n_latent + n_ctx, LANES - MLA_D_ROPE), F32)
    return jnp.concatenate([cos, pad], axis=-1), jnp.concatenate([sin, pad], axis=-1)


def _row_tile(n_rows, target):
    best = 8
    for cand in range(8, target + 1, 8):
        if n_rows % cand == 0:
            best = cand
    return best


def kernel(x, c, ctx, c_ctx, ada_w, ada_b, ffn1_w_in, ffn1_w_out, ffn2_w_in, ffn2_w_out, even_w_in, gla_wg2_f, gla_bg_f, gla_wg2_b, gla_bg_b, gla_norm_g, mla_q_norm_g, mla_kv_norm_g, mla_w_uq, mla_w_ukv, even_w_out, na_w_in, na_rpb, na_w_out):
    n_batch, n_latent, d = x.shape
    n_ctx = ctx.shape[1]
    t_all = n_latent + n_ctx
    assert n_batch + 1 <= 8 and n_latent % (NA_ROWS * GRID_W) == 0 and n_ctx == NA_UNIT * GRID_W
    assert n_latent % GLA_CHUNK == 0 and n_ctx % GLA_CHUNK == 0

    xa = jnp.concatenate([x, ctx], axis=1)
    cc = jnp.concatenate([c, c_ctx[None], jnp.zeros((7 - n_batch, d), F32)], axis=0)
    mods = _ada_modulation(cc, ada_w, ada_b)
    mods = mods[:, :n_batch + 1].reshape(DEPTH, n_batch + 1, 9, 1, d).transpose(0, 2, 1, 3, 4)

    tm_ffn = _row_tile(t_all, 768)
    tm_ffn_lat = _row_tile(n_latent, 768)
    tm_row = _row_tile(t_all, 384)
    tm_row_lat = _row_tile(n_latent, 512)
    rope_c, rope_s = _rope_tables(n_latent, n_ctx)
    tq = _row_tile(n_latent, 1024)
    ck = next(cand for cand in (768, 512, 256, 128) if t_all % cand == 0)
    ck_ctx = next(cand for cand in (768, 512, 256, 128) if n_ctx % cand == 0)

    for l in range(DEPTH):
        last = l == DEPTH - 1
        i = l // 2
        m = mods[l]
        xa = _ffn(xa, (m[0], m[1], m[2]), ffn1_w_in[l].astype(BF16), ffn1_w_out[l].astype(BF16),
                  n_latent=n_latent, n_rows=t_all, tm=tm_ffn, n_chunks=2)
        n_rows = n_latent if last else t_all
        tm_out = tm_row_lat if last else tm_row
        if l % 2 == 0:
            wts = _even_weights(even_w_in[i], gla_wg2_f[i], gla_bg_f[i], gla_wg2_b[i], gla_bg_b[i],
                                mla_q_norm_g[i], mla_kv_norm_g[i], mla_w_uq[i], mla_w_ukv[i])
            gq, gk, gv, r, gf, gb, qm, km, vm = _even_proj(xa, (m[3], m[4]), wts, rope_c, rope_s,
                                                           n_latent=n_latent, tm=tm_row)
            o_f, o_b = _gla(gq, gk, gv, gf, gb, n_latent=n_latent, chunk=GLA_CHUNK)
            mla_o = _flash(qm, km, vm, None, n_heads=MLA_HEADS, dq=2 * LANES, dv=MLA_D_V, tq=tq, tk=t_all, ck=ck,
                           q_blk0=0, n_q=n_latent // tq, k_blk0=0, n_k=1)
            if not last:
                mla_o = _flash(qm, km, vm, mla_o, n_heads=MLA_HEADS, dq=2 * LANES, dv=MLA_D_V,
                               tq=n_ctx, tk=n_ctx, ck=ck_ctx, q_blk0=n_latent // n_ctx, n_q=1,
                               k_blk0=n_latent // n_ctx, n_k=1)
            w_out = even_w_out[i].astype(BF16)
            gv_w = GLA_HEADS * GLA_DV
            xa = _even_out(xa, o_f, o_b, r, gla_norm_g[i][None], mla_o, w_out[:gv_w], w_out[gv_w:], m[5],
                           n_latent=n_latent, n_rows=n_rows, tm=tm_out)
        else:
            qkv = _qkv_proj(xa, (m[3], m[4]), na_w_in[i].astype(BF16), n_latent=n_latent, tm=tm_row)
            bias = _na_bias_tables(na_rpb[i], n_latent // GRID_W)
            att = _na_attention(qkv, bias, n_latent=n_latent)
            if not last:
                att = _ctx_pair_attention(qkv, att, n_latent=n_latent)
            xa = _odd_out(xa, att, na_w_out[i].astype(BF16), m[5], n_latent=n_latent, n_rows=n_rows, tm=tm_out)
        xa = _ffn(xa, (m[6], m[7], m[8]), ffn2_w_in[l].astype(BF16), ffn2_w_out[l].astype(BF16),
                  n_latent=n_latent, n_rows=n_rows, tm=tm_ffn_lat if last else tm_ffn, n_chunks=2)
    return xa
```

```python
import functools

import numpy as np
import jax
import jax.numpy as jnp
from jax import lax
from jax.experimental import pallas as pl
from jax.experimental.pallas import tpu as pltpu

DEPTH = 4
GRID_W = 64
D_FF = 2816
GLA_HEADS = 4
GLA_DK = 64
GLA_DV = 128
GLA_RANK = 16
GLA_TAU = 16.0
MLA_HEADS = 4
MLA_D_NOPE = 128
MLA_D_ROPE = 64
MLA_D_V = 128
MLA_D_CQ = 384
MLA_D_CKV = 128
MLA_SCALE = (MLA_D_NOPE + MLA_D_ROPE) ** -0.5
LOG2_E = 1.4426950408889634
MLA_QSCALE = MLA_SCALE * LOG2_E
NA_HEADS = 16
NA_HEAD_DIM = 64
NA_KH = 8
NA_KW = 16
ROPE_BASE = 10000.0
EPS = 1e-6
ALPHA = (2 * DEPTH) ** 0.25

LANES = 128
VMEM_LIMIT = 56 * 1024 * 1024

GLA_CHUNK = 256
NA_ROWS = 8
NA_UNIT = 4
MASK_NEG = -1e30
GLA_FAST_MAX_DECAY = 60.0
GLA_FAST_MAX_KEY = 1e10
NA_COL_GROUPS = ((0, 24, 0), (24, 16, 16), (40, 24, 32))
NA_GROUP_KEYS = 32

BF16 = jnp.bfloat16
F32 = jnp.float32


def _cparams(sem):
    return pltpu.CompilerParams(dimension_semantics=sem, vmem_limit_bytes=VMEM_LIMIT)


def _dot(a, b):
    return jnp.dot(a, b, preferred_element_type=F32)


def _dot_nt(a, b):
    return lax.dot_general(a, b, (((1,), (1,)), ((), ())), preferred_element_type=F32)


def _dot_tn(a, b):
    return lax.dot_general(a, b, (((0,), (0,)), ((), ())), preferred_element_type=F32)


def _silu(v):
    return v * (1.0 / (1.0 + jnp.exp(-v)))


def _row_select(is_ctx, lat_ref, ctx_ref):
    return jnp.where(is_ctx, ctx_ref[0], lat_ref[0])


def _is_ctx_rows(tile_idx, tm, n_latent):
    rows = tile_idx * tm + lax.broadcasted_iota(jnp.int32, (tm, 1), 0)
    return rows >= n_latent


def _modulate(x, is_ctx, sh_l, sh_c, sc_l, sc_c):
    shift = _row_select(is_ctx, sh_l, sh_c)
    scale = _row_select(is_ctx, sc_l, sc_c)
    return x * (1.0 + scale) + shift


def _post_norm(x, y, gate, coef):
    z = ALPHA * x + (coef * gate) * y
    mu = jnp.mean(z, axis=-1, keepdims=True)
    zc = z - mu
    var = jnp.mean(zc * zc, axis=-1, keepdims=True)
    return zc * lax.rsqrt(var + EPS)


def _mod_specs(n_batch, d, grid_rank):
    if grid_rank == 2:
        lat = pl.BlockSpec((1, 1, d), lambda b, i: (b, 0, 0))
        ctx = pl.BlockSpec((1, 1, d), lambda b, i: (n_batch, 0, 0))
    else:
        lat = pl.BlockSpec((1, 1, d), lambda b, i, j: (b, 0, 0))
        ctx = pl.BlockSpec((1, 1, d), lambda b, i, j: (n_batch, 0, 0))
    return lat, ctx


def _ada_kernel(c_ref, w_ref, b_ref, o_ref):
    a = _silu(c_ref[...]).astype(BF16)
    o_ref[0] = _dot(a, w_ref[0].astype(BF16)) + b_ref[0]


def _ada_modulation(cc, ada_w, ada_b):
    depth, d, n9 = ada_w.shape
    tn = n9 // 8
    return pl.pallas_call(
        _ada_kernel,
        grid=(depth, n9 // tn),
        in_specs=[pl.BlockSpec((8, d), lambda l, j: (0, 0)),
                  pl.BlockSpec((1, d, tn), lambda l, j: (l, 0, j)),
                  pl.BlockSpec((1, 1, tn), lambda l, j: (l, 0, j))],
        out_specs=pl.BlockSpec((1, 8, tn), lambda l, j: (l, 0, j)),
        out_shape=jax.ShapeDtypeStruct((depth, 8, n9), F32),
        compiler_params=_cparams(("arbitrary", "arbitrary")),
        name="ada_modulation",
    )(cc, ada_w, ada_b.reshape(depth, 1, n9))


def _gla_gate_norm(o, r, norm_g):
    parts = []
    for hd in range(GLA_HEADS):
        sl = slice(hd * GLA_DV, (hd + 1) * GLA_DV)
        oh = o[:, sl]
        y = oh * lax.rsqrt(jnp.mean(oh * oh, axis=-1, keepdims=True) + EPS) * norm_g
        parts.append((y * _silu(r[:, sl])).astype(BF16))
    return jnp.concatenate(parts, axis=-1)


def _ffn_kernel(*refs, tm, n_latent, n_chunks, mixer):
    x_ref, shl, shc, scl, scc, gl, gc, wi_ref, wo_ref = refs[:9]
    o_ref = refs[-1]
    i = pl.program_id(1)
    is_ctx = _is_ctx_rows(i, tm, n_latent)
    x = x_ref[0]
    if mixer == "even":
        of_ref, ob_ref, r_ref, ng_ref, mo_ref, wa_ref, wb_ref, g2l, g2c = refs[9:-1]
        a = _gla_gate_norm(of_ref[0] + ob_ref[0], r_ref[0], ng_ref[...])
        y_mix = _dot(a, wa_ref[...]) + _dot(mo_ref[0], wb_ref[...])
        x = _post_norm(x, y_mix, _row_select(is_ctx, g2l, g2c), 1.0)
    elif mixer == "odd":
        a_ref, wp_ref, g2l, g2c = refs[9:-1]
        x = _post_norm(x, _dot(a_ref[0], wp_ref[...]), _row_select(is_ctx, g2l, g2c), 1.0)
    h = _modulate(x, is_ctx, shl, shc, scl, scc).astype(BF16)
    d_ff = wo_ref.shape[0]
    tf = d_ff // n_chunks
    y = None
    for c in range(n_chunks):
        gate = _dot(h, wi_ref[:, c * tf:(c + 1) * tf])
        up = _dot(h, wi_ref[:, d_ff + c * tf:d_ff + (c + 1) * tf])
        act = (_silu(gate) * up).astype(BF16)
        part = _dot(act, wo_ref[c * tf:(c + 1) * tf, :])
        y = part if y is None else y + part
    o_ref[0] = _post_norm(x, y, _row_select(is_ctx, gl, gc), 0.5)


def _ffn(x, mods, w_in, w_out, *, n_latent, n_rows, tm, n_chunks=1, mixer=None, mixer_rows=(), mixer_consts=(),
         mixer_gate=None):
    n_batch, _, d = x.shape
    sh, sc, g = mods
    lat, ctx = _mod_specs(n_batch, d, 2)
    resident = lambda w: pl.BlockSpec(w.shape, lambda b, i: (0,) * w.ndim, pipeline_mode=pl.Buffered(1))
    row = lambda a: pl.BlockSpec((1, tm, a.shape[-1]), lambda b, i: (b, i, 0))
    if mixer == "even":
        o_f, o_b, r, mla_o = mixer_rows
        norm_g, w_a, w_b = mixer_consts
        extra = [o_f, o_b, r, norm_g, mla_o, w_a, w_b, mixer_gate, mixer_gate]
        extra_specs = [row(o_f), row(o_b), row(r), resident(norm_g), row(mla_o), resident(w_a), resident(w_b), lat, ctx]
    elif mixer == "odd":
        (att,), (w_p,) = mixer_rows, mixer_consts
        extra = [att, w_p, mixer_gate, mixer_gate]
        extra_specs = [row(att), resident(w_p), lat, ctx]
    else:
        extra, extra_specs = [], []
    return pl.pallas_call(
        functools.partial(_ffn_kernel, tm=tm, n_latent=n_latent, n_chunks=n_chunks, mixer=mixer),
        grid=(n_batch, n_rows // tm),
        in_specs=[pl.BlockSpec((1, tm, d), lambda b, i: (b, i, 0)),
                  lat, ctx, lat, ctx, lat, ctx, resident(w_in), resident(w_out)] + extra_specs,
        out_specs=pl.BlockSpec((1, tm, d), lambda b, i: (b, i, 0)),
        out_shape=jax.ShapeDtypeStruct((n_batch, n_rows, d), F32),
        compiler_params=_cparams(("arbitrary", "arbitrary")),
        name="ffn_postnorm" if mixer is None else "mixer_out_ffn_postnorm",
    )(x, sh, sh, sc, sc, g, g, w_in, w_out, *extra)


def _even_proj_kernel(x_ref, shl, shc, scl, scc, w1_ref, wg_ref, bg_ref, qng_ref, kvng_ref,
                      wqm_ref, wqs_ref, wkv_ref, cb_ref, sb_ref,
                      gq_ref, gk_ref, gv_ref, r_ref, gf_ref, gb_ref, qm_ref, km_ref, vm_ref,
                      *, tm, n_latent):
    i = pl.program_id(1)
    is_ctx = _is_ctx_rows(i, tm, n_latent)
    h = _modulate(x_ref[0], is_ctx, shl, shc, scl, scc).astype(BF16)
    p = _dot(h, w1_ref[...])
    hp = GLA_HEADS * GLA_DK
    gv_w = GLA_HEADS * GLA_DV
    o0 = 0
    gq_ref[0] = p[:, o0:o0 + hp] * (GLA_DK ** -0.5)
    o0 += hp
    gk_ref[0] = p[:, o0:o0 + hp]
    o0 += hp
    gv_ref[0] = p[:, o0:o0 + gv_w].astype(BF16)
    o0 += gv_w
    r_ref[0] = p[:, o0:o0 + gv_w]
    o0 += gv_w
    lr = p[:, o0:o0 + LANES].astype(BF16)
    o0 += LANES
    cq = p[:, o0:o0 + MLA_D_CQ]
    o0 += MLA_D_CQ
    ckv = p[:, o0:o0 + MLA_D_CKV]
    o0 += MLA_D_CKV
    kr = p[:, o0:o0 + LANES]
    o0 += LANES
    krs = p[:, o0:o0 + LANES]

    z = _dot(lr, wg_ref[...]) + bg_ref[...]
    logg = (jnp.minimum(z, 0.0) - jnp.log1p(jnp.exp(-jnp.abs(z)))) / GLA_TAU
    gf_ref[0] = logg[:, :hp]
    gb_ref[0] = logg[:, hp:]

    cb = cb_ref[...]
    sb = sb_ref[...]
    cqn = (cq * lax.rsqrt(jnp.mean(cq * cq, axis=-1, keepdims=True) + EPS) * qng_ref[...]).astype(BF16)
    qmain = _dot(cqn, wqm_ref[...])
    qswap = _dot(cqn, wqs_ref[...])
    ckn = (ckv * lax.rsqrt(jnp.mean(ckv * ckv, axis=-1, keepdims=True) + EPS) * kvng_ref[...]).astype(BF16)
    kv = _dot(ckn, wkv_ref[...])
    k_rope = (kr * cb + krs * sb).astype(BF16)
    ones_col = (lax.broadcasted_iota(jnp.int32, (1, LANES), 1) == 0).astype(BF16)
    for hd in range(MLA_HEADS):
        b0 = 2 * LANES * hd
        qm_ref[0, :, b0:b0 + LANES] = (qmain[:, b0:b0 + LANES] * MLA_QSCALE).astype(BF16)
        q_rope = qmain[:, b0 + LANES:b0 + 2 * LANES] * cb + qswap[:, LANES * hd:LANES * (hd + 1)] * sb
        qm_ref[0, :, b0 + LANES:b0 + 2 * LANES] = (q_rope * MLA_QSCALE).astype(BF16)
        km_ref[0, :, b0:b0 + LANES] = kv[:, LANES * hd:LANES * (hd + 1)].astype(BF16)
        km_ref[0, :, b0 + LANES:b0 + 2 * LANES] = k_rope
        v0 = MLA_HEADS * MLA_D_NOPE + MLA_D_V * hd
        vm_ref[0, :, b0:b0 + LANES] = kv[:, v0:v0 + MLA_D_V].astype(BF16)
        vm_ref[0, :, b0 + LANES:b0 + 2 * LANES] = jnp.broadcast_to(ones_col, (tm, LANES))


def _even_proj(x, mods, wts, rope_c, rope_s, *, n_latent, tm):
    n_batch, t, d = x.shape
    sh, sc = mods
    lat, ctx = _mod_specs(n_batch, d, 2)
    full = lambda a: pl.BlockSpec(a.shape, lambda b, i: (0,) * a.ndim)
    row = lambda w: pl.BlockSpec((1, tm, w), lambda b, i: (b, i, 0))
    hp = GLA_HEADS * GLA_DK
    gv_w = GLA_HEADS * GLA_DV
    out_w = [(hp, F32), (hp, F32), (gv_w, BF16), (gv_w, F32), (hp, F32), (hp, F32),
             (MLA_HEADS * 2 * LANES, BF16), (MLA_HEADS * 2 * LANES, BF16), (MLA_HEADS * 2 * MLA_D_V, BF16)]
    return pl.pallas_call(
        functools.partial(_even_proj_kernel, tm=tm, n_latent=n_latent),
        grid=(n_batch, t // tm),
        in_specs=[row(d), lat, ctx, lat, ctx] + [full(w) for w in wts]
                 + [pl.BlockSpec((tm, LANES), lambda b, i: (i, 0))] * 2,
        out_specs=[row(w) for w, _ in out_w],
        out_shape=[jax.ShapeDtypeStruct((n_batch, t, w), dt) for w, dt in out_w],
        compiler_params=_cparams(("arbitrary", "arbitrary")),
        name="even_project",
    )(x, sh, sh, sc, sc, *wts, rope_c, rope_s)


def _gla_constants(chunk, reverse):
    n_lvl = int(np.log2(chunk))
    pos = np.arange(chunk)
    src = pos[None, :]
    tri = src <= pos[:, None]
    cum, mask = [], []
    for lvl in range(n_lvl):
        size = chunk >> lvl
        half = size // 2
        ref = ((pos // size) * size + half - 1)[:, None]
        upper = (pos % size) >= half
        a_up = (src > ref) & (src <= pos[:, None])
        a_lo = (src > pos[:, None]) & (src <= ref)
        cum.append(np.where(upper[:, None], a_up, a_lo))
        same = (pos[:, None] // size) == (pos[None, :] // size)
        mask.append(same & upper[:, None] & ~upper[None, :])
    mask.append(np.any(mask, axis=0) | (np.eye(chunk, dtype=bool) & (not reverse)))
    cum = np.stack([tri, ~tri] + cum).astype(np.float32)
    mask = np.stack(mask).astype(np.float32)
    if reverse:
        cum = cum[:, ::-1, ::-1]
        mask = mask[:, ::-1, ::-1]
    return np.ascontiguousarray(cum).reshape(-1, chunk), np.ascontiguousarray(mask)


class _GlaDirection:
    def __init__(self, q_ref, k_ref, v_ref, g_ref, cum_ref, mask_ref, st_ref, att_ref, o_ref, *, chunk, reverse):
        self.q_ref, self.k_ref, self.v_ref = q_ref, k_ref, v_ref
        self.cum_ref, self.mask_ref, self.st_ref, self.att_ref, self.o_ref = cum_ref, mask_ref, st_ref, att_ref, o_ref
        self.chunk, self.reverse = chunk, reverse
        self.n_lvl = mask_ref.shape[0] - 1
        self.heads_per_group = LANES // GLA_DK
        self.groups = [slice(grp * LANES, (grp + 1) * LANES) for grp in range(GLA_HEADS * GLA_DK // LANES)]
        g = g_ref[0]
        self.g_hi = g.astype(BF16)
        self.g_lo = (g - self.g_hi.astype(F32)).astype(BF16)
        base = self.partial_sums(slice(0, 2 * chunk))
        self.b = base[:chunk]
        self.rest = base[chunk:]
        last = 0 if reverse else chunk - 1
        self.b_last = self.b[last:last + 1, :]
        self.lane_lo = lax.broadcasted_iota(jnp.int32, (1, LANES), 1) < GLA_DK

    def partial_sums(self, rows):
        return _dot(self.cum_ref[rows, :], self.g_hi) + _dot(self.cum_ref[rows, :], self.g_lo)

    def head_only(self, x, hh):
        return jnp.where(self.lane_lo if hh == 0 else jnp.logical_not(self.lane_lo), x, jnp.zeros_like(x))

    def stack_heads(self, x):
        return jnp.concatenate([self.head_only(x, hh) for hh in range(self.heads_per_group)], axis=0)

    def single_reference_ok(self):
        return jnp.logical_and(jnp.max(-self.b_last) <= GLA_FAST_MAX_DECAY,
                               jnp.max(jnp.abs(self.k_ref[0])) <= GLA_FAST_MAX_KEY)

    def scores_single_reference(self):
        for grp, sl in enumerate(self.groups):
            b = self.b[:, sl]
            qe = (self.q_ref[0, :, sl] * jnp.exp(b)).astype(BF16)
            ki = (self.k_ref[0, :, sl] * jnp.exp(-b)).astype(BF16)
            att = jnp.where(self.mask_ref[self.n_lvl] > 0.0, _dot_nt(self.stack_heads(qe), ki), 0.0)
            self.att_ref[grp] = att.astype(BF16)

    def scores_by_level(self):
        chunk, n_lvl = self.chunk, self.n_lvl
        rows = self.heads_per_group * chunk
        sums = self.partial_sums(slice(2 * chunk, (n_lvl + 2) * chunk))
        eye = ((lax.broadcasted_iota(jnp.int32, (rows, chunk), 0) & (chunk - 1))
               == lax.broadcasted_iota(jnp.int32, (rows, chunk), 1)).astype(F32)
        for grp, sl in enumerate(self.groups):
            q = self.q_ref[0, :, sl]
            k = self.k_ref[0, :, sl]
            att = jnp.zeros((rows, chunk), F32)
            for lvl in range(n_lvl):
                w = jnp.exp(sums[lvl * chunk:(lvl + 1) * chunk, sl])
                att = att + _dot_nt(self.stack_heads((q * w).astype(BF16)), (k * w).astype(BF16)) * self.mask_ref[lvl]
            if not self.reverse:
                att = att + _dot_nt(self.stack_heads(q.astype(BF16)), k.astype(BF16)) * eye
            self.att_ref[grp] = att.astype(BF16)

    def outputs_and_state(self):
        chunk = self.chunk
        for grp, sl in enumerate(self.groups):
            q = self.q_ref[0, :, sl]
            k = self.k_ref[0, :, sl]
            qe = (q * jnp.exp(self.b[:, sl])).astype(BF16)
            kd = (k * jnp.exp(self.rest[:, sl])).astype(BF16)
            decay = jnp.exp(self.b_last[:, sl])
            att = self.att_ref[grp]
            for hh in range(self.heads_per_group):
                hd = grp * self.heads_per_group + hh
                vs = slice(hd * GLA_DV, (hd + 1) * GLA_DV)
                v = self.v_ref[0, :, vs]
                st = self.st_ref[hd]
                o = _dot(att[hh * chunk:(hh + 1) * chunk], v) + _dot_nt(self.head_only(qe, hh), st.astype(BF16))
                self.o_ref[0, :, vs] = o
                self.st_ref[hd] = st * decay + _dot_tn(v, kd)


def _gla_kernel(qf, kf, vf, gf, qb, kb, vb, gb, cumf, maskf, cumb, maskb, of_ref, ob_ref, stf, stb, attf, attb,
                *, chunk):
    @pl.when(pl.program_id(1) == 0)
    def _():
        stf[...] = jnp.zeros_like(stf)
        stb[...] = jnp.zeros_like(stb)

    dirs = [_GlaDirection(qf, kf, vf, gf, cumf, maskf, stf, attf, of_ref, chunk=chunk, reverse=False),
            _GlaDirection(qb, kb, vb, gb, cumb, maskb, stb, attb, ob_ref, chunk=chunk, reverse=True)]
    ok = jnp.logical_and(dirs[0].single_reference_ok(), dirs[1].single_reference_ok())

    @pl.when(ok)
    def _():
        for d in dirs:
            d.scores_single_reference()

    @pl.when(jnp.logical_not(ok))
    def _():
        for d in dirs:
            d.scores_by_level()

    for d in dirs:
        d.outputs_and_state()


def _gla(gq, gk, gv, gf, gb, *, n_latent, chunk):
    n_batch, t, hp = gq.shape
    n_lat = n_latent // chunk
    n_ctx = (t - n_latent) // chunk
    n_steps = n_lat + n_ctx

    def fwd(b, s):
        return (b, jnp.where(s < n_ctx, n_lat + s, s - n_ctx), 0)

    def bwd(b, s):
        return (b, n_steps - 1 - s, 0)

    cumf, maskf = _gla_constants(chunk, False)
    cumb, maskb = _gla_constants(chunk, True)
    tile_heads = lambda m: jnp.asarray(np.tile(m, (1, LANES // GLA_DK, 1)))
    consts = [jnp.asarray(cumf, BF16), tile_heads(maskf), jnp.asarray(cumb, BF16), tile_heads(maskb)]
    full = lambda a: pl.BlockSpec(a.shape, lambda b, s: (0,) * a.ndim)
    blk = lambda w, im: pl.BlockSpec((1, chunk, w), im)
    gv_w = gv.shape[-1]
    return pl.pallas_call(
        functools.partial(_gla_kernel, chunk=chunk),
        grid=(n_batch, n_steps),
        in_specs=[blk(hp, fwd), blk(hp, fwd), blk(gv_w, fwd), blk(hp, fwd),
                  blk(hp, bwd), blk(hp, bwd), blk(gv_w, bwd), blk(hp, bwd)] + [full(a) for a in consts],
        out_specs=[blk(gv_w, fwd), blk(gv_w, bwd)],
        out_shape=[jax.ShapeDtypeStruct((n_batch, t, gv_w), F32)] * 2,
        scratch_shapes=[pltpu.VMEM((GLA_HEADS, GLA_DV, LANES), F32)] * 2
                       + [pltpu.VMEM((GLA_HEADS * GLA_DK // LANES, chunk * LANES // GLA_DK, chunk), BF16)] * 2,
        compiler_params=_cparams(("arbitrary", "arbitrary")),
        name="gla_bidirectional",
    )(gq, gk, gv, gf, gq, gk, gv, gb, *consts)


def _flash_kernel(q_ref, k_ref, v_ref, o_ref, m_ref, acc_ref, *, ck, dv):
    j = pl.program_id(3)
    n_sub = k_ref.shape[1] // ck

    @pl.when(j == 0)
    def _():
        m_ref[...] = jnp.full_like(m_ref, -jnp.inf)
        acc_ref[...] = jnp.zeros_like(acc_ref)

    q = q_ref[0]

    for c in range(n_sub):
        off = c * ck
        s = _dot_nt(q, k_ref[0, pl.ds(off, ck), :])
        m_prev = m_ref[...]
        m_new = jnp.maximum(m_prev, jnp.max(s, axis=-1, keepdims=True))
        alpha = jnp.exp2(m_prev - m_new)
        p = jnp.concatenate([jnp.exp2(s[:, u * LANES:(u + 1) * LANES] - m_new)
                             for u in range(ck // LANES)], axis=-1).astype(BF16)
        pv = _dot(p, v_ref[0, pl.ds(off, ck), :])
        acc_ref[...] = jnp.concatenate([alpha] * (acc_ref.shape[1] // LANES), axis=-1) * acc_ref[...] + pv
        m_ref[...] = m_new

    @pl.when(j == pl.num_programs(3) - 1)
    def _():
        acc = acc_ref[...]
        denom = jnp.sum(acc[:, dv:], axis=-1, keepdims=True)
        o_ref[0] = (acc[:, :dv] / denom).astype(o_ref.dtype)


def _flash(q, k, v, *, n_heads, dq, dv, tq, tk, ck, q_blk0, n_q, k_blk0, n_k):
    n_batch = q.shape[0]
    return pl.pallas_call(
        functools.partial(_flash_kernel, ck=ck, dv=dv),
        grid=(n_batch, n_heads, n_q, n_k),
        in_specs=[pl.BlockSpec((1, tq, dq), lambda b, h, i, j: (b, q_blk0 + i, h)),
                  pl.BlockSpec((1, tk, dq), lambda b, h, i, j: (b, k_blk0 + j, h)),
                  pl.BlockSpec((1, tk, 2 * dv), lambda b, h, i, j: (b, k_blk0 + j, h))],
        out_specs=pl.BlockSpec((1, tq, dv), lambda b, h, i, j: (b, i, h)),
        out_shape=jax.ShapeDtypeStruct((n_batch, n_q * tq, n_heads * dv), BF16),
        scratch_shapes=[pltpu.VMEM((tq, LANES), F32), pltpu.VMEM((tq, 2 * dv), F32)],
        compiler_params=_cparams(("arbitrary",) * 4),
        name="flash_attention",
    )(q, k, v)


def _qkv_kernel(x_ref, shl, shc, scl, scc, w_ref, o_ref, *, tm, n_latent, q_width, q_scale):
    i = pl.program_id(1)
    is_ctx = _is_ctx_rows(i, tm, n_latent)
    h = _modulate(x_ref[0], is_ctx, shl, shc, scl, scc).astype(BF16)
    p = _dot(h, w_ref[...])
    o_ref[0, :, :q_width] = (p[:, :q_width] * q_scale).astype(BF16)
    o_ref[0, :, q_width:] = p[:, q_width:].astype(BF16)


def _qkv_proj(x, mods, w, *, n_latent, tm):
    n_batch, t, d = x.shape
    sh, sc = mods
    lat, ctx = _mod_specs(n_batch, d, 2)
    n_out = w.shape[1]
    return pl.pallas_call(
        functools.partial(_qkv_kernel, tm=tm, n_latent=n_latent, q_width=n_out // 3,
                          q_scale=NA_HEAD_DIM ** -0.5 * LOG2_E),
        grid=(n_batch, t // tm),
        in_specs=[pl.BlockSpec((1, tm, d), lambda b, i: (b, i, 0)), lat, ctx, lat, ctx,
                  pl.BlockSpec(w.shape, lambda b, i: (0, 0))],
        out_specs=pl.BlockSpec((1, tm, n_out), lambda b, i: (b, i, 0)),
        out_shape=jax.ShapeDtypeStruct((n_batch, t, n_out), BF16),
        compiler_params=_cparams(("arbitrary", "arbitrary")),
        name="qkv_project",
    )(x, sh, sh, sc, sc, w)


def _pair_softmax_av(q, lane_lo, scores_fn, av_fn):
    outs = []
    for half in range(2):
        keep = lane_lo if half == 0 else jnp.logical_not(lane_lo)
        qh = jnp.where(keep, q, jnp.zeros_like(q))
        s_list = scores_fn(qh, half)
        m = s_list[0].max(axis=-1, keepdims=True)
        for s in s_list[1:]:
            m = jnp.maximum(m, s.max(axis=-1, keepdims=True))
        p_list = [jnp.exp2(s - m) for s in s_list]
        denom = p_list[0].sum(axis=-1, keepdims=True)
        for p in p_list[1:]:
            denom = denom + p.sum(axis=-1, keepdims=True)
        outs.append(av_fn([p.astype(BF16) for p in p_list]) / denom)
    return jnp.where(lane_lo, outs[0], outs[1])


def _na_kernel(q_ref, k0, k1, k2, k3, v0, v1, v2, v3, kc_ref, vc_ref, bias_ref, o_ref, *, n_blocks):
    rb = pl.program_id(2)
    kind = jnp.where(rb == 0, 0, jnp.where(rb == n_blocks - 1, 2, 1))
    q = q_ref[0].astype(F32)
    k_win = jnp.concatenate([k0[0], k1[0], k2[0], k3[0]], axis=0)
    v_win = jnp.concatenate([v0[0], v1[0], v2[0], v3[0]], axis=0)
    n_keyrows = k_win.shape[0] // GRID_W
    lane_lo = lax.broadcasted_iota(jnp.int32, (1, LANES), 1) < NA_HEAD_DIM
    ones_col = (lax.broadcasted_iota(jnp.int32, (1, LANES), 1) == 0).astype(BF16)

    def with_ones(v):
        return jnp.concatenate([v, jnp.broadcast_to(ones_col, v.shape)], axis=-1)

    k_ctx = kc_ref[0]
    v_ctx = with_ones(vc_ref[0])
    pieces = [[None] * len(NA_COL_GROUPS) for _ in range(NA_ROWS)]
    row0 = 0
    for g, (c0, gw, kc0) in enumerate(NA_COL_GROUPS):
        mg = NA_ROWS * gw
        qg = jnp.concatenate([q[i * GRID_W + c0:i * GRID_W + c0 + gw] for i in range(NA_ROWS)], axis=0)
        qs = jnp.concatenate([jnp.where(lane_lo, qg, 0.0), jnp.where(lane_lo, 0.0, qg)], axis=0).astype(BF16)
        kg = jnp.concatenate([k_win[t * GRID_W + kc0:t * GRID_W + kc0 + NA_GROUP_KEYS] for t in range(n_keyrows)], axis=0)
        vg = jnp.concatenate([v_win[t * GRID_W + kc0:t * GRID_W + kc0 + NA_GROUP_KEYS] for t in range(n_keyrows)], axis=0)
        s_w = _dot_nt(qs, kg) + bias_ref[kind, 0, pl.ds(row0, 2 * mg), :]
        s_c = _dot_nt(qs, k_ctx)
        m = jnp.maximum(jnp.max(s_w, axis=-1, keepdims=True), jnp.max(s_c, axis=-1, keepdims=True))
        pv = _dot(jnp.exp2(s_w - m).astype(BF16), with_ones(vg)) + _dot(jnp.exp2(s_c - m).astype(BF16), v_ctx)
        o = pv[:, :LANES] / jnp.sum(pv[:, LANES:], axis=-1, keepdims=True)
        og = jnp.where(lane_lo, o[:mg], o[mg:])
        for i in range(NA_ROWS):
            pieces[i][g] = og[i * gw:(i + 1) * gw]
        row0 += 2 * mg
    o_ref[0] = jnp.concatenate([pc for row in pieces for pc in row], axis=0).astype(o_ref.dtype)


def _na_bias_tables(rpb, rows):
    n_keyrows = NA_ROWS + 2 * NA_UNIT
    col = np.arange(GRID_W)
    col_start = np.clip(col - NA_KW // 2, 0, GRID_W - NA_KW)
    kc = np.arange(GRID_W)
    col_ok = (kc[None, :] >= col_start[:, None]) & (kc[None, :] < col_start[:, None] + NA_KW)
    rel_col = np.clip(kc[None, :] - col[:, None] + (NA_KW - 1), 0, 2 * NA_KW - 2)
    col_sel = ((rel_col[:, :, None] == np.arange(2 * NA_KW - 1)) & col_ok[:, :, None]).astype(np.float32)
    n_heads = rpb.shape[0]
    by_col = jnp.einsum('hab,ckb->hack', rpb, jnp.asarray(col_sel), precision=lax.Precision.HIGHEST)
    by_col = jnp.where(jnp.asarray(col_ok)[None, None], by_col * LOG2_E, MASK_NEG)
    by_col = jnp.pad(by_col, ((0, 0), (NA_ROWS, n_keyrows - NA_ROWS), (0, 0), (0, 0)), constant_values=MASK_NEG)
    n_blocks = rows // NA_ROWS
    offs, row_mask = [], []
    for blk in (0, min(1, n_blocks - 1), n_blocks - 1):
        r = blk * NA_ROWS + np.arange(NA_ROWS)
        kh = min(NA_KH, rows)
        r_start = np.clip(r - kh // 2, 0, rows - kh)
        key_row0 = int(np.clip(blk * NA_ROWS - NA_UNIT, 0, rows - n_keyrows))
        kr = key_row0 + np.arange(n_keyrows)
        row_ok = (kr[None, :] >= r_start[:, None]) & (kr[None, :] < r_start[:, None] + kh)
        offs.append(key_row0 - blk * NA_ROWS + NA_KH - 1)
        row_mask.append(np.repeat(np.where(row_ok, 0.0, MASK_NEG), NA_GROUP_KEYS, axis=1))
    row_mask = jnp.asarray(np.stack(row_mask), F32)
    width = n_keyrows * NA_GROUP_KEYS
    parts = []
    for c0, gw, kc0 in NA_COL_GROUPS:
        strip = by_col[:, :, c0:c0 + gw, kc0:kc0 + NA_GROUP_KEYS]
        strip = strip.transpose(0, 2, 1, 3).reshape(n_heads, gw, -1)
        tiles = jnp.stack([jnp.stack([strip[:, :, (NA_ROWS + off - i) * NA_GROUP_KEYS:][:, :, :width]
                                      for i in range(NA_ROWS)], axis=1) for off in offs], axis=0)
        tiles = tiles + row_mask[:, None, :, None, :]
        parts.append(tiles.reshape(3, n_heads // 2, 2 * NA_ROWS * gw, width))
    return jnp.concatenate(parts, axis=2)


def _na_attention(qkv, bias, *, n_latent):
    n_batch, t, _ = qkv.shape
    rows = n_latent // GRID_W
    n_blocks = rows // NA_ROWS
    n_pairs = NA_HEADS // 2
    tq = NA_ROWS * GRID_W
    tu = NA_UNIT * GRID_W
    n_units = rows // NA_UNIT
    ctx_blk = n_latent // tu

    def unit(u):
        def im(p, b, rb):
            u0 = jnp.clip(rb * (NA_ROWS // NA_UNIT) - 1, 0, n_units - 4)
            return u0 + u
        return im

    k_specs = [pl.BlockSpec((1, tu, LANES), lambda p, b, rb, f=unit(u): (b, f(p, b, rb), n_pairs + p))
               for u in range(4)]
    v_specs = [pl.BlockSpec((1, tu, LANES), lambda p, b, rb, f=unit(u): (b, f(p, b, rb), 2 * n_pairs + p))
               for u in range(4)]
    return pl.pallas_call(
        functools.partial(_na_kernel, n_blocks=n_blocks),
        grid=(n_pairs, n_batch, n_blocks),
        in_specs=[pl.BlockSpec((1, tq, LANES), lambda p, b, rb: (b, rb, p))] + k_specs + v_specs
                 + [pl.BlockSpec((1, tu, LANES), lambda p, b, rb: (b, ctx_blk, n_pairs + p)),
                    pl.BlockSpec((1, tu, LANES), lambda p, b, rb: (b, ctx_blk, 2 * n_pairs + p)),
                    pl.BlockSpec((3, 1) + bias.shape[2:], lambda p, b, rb: (0, p, 0, 0))],
        out_specs=pl.BlockSpec((1, tq, LANES), lambda p, b, rb: (b, rb, p)),
        out_shape=jax.ShapeDtypeStruct((n_batch, n_latent, NA_HEADS * NA_HEAD_DIM), BF16),
        compiler_params=_cparams(("arbitrary",) * 3),
        name="neighbourhood_attention",
    )(qkv, *([qkv] * 10), bias)


def _ctx_pair_kernel(q_ref, k_ref, v_ref, o_ref):
    k = k_ref[0]
    v = v_ref[0]
    lane_lo = lax.broadcasted_iota(jnp.int32, (1, LANES), 1) < NA_HEAD_DIM
    o_ref[0] = _pair_softmax_av(q_ref[0], lane_lo, lambda qh, half: [_dot_nt(qh, k)],
                                lambda ps: _dot(ps[0], v)).astype(o_ref.dtype)


def _ctx_pair_attention(qkv, *, n_latent):
    n_batch, t, _ = qkv.shape
    n_ctx = t - n_latent
    blk = n_latent // n_ctx
    n_pairs = NA_HEADS // 2
    spec = lambda off: pl.BlockSpec((1, n_ctx, LANES), lambda b, p: (b, blk, off + p))
    return pl.pallas_call(
        _ctx_pair_kernel,
        grid=(n_batch, n_pairs),
        in_specs=[spec(0), spec(n_pairs), spec(2 * n_pairs)],
        out_specs=pl.BlockSpec((1, n_ctx, LANES), lambda b, p: (b, 0, p)),
        out_shape=jax.ShapeDtypeStruct((n_batch, n_ctx, NA_HEADS * NA_HEAD_DIM), BF16),
        compiler_params=_cparams(("arbitrary", "arbitrary")),
        name="context_pair_attention",
    )(qkv, qkv, qkv)


def _even_weights(w_in, wg2_f, bg_f, wg2_b, bg_b, q_norm_g, kv_norm_g, w_uq, w_ukv):
    sizes = (GLA_HEADS * GLA_DK, GLA_HEADS * GLA_DK, GLA_HEADS * GLA_DV, GLA_HEADS * GLA_DV,
             GLA_RANK, GLA_RANK, MLA_D_CQ, MLA_D_CKV, MLA_D_ROPE)
    q_g, k_g, v_g, r_g, lr_f, lr_b, c_q, c_kv, k_r = jnp.split(w_in, np.cumsum(sizes)[:-1].tolist(), axis=-1)
    swap = np.arange(MLA_D_ROPE) ^ 1
    d = w_in.shape[0]
    zeros = lambda n: jnp.zeros((d, n), w_in.dtype)
    w1 = jnp.concatenate([
        q_g, k_g, v_g, r_g,
        lr_f, lr_b, zeros(LANES - 2 * GLA_RANK), c_q, c_kv,
        k_r, zeros(LANES - MLA_D_ROPE), k_r[:, swap], zeros(LANES - MLA_D_ROPE)], axis=-1).astype(BF16)
    hp = GLA_HEADS * GLA_DK
    wg = jnp.zeros((LANES, 2 * hp), F32)
    wg = wg.at[:GLA_RANK, :hp].set(wg2_f)
    wg = wg.at[GLA_RANK:2 * GLA_RANK, hp:].set(wg2_b)
    bg = jnp.concatenate([bg_f, bg_b])[None]
    uq = w_uq.reshape(MLA_D_CQ, MLA_HEADS, MLA_D_NOPE + MLA_D_ROPE)
    nope, rope = uq[..., :MLA_D_NOPE], uq[..., MLA_D_NOPE:]
    zr = jnp.zeros((MLA_D_CQ, MLA_HEADS, LANES - MLA_D_ROPE), w_uq.dtype)
    wq_main = jnp.concatenate([nope, rope, zr], axis=-1).reshape(MLA_D_CQ, -1).astype(BF16)
    wq_swap = jnp.concatenate([rope[..., swap], zr], axis=-1).reshape(MLA_D_CQ, -1).astype(BF16)
    ukv = w_ukv.reshape(MLA_D_CKV, MLA_HEADS, MLA_D_NOPE + MLA_D_V)
    wkv = jnp.concatenate([ukv[..., :MLA_D_NOPE].reshape(MLA_D_CKV, -1),
                           ukv[..., MLA_D_NOPE:].reshape(MLA_D_CKV, -1)], axis=-1).astype(BF16)
    return (w1, wg.astype(BF16), bg, q_norm_g[None], kv_norm_g[None], wq_main, wq_swap, wkv)


def _rope_tables(n_latent, n_ctx):
    t = jnp.arange(n_latent)
    row = (t // GRID_W).astype(F32)
    col = (t % GRID_W).astype(F32)
    n_freq = MLA_D_ROPE // 4
    inv = ROPE_BASE ** (-jnp.arange(n_freq, dtype=F32) / n_freq)
    ang = jnp.concatenate([row[:, None] * inv, col[:, None] * inv], axis=-1)
    cos = jnp.repeat(jnp.cos(ang), 2, axis=-1)
    sin = jnp.repeat(jnp.sin(ang), 2, axis=-1) * jnp.tile(jnp.array([-1.0, 1.0], F32), MLA_D_ROPE // 2)
    cos = jnp.concatenate([cos, jnp.ones((n_ctx, MLA_D_ROPE), F32)])
    sin = jnp.concatenate([sin, jnp.zeros((n_ctx, MLA_D_ROPE), F32)])
    pad = jnp.zeros((n_latent + n_ctx, LANES - MLA_D_ROPE), F32)
    return jnp.concatenate([cos, pad], axis=-1), jnp.concatenate([sin, pad], axis=-1)


def _row_tile(n_rows, target):
    best = 8
    for cand in range(8, target + 1, 8):
        if n_rows % cand == 0:
            best = cand
    return best


def kernel(x, c, ctx, c_ctx, ada_w, ada_b, ffn1_w_in, ffn1_w_out, ffn2_w_in, ffn2_w_out, even_w_in, gla_wg2_f, gla_bg_f, gla_wg2_b, gla_bg_b, gla_norm_g, mla_q_norm_g, mla_kv_norm_g, mla_w_uq, mla_w_ukv, even_w_out, na_w_in, na_rpb, na_w_out):
    n_batch, n_latent, d = x.shape
    n_ctx = ctx.shape[1]
    t_all = n_latent + n_ctx
    assert n_batch + 1 <= 8 and n_latent % (NA_ROWS * GRID_W) == 0 and n_ctx == NA_UNIT * GRID_W
    assert n_latent % GLA_CHUNK == 0 and n_ctx % GLA_CHUNK == 0

    xa = jnp.concatenate([x, ctx], axis=1)
    cc = jnp.concatenate([c, c_ctx[None], jnp.zeros((7 - n_batch, d), F32)], axis=0)
    mods = _ada_modulation(cc, ada_w, ada_b)
    mods = mods[:, :n_batch + 1].reshape(DEPTH, n_batch + 1, 9, 1, d).transpose(0, 2, 1, 3, 4)

    tm_ffn = _row_tile(t_all, 768)
    tm_row = _row_tile(t_all, 384)
    rope_c, rope_s = _rope_tables(n_latent, n_ctx)
    tq = _row_tile(n_latent, 1024)
    ck = next(cand for cand in (768, 512, 256, 128) if t_all % cand == 0)
    ck_ctx = next(cand for cand in (768, 512, 256, 128) if n_ctx % cand == 0)

    for l in range(DEPTH):
        last = l == DEPTH - 1
        i = l // 2
        m = mods[l]
        xa = _ffn(xa, (m[0], m[1], m[2]), ffn1_w_in[l].astype(BF16), ffn1_w_out[l].astype(BF16),
                  n_latent=n_latent, n_rows=t_all, tm=tm_ffn)
        n_rows = n_latent if last else t_all
        tm_out = _row_tile(n_rows, 512 if last else 384)
        if l % 2 == 0:
            wts = _even_weights(even_w_in[i], gla_wg2_f[i], gla_bg_f[i], gla_wg2_b[i], gla_bg_b[i],
                                mla_q_norm_g[i], mla_kv_norm_g[i], mla_w_uq[i], mla_w_ukv[i])
            gq, gk, gv, r, gf, gb, qm, km, vm = _even_proj(xa, (m[3], m[4]), wts, rope_c, rope_s,
                                                           n_latent=n_latent, tm=tm_row)
            o_f, o_b = _gla(gq, gk, gv, gf, gb, n_latent=n_latent, chunk=GLA_CHUNK)
            mla_o = _flash(qm, km, vm, n_heads=MLA_HEADS, dq=2 * LANES, dv=MLA_D_V, tq=tq, tk=t_all, ck=ck,
                           q_blk0=0, n_q=n_latent // tq, k_blk0=0, n_k=1)
            if not last:
                mla_c = _flash(qm, km, vm, n_heads=MLA_HEADS, dq=2 * LANES, dv=MLA_D_V,
                               tq=n_ctx, tk=n_ctx, ck=ck_ctx, q_blk0=n_latent // n_ctx, n_q=1,
                               k_blk0=n_latent // n_ctx, n_k=1)
                mla_o = jnp.concatenate([mla_o, mla_c], axis=1)
            w_out = even_w_out[i].astype(BF16)
            gv_w = GLA_HEADS * GLA_DV
            mixer = dict(mixer="even", mixer_rows=(o_f, o_b, r, mla_o),
                         mixer_consts=(gla_norm_g[i][None], w_out[:gv_w], w_out[gv_w:]))
        else:
            qkv = _qkv_proj(xa, (m[3], m[4]), na_w_in[i].astype(BF16), n_latent=n_latent, tm=tm_row)
            bias = _na_bias_tables(na_rpb[i], n_latent // GRID_W)
            att = _na_attention(qkv, bias, n_latent=n_latent)
            if not last:
                att = jnp.concatenate([att, _ctx_pair_attention(qkv, n_latent=n_latent)], axis=1)
            mixer = dict(mixer="odd", mixer_rows=(att,), mixer_consts=(na_w_out[i].astype(BF16),))
        xa = _ffn(xa, (m[6], m[7], m[8]), ffn2_w_in[l].astype(BF16), ffn2_w_out[l].astype(BF16),
                  n_latent=n_latent, n_rows=n_rows, tm=tm_out, mixer_gate=m[5], **mixer)
    return xa
```

```python
import functools

import numpy as np
import jax
import jax.numpy as jnp
from jax import lax
from jax.experimental import pallas as pl
from jax.experimental.pallas import tpu as pltpu

DEPTH = 4
GRID_W = 64
D_FF = 2816
GLA_HEADS = 4
GLA_DK = 64
GLA_DV = 128
GLA_RANK = 16
GLA_TAU = 16.0
MLA_HEADS = 4
MLA_D_NOPE = 128
MLA_D_ROPE = 64
MLA_D_V = 128
MLA_D_CQ = 384
MLA_D_CKV = 128
MLA_SCALE = (MLA_D_NOPE + MLA_D_ROPE) ** -0.5
LOG2_E = 1.4426950408889634
MLA_QSCALE = MLA_SCALE * LOG2_E
NA_HEADS = 16
NA_HEAD_DIM = 64
NA_KH = 8
NA_KW = 16
ROPE_BASE = 10000.0
EPS = 1e-6
ALPHA = (2 * DEPTH) ** 0.25

LANES = 128
VMEM_LIMIT = 56 * 1024 * 1024

GLA_CHUNK = 256
NA_ROWS = 8
NA_SUB = 4
NA_UNIT = 4
MASK_NEG = -1e30
GLA_FAST_MAX_DECAY = 60.0
GLA_FAST_MAX_KEY = 1e10
NA_COL_GROUPS = ((0, 24, 0), (24, 16, 16), (40, 24, 32))
NA_GROUP_KEYS = 32

BF16 = jnp.bfloat16
F32 = jnp.float32


def _cparams(sem):
    return pltpu.CompilerParams(dimension_semantics=sem, vmem_limit_bytes=VMEM_LIMIT)


def _dot(a, b):
    return jnp.dot(a, b, preferred_element_type=F32)


def _dot_nt(a, b):
    return lax.dot_general(a, b, (((1,), (1,)), ((), ())), preferred_element_type=F32)


def _dot_tn(a, b):
    return lax.dot_general(a, b, (((0,), (0,)), ((), ())), preferred_element_type=F32)


def _silu(v):
    return v * (1.0 / (1.0 + jnp.exp(-v)))


def _row_select(is_ctx, lat_ref, ctx_ref):
    return jnp.where(is_ctx, ctx_ref[0], lat_ref[0])


def _is_ctx_rows(tile_idx, tm, n_latent):
    rows = tile_idx * tm + lax.broadcasted_iota(jnp.int32, (tm, 1), 0)
    return rows >= n_latent


def _modulate(x, is_ctx, sh_l, sh_c, sc_l, sc_c):
    shift = _row_select(is_ctx, sh_l, sh_c)
    scale = _row_select(is_ctx, sc_l, sc_c)
    return x * (1.0 + scale) + shift


def _post_norm(x, y, gate, coef):
    z = ALPHA * x + (coef * gate) * y
    mu = jnp.mean(z, axis=-1, keepdims=True)
    zc = z - mu
    var = jnp.mean(zc * zc, axis=-1, keepdims=True)
    return zc * lax.rsqrt(var + EPS)


def _mod_specs(n_batch, d, layer, k):
    blk = (None, None, 1, 1, d)
    return (pl.BlockSpec(blk, lambda b, i: (layer, k, b, 0, 0)),
            pl.BlockSpec(blk, lambda b, i: (layer, k, n_batch, 0, 0)))


def _ada_kernel(c_ref, w_ref, b_ref, o_ref):
    a = _silu(c_ref[...]).astype(BF16)
    o_ref[0] = _dot(a, w_ref[0].astype(BF16)) + b_ref[0]


def _ada_modulation(cc, ada_w, ada_b):
    depth, d, n9 = ada_w.shape
    tn = n9 // 8
    return pl.pallas_call(
        _ada_kernel,
        grid=(depth, n9 // tn),
        in_specs=[pl.BlockSpec((8, d), lambda l, j: (0, 0)),
                  pl.BlockSpec((1, d, tn), lambda l, j: (l, 0, j)),
                  pl.BlockSpec((1, 1, tn), lambda l, j: (l, 0, j))],
        out_specs=pl.BlockSpec((1, 8, tn), lambda l, j: (l, 0, j)),
        out_shape=jax.ShapeDtypeStruct((depth, 8, n9), F32),
        compiler_params=_cparams(("arbitrary", "arbitrary")),
        name="ada_modulation",
    )(cc, ada_w, ada_b.reshape(depth, 1, n9))


def _gla_gate_norm(o, r, norm_g):
    parts = []
    for hd in range(GLA_HEADS):
        sl = slice(hd * GLA_DV, (hd + 1) * GLA_DV)
        oh = o[:, sl]
        y = oh * lax.rsqrt(jnp.mean(oh * oh, axis=-1, keepdims=True) + EPS) * norm_g
        parts.append((y * _silu(r[:, sl])).astype(BF16))
    return jnp.concatenate(parts, axis=-1)


def _ffn_kernel(*refs, tm, n_latent, n_chunks, mixer):
    x_ref, shl, shc, scl, scc, gl, gc, wi_ref, wo_ref = refs[:9]
    o_ref = refs[-1]
    i = pl.program_id(1)
    is_ctx = _is_ctx_rows(i, tm, n_latent)
    x = x_ref[0]
    if mixer == "even":
        of_ref, ob_ref, r_ref, ng_ref, mo_ref, wa_ref, wb_ref, g2l, g2c = refs[9:-1]
        a = _gla_gate_norm(of_ref[0] + ob_ref[0], r_ref[0], ng_ref[...])
        y_mix = _dot(a, wa_ref[...]) + _dot(mo_ref[0], wb_ref[...])
        x = _post_norm(x, y_mix, _row_select(is_ctx, g2l, g2c), 1.0)
    elif mixer == "odd":
        a_ref, wp_ref, g2l, g2c = refs[9:-1]
        x = _post_norm(x, _dot(a_ref[0], wp_ref[...]), _row_select(is_ctx, g2l, g2c), 1.0)
    h = _modulate(x, is_ctx, shl, shc, scl, scc).astype(BF16)
    d_ff = wo_ref.shape[0]
    tf = d_ff // n_chunks
    y = None
    for c in range(n_chunks):
        gate = _dot(h, wi_ref[:, c * tf:(c + 1) * tf])
        up = _dot(h, wi_ref[:, d_ff + c * tf:d_ff + (c + 1) * tf])
        act = (_silu(gate) * up).astype(BF16)
        part = _dot(act, wo_ref[c * tf:(c + 1) * tf, :])
        y = part if y is None else y + part
    o_ref[0] = _post_norm(x, y, _row_select(is_ctx, gl, gc), 0.5)


def _ffn(x, mods, layer, ks, w_in, w_out, *, n_latent, n_rows, tm, n_chunks=1, mixer=None, mixer_rows=(),
         mixer_consts=(), mixer_gate=None):
    n_batch, _, d = x.shape
    mod_specs = [spec for k in ks for spec in _mod_specs(n_batch, d, layer, k)]
    gate_specs = list(_mod_specs(n_batch, d, layer, mixer_gate)) if mixer else []
    resident = lambda w: pl.BlockSpec(w.shape, lambda b, i: (0,) * w.ndim, pipeline_mode=pl.Buffered(1))
    row = lambda a: pl.BlockSpec((1, tm, a.shape[-1]), lambda b, i: (b, i, 0))
    if mixer == "even":
        o_f, o_b, r, mla_o = mixer_rows
        norm_g, w_a, w_b = mixer_consts
        extra = [o_f, o_b, r, norm_g, mla_o, w_a, w_b, mods, mods]
        extra_specs = [row(o_f), row(o_b), row(r), resident(norm_g), row(mla_o), resident(w_a), resident(w_b)] + gate_specs
    elif mixer == "odd":
        (att,), (w_p,) = mixer_rows, mixer_consts
        extra = [att, w_p, mods, mods]
        extra_specs = [row(att), resident(w_p)] + gate_specs
    else:
        extra, extra_specs = [], []
    return pl.pallas_call(
        functools.partial(_ffn_kernel, tm=tm, n_latent=n_latent, n_chunks=n_chunks, mixer=mixer),
        grid=(n_batch, n_rows // tm),
        in_specs=[pl.BlockSpec((1, tm, d), lambda b, i: (b, i, 0))] + mod_specs
                 + [resident(w_in), resident(w_out)] + extra_specs,
        out_specs=pl.BlockSpec((1, tm, d), lambda b, i: (b, i, 0)),
        out_shape=jax.ShapeDtypeStruct((n_batch, n_rows, d), F32),
        compiler_params=_cparams(("arbitrary", "arbitrary")),
        name="ffn_postnorm" if mixer is None else "mixer_out_ffn_postnorm",
    )(x, *([mods] * 6), w_in, w_out, *extra)


def _even_proj_kernel(x_ref, shl, shc, scl, scc, w1_ref, wg_ref, bg_ref, qng_ref, kvng_ref,
                      wqm_ref, wqs_ref, wkv_ref, cb_ref, sb_ref,
                      gq_ref, gk_ref, gv_ref, r_ref, gf_ref, gb_ref, qm_ref, km_ref, vm_ref,
                      *, tm, n_latent):
    i = pl.program_id(1)
    is_ctx = _is_ctx_rows(i, tm, n_latent)
    h = _modulate(x_ref[0], is_ctx, shl, shc, scl, scc).astype(BF16)
    p = _dot(h, w1_ref[...])
    hp = GLA_HEADS * GLA_DK
    gv_w = GLA_HEADS * GLA_DV
    o0 = 0
    gq_ref[0] = p[:, o0:o0 + hp] * (GLA_DK ** -0.5)
    o0 += hp
    gk_ref[0] = p[:, o0:o0 + hp]
    o0 += hp
    gv_ref[0] = p[:, o0:o0 + gv_w].astype(BF16)
    o0 += gv_w
    r_ref[0] = p[:, o0:o0 + gv_w]
    o0 += gv_w
    lr = p[:, o0:o0 + LANES].astype(BF16)
    o0 += LANES
    cq = p[:, o0:o0 + MLA_D_CQ]
    o0 += MLA_D_CQ
    ckv = p[:, o0:o0 + MLA_D_CKV]
    o0 += MLA_D_CKV
    kr = p[:, o0:o0 + LANES]
    o0 += LANES
    krs = p[:, o0:o0 + LANES]

    z = _dot(lr, wg_ref[...]) + bg_ref[...]
    logg = (jnp.minimum(z, 0.0) - jnp.log1p(jnp.exp(-jnp.abs(z)))) / GLA_TAU
    gf_ref[0] = logg[:, :hp]
    gb_ref[0] = logg[:, hp:]

    cb = cb_ref[...]
    sb = sb_ref[...]
    cqn = (cq * lax.rsqrt(jnp.mean(cq * cq, axis=-1, keepdims=True) + EPS) * qng_ref[...]).astype(BF16)
    qmain = _dot(cqn, wqm_ref[...])
    qswap = _dot(cqn, wqs_ref[...])
    ckn = (ckv * lax.rsqrt(jnp.mean(ckv * ckv, axis=-1, keepdims=True) + EPS) * kvng_ref[...]).astype(BF16)
    kv = _dot(ckn, wkv_ref[...])
    k_rope = (kr * cb + krs * sb).astype(BF16)
    ones_col = (lax.broadcasted_iota(jnp.int32, (1, LANES), 1) == 0).astype(BF16)
    for hd in range(MLA_HEADS):
        b0 = 2 * LANES * hd
        qm_ref[0, :, b0:b0 + LANES] = (qmain[:, b0:b0 + LANES] * MLA_QSCALE).astype(BF16)
        q_rope = qmain[:, b0 + LANES:b0 + 2 * LANES] * cb + qswap[:, LANES * hd:LANES * (hd + 1)] * sb
        qm_ref[0, :, b0 + LANES:b0 + 2 * LANES] = (q_rope * MLA_QSCALE).astype(BF16)
        km_ref[0, :, b0:b0 + LANES] = kv[:, LANES * hd:LANES * (hd + 1)].astype(BF16)
        km_ref[0, :, b0 + LANES:b0 + 2 * LANES] = k_rope
        v0 = MLA_HEADS * MLA_D_NOPE + MLA_D_V * hd
        vm_ref[0, :, b0:b0 + LANES] = kv[:, v0:v0 + MLA_D_V].astype(BF16)
        vm_ref[0, :, b0 + LANES:b0 + 2 * LANES] = jnp.broadcast_to(ones_col, (tm, LANES))


def _even_proj(x, mods, layer, ks, wts, rope_c, rope_s, *, n_latent, tm):
    n_batch, t, d = x.shape
    mod_specs = [spec for k in ks for spec in _mod_specs(n_batch, d, layer, k)]
    full = lambda a: pl.BlockSpec(a.shape, lambda b, i: (0,) * a.ndim)
    row = lambda w: pl.BlockSpec((1, tm, w), lambda b, i: (b, i, 0))
    hp = GLA_HEADS * GLA_DK
    gv_w = GLA_HEADS * GLA_DV
    out_w = [(hp, F32), (hp, F32), (gv_w, BF16), (gv_w, F32), (hp, F32), (hp, F32),
             (MLA_HEADS * 2 * LANES, BF16), (MLA_HEADS * 2 * LANES, BF16), (MLA_HEADS * 2 * MLA_D_V, BF16)]
    return pl.pallas_call(
        functools.partial(_even_proj_kernel, tm=tm, n_latent=n_latent),
        grid=(n_batch, t // tm),
        in_specs=[row(d)] + mod_specs + [full(w) for w in wts]
                 + [pl.BlockSpec((tm, LANES), lambda b, i: (i, 0))] * 2,
        out_specs=[row(w) for w, _ in out_w],
        out_shape=[jax.ShapeDtypeStruct((n_batch, t, w), dt) for w, dt in out_w],
        compiler_params=_cparams(("arbitrary", "arbitrary")),
        name="even_project",
    )(x, *([mods] * 4), *wts, rope_c, rope_s)


def _gla_constants(chunk, reverse):
    n_lvl = int(np.log2(chunk))
    pos = np.arange(chunk)
    src = pos[None, :]
    tri = src <= pos[:, None]
    cum, mask = [], []
    for lvl in range(n_lvl):
        size = chunk >> lvl
        half = size // 2
        ref = ((pos // size) * size + half - 1)[:, None]
        upper = (pos % size) >= half
        a_up = (src > ref) & (src <= pos[:, None])
        a_lo = (src > pos[:, None]) & (src <= ref)
        cum.append(np.where(upper[:, None], a_up, a_lo))
        same = (pos[:, None] // size) == (pos[None, :] // size)
        mask.append(same & upper[:, None] & ~upper[None, :])
    mask.append(np.any(mask, axis=0) | (np.eye(chunk, dtype=bool) & (not reverse)))
    cum = np.stack([tri, ~tri] + cum).astype(np.float32)
    mask = np.stack(mask).astype(np.float32)
    if reverse:
        cum = cum[:, ::-1, ::-1]
        mask = mask[:, ::-1, ::-1]
    return np.ascontiguousarray(cum).reshape(-1, chunk), np.ascontiguousarray(mask)


class _GlaDirection:
    def __init__(self, q_ref, k_ref, v_ref, g_ref, cum_ref, mask_ref, st_ref, att_ref, o_ref, *, chunk, reverse):
        self.q_ref, self.k_ref, self.v_ref = q_ref, k_ref, v_ref
        self.cum_ref, self.mask_ref, self.st_ref, self.att_ref, self.o_ref = cum_ref, mask_ref, st_ref, att_ref, o_ref
        self.chunk, self.reverse = chunk, reverse
        self.n_lvl = mask_ref.shape[0] - 1
        self.heads_per_group = LANES // GLA_DK
        self.groups = [slice(grp * LANES, (grp + 1) * LANES) for grp in range(GLA_HEADS * GLA_DK // LANES)]
        g = g_ref[0]
        self.g_hi = g.astype(BF16)
        self.g_lo = (g - self.g_hi.astype(F32)).astype(BF16)
        base = self.partial_sums(slice(0, 2 * chunk))
        self.b = base[:chunk]
        self.rest = base[chunk:]
        last = 0 if reverse else chunk - 1
        self.b_last = self.b[last:last + 1, :]
        self.lane_lo = lax.broadcasted_iota(jnp.int32, (1, LANES), 1) < GLA_DK

    def partial_sums(self, rows):
        return _dot(self.cum_ref[rows, :], self.g_hi) + _dot(self.cum_ref[rows, :], self.g_lo)

    def head_only(self, x, hh):
        return jnp.where(self.lane_lo if hh == 0 else jnp.logical_not(self.lane_lo), x, jnp.zeros_like(x))

    def stack_heads(self, x):
        return jnp.concatenate([self.head_only(x, hh) for hh in range(self.heads_per_group)], axis=0)

    def single_reference_ok(self):
        return jnp.logical_and(jnp.max(-self.b_last) <= GLA_FAST_MAX_DECAY,
                               jnp.max(jnp.abs(self.k_ref[0])) <= GLA_FAST_MAX_KEY)

    def scores_single_reference(self):
        for grp, sl in enumerate(self.groups):
            b = self.b[:, sl]
            qe = (self.q_ref[0, :, sl] * jnp.exp(b)).astype(BF16)
            ki = (self.k_ref[0, :, sl] * jnp.exp(-b)).astype(BF16)
            att = jnp.where(self.mask_ref[self.n_lvl] > 0.0, _dot_nt(self.stack_heads(qe), ki), 0.0)
            self.att_ref[grp] = att.astype(BF16)

    def scores_by_level(self):
        chunk, n_lvl = self.chunk, self.n_lvl
        rows = self.heads_per_group * chunk
        sums = self.partial_sums(slice(2 * chunk, (n_lvl + 2) * chunk))
        eye = ((lax.broadcasted_iota(jnp.int32, (rows, chunk), 0) & (chunk - 1))
               == lax.broadcasted_iota(jnp.int32, (rows, chunk), 1)).astype(F32)
        for grp, sl in enumerate(self.groups):
            q = self.q_ref[0, :, sl]
            k = self.k_ref[0, :, sl]
            att = jnp.zeros((rows, chunk), F32)
            for lvl in range(n_lvl):
                w = jnp.exp(sums[lvl * chunk:(lvl + 1) * chunk, sl])
                att = att + _dot_nt(self.stack_heads((q * w).astype(BF16)), (k * w).astype(BF16)) * self.mask_ref[lvl]
            if not self.reverse:
                att = att + _dot_nt(self.stack_heads(q.astype(BF16)), k.astype(BF16)) * eye
            self.att_ref[grp] = att.astype(BF16)

    def outputs_and_state(self):
        chunk = self.chunk
        for grp, sl in enumerate(self.groups):
            q = self.q_ref[0, :, sl]
            k = self.k_ref[0, :, sl]
            qe = (q * jnp.exp(self.b[:, sl])).astype(BF16)
            kd = (k * jnp.exp(self.rest[:, sl])).astype(BF16)
            decay = jnp.exp(self.b_last[:, sl])
            att = self.att_ref[grp]
            for hh in range(self.heads_per_group):
                hd = grp * self.heads_per_group + hh
                vs = slice(hd * GLA_DV, (hd + 1) * GLA_DV)
                v = self.v_ref[0, :, vs]
                st = self.st_ref[hd]
                o = _dot(att[hh * chunk:(hh + 1) * chunk], v) + _dot_nt(self.head_only(qe, hh), st.astype(BF16))
                self.o_ref[0, :, vs] = o
                self.st_ref[hd] = st * decay + _dot_tn(v, kd)


def _gla_kernel(qf, kf, vf, gf, qb, kb, vb, gb, cumf, maskf, cumb, maskb, of_ref, ob_ref, stf, stb, attf, attb,
                *, chunk):
    @pl.when(pl.program_id(1) == 0)
    def _():
        stf[...] = jnp.zeros_like(stf)
        stb[...] = jnp.zeros_like(stb)

    dirs = [_GlaDirection(qf, kf, vf, gf, cumf, maskf, stf, attf, of_ref, chunk=chunk, reverse=False),
            _GlaDirection(qb, kb, vb, gb, cumb, maskb, stb, attb, ob_ref, chunk=chunk, reverse=True)]
    ok = jnp.logical_and(dirs[0].single_reference_ok(), dirs[1].single_reference_ok())

    @pl.when(ok)
    def _():
        for d in dirs:
            d.scores_single_reference()

    @pl.when(jnp.logical_not(ok))
    def _():
        for d in dirs:
            d.scores_by_level()

    for d in dirs:
        d.outputs_and_state()


def _gla(gq, gk, gv, gf, gb, *, n_latent, chunk):
    n_batch, t, hp = gq.shape
    n_lat = n_latent // chunk
    n_ctx = (t - n_latent) // chunk
    n_steps = n_lat + n_ctx

    def fwd(b, s):
        return (b, jnp.where(s < n_ctx, n_lat + s, s - n_ctx), 0)

    def bwd(b, s):
        return (b, n_steps - 1 - s, 0)

    cumf, maskf = _gla_constants(chunk, False)
    cumb, maskb = _gla_constants(chunk, True)
    tile_heads = lambda m: jnp.asarray(np.tile(m, (1, LANES // GLA_DK, 1)))
    consts = [jnp.asarray(cumf, BF16), tile_heads(maskf), jnp.asarray(cumb, BF16), tile_heads(maskb)]
    full = lambda a: pl.BlockSpec(a.shape, lambda b, s: (0,) * a.ndim)
    blk = lambda w, im: pl.BlockSpec((1, chunk, w), im)
    gv_w = gv.shape[-1]
    return pl.pallas_call(
        functools.partial(_gla_kernel, chunk=chunk),
        grid=(n_batch, n_steps),
        in_specs=[blk(hp, fwd), blk(hp, fwd), blk(gv_w, fwd), blk(hp, fwd),
                  blk(hp, bwd), blk(hp, bwd), blk(gv_w, bwd), blk(hp, bwd)] + [full(a) for a in consts],
        out_specs=[blk(gv_w, fwd), blk(gv_w, bwd)],
        out_shape=[jax.ShapeDtypeStruct((n_batch, t, gv_w), F32)] * 2,
        scratch_shapes=[pltpu.VMEM((GLA_HEADS, GLA_DV, LANES), F32)] * 2
                       + [pltpu.VMEM((GLA_HEADS * GLA_DK // LANES, chunk * LANES // GLA_DK, chunk), BF16)] * 2,
        compiler_params=_cparams(("arbitrary", "arbitrary")),
        name="gla_bidirectional",
    )(gq, gk, gv, gf, gq, gk, gv, gb, *consts)


def _flash_kernel(q_ref, k_ref, v_ref, o_ref, m_ref, acc_ref, *, ck, dv):
    j = pl.program_id(3)
    n_sub = k_ref.shape[1] // ck

    @pl.when(j == 0)
    def _():
        m_ref[...] = jnp.full_like(m_ref, -jnp.inf)
        acc_ref[...] = jnp.zeros_like(acc_ref)

    q = q_ref[0]

    for c in range(n_sub):
        off = c * ck
        s = _dot_nt(q, k_ref[0, pl.ds(off, ck), :])
        m_prev = m_ref[...]
        m_new = jnp.maximum(m_prev, jnp.max(s, axis=-1, keepdims=True))
        alpha = jnp.exp2(m_prev - m_new)
        p = jnp.concatenate([jnp.exp2(s[:, u * LANES:(u + 1) * LANES] - m_new)
                             for u in range(ck // LANES)], axis=-1).astype(BF16)
        pv = _dot(p, v_ref[0, pl.ds(off, ck), :])
        acc_ref[...] = jnp.concatenate([alpha] * (acc_ref.shape[1] // LANES), axis=-1) * acc_ref[...] + pv
        m_ref[...] = m_new

    @pl.when(j == pl.num_programs(3) - 1)
    def _():
        acc = acc_ref[...]
        denom = jnp.sum(acc[:, dv:], axis=-1, keepdims=True)
        o_ref[0] = (acc[:, :dv] / denom).astype(o_ref.dtype)


def _flash(q, k, v, *, n_heads, dq, dv, tq, tk, ck, q_blk0, n_q, k_blk0, n_k):
    n_batch = q.shape[0]
    return pl.pallas_call(
        functools.partial(_flash_kernel, ck=ck, dv=dv),
        grid=(n_batch, n_heads, n_q, n_k),
        in_specs=[pl.BlockSpec((1, tq, dq), lambda b, h, i, j: (b, q_blk0 + i, h)),
                  pl.BlockSpec((1, tk, dq), lambda b, h, i, j: (b, k_blk0 + j, h)),
                  pl.BlockSpec((1, tk, 2 * dv), lambda b, h, i, j: (b, k_blk0 + j, h))],
        out_specs=pl.BlockSpec((1, tq, dv), lambda b, h, i, j: (b, i, h)),
        out_shape=jax.ShapeDtypeStruct((n_batch, n_q * tq, n_heads * dv), BF16),
        scratch_shapes=[pltpu.VMEM((tq, LANES), F32), pltpu.VMEM((tq, 2 * dv), F32)],
        compiler_params=_cparams(("arbitrary",) * 4),
        name="flash_attention",
    )(q, k, v)


def _qkv_kernel(x_ref, shl, shc, scl, scc, w_ref, o_ref, *, tm, n_latent, q_width, q_scale):
    i = pl.program_id(1)
    is_ctx = _is_ctx_rows(i, tm, n_latent)
    h = _modulate(x_ref[0], is_ctx, shl, shc, scl, scc).astype(BF16)
    p = _dot(h, w_ref[...])
    o_ref[0, :, :q_width] = (p[:, :q_width] * q_scale).astype(BF16)
    o_ref[0, :, q_width:] = p[:, q_width:].astype(BF16)


def _qkv_proj(x, mods, layer, ks, w, *, n_latent, tm):
    n_batch, t, d = x.shape
    mod_specs = [spec for k in ks for spec in _mod_specs(n_batch, d, layer, k)]
    n_out = w.shape[1]
    return pl.pallas_call(
        functools.partial(_qkv_kernel, tm=tm, n_latent=n_latent, q_width=n_out // 3,
                          q_scale=NA_HEAD_DIM ** -0.5 * LOG2_E),
        grid=(n_batch, t // tm),
        in_specs=[pl.BlockSpec((1, tm, d), lambda b, i: (b, i, 0))] + mod_specs
                 + [pl.BlockSpec(w.shape, lambda b, i: (0, 0))],
        out_specs=pl.BlockSpec((1, tm, n_out), lambda b, i: (b, i, 0)),
        out_shape=jax.ShapeDtypeStruct((n_batch, t, n_out), BF16),
        compiler_params=_cparams(("arbitrary", "arbitrary")),
        name="qkv_project",
    )(x, *([mods] * 4), w)


def _pair_softmax_av(q, lane_lo, scores_fn, av_fn):
    outs = []
    for half in range(2):
        keep = lane_lo if half == 0 else jnp.logical_not(lane_lo)
        qh = jnp.where(keep, q, jnp.zeros_like(q))
        s_list = scores_fn(qh, half)
        m = s_list[0].max(axis=-1, keepdims=True)
        for s in s_list[1:]:
            m = jnp.maximum(m, s.max(axis=-1, keepdims=True))
        p_list = [jnp.exp2(s - m) for s in s_list]
        denom = p_list[0].sum(axis=-1, keepdims=True)
        for p in p_list[1:]:
            denom = denom + p.sum(axis=-1, keepdims=True)
        outs.append(av_fn([p.astype(BF16) for p in p_list]) / denom)
    return jnp.where(lane_lo, outs[0], outs[1])


def _na_block(q, k_win, v_win, k_ctx, v_ctx, bias_ref, kind, with_ones):
    n_keyrows = k_win.shape[0] // GRID_W
    lane_lo = lax.broadcasted_iota(jnp.int32, (1, LANES), 1) < NA_HEAD_DIM
    pieces = [[None] * len(NA_COL_GROUPS) for _ in range(NA_ROWS)]
    row0 = 0
    for g, (c0, gw, kc0) in enumerate(NA_COL_GROUPS):
        mg = NA_ROWS * gw
        qg = jnp.concatenate([q[i * GRID_W + c0:i * GRID_W + c0 + gw] for i in range(NA_ROWS)], axis=0)
        qs = jnp.concatenate([jnp.where(lane_lo, qg, 0.0), jnp.where(lane_lo, 0.0, qg)], axis=0).astype(BF16)
        kg = jnp.concatenate([k_win[t * GRID_W + kc0:t * GRID_W + kc0 + NA_GROUP_KEYS] for t in range(n_keyrows)], axis=0)
        vg = jnp.concatenate([v_win[t * GRID_W + kc0:t * GRID_W + kc0 + NA_GROUP_KEYS] for t in range(n_keyrows)], axis=0)
        s_w = _dot_nt(qs, kg) + bias_ref[kind, 0, pl.ds(row0, 2 * mg), :]
        s_c = _dot_nt(qs, k_ctx)
        m = jnp.maximum(jnp.max(s_w, axis=-1, keepdims=True), jnp.max(s_c, axis=-1, keepdims=True))
        pv = _dot(jnp.exp2(s_w - m).astype(BF16), with_ones(vg)) + _dot(jnp.exp2(s_c - m).astype(BF16), v_ctx)
        o = pv[:, :LANES] / jnp.sum(pv[:, LANES:], axis=-1, keepdims=True)
        og = jnp.where(lane_lo, o[:mg], o[mg:])
        for i in range(NA_ROWS):
            pieces[i][g] = og[i * gw:(i + 1) * gw]
        row0 += 2 * mg
    return jnp.concatenate([pc for row in pieces for pc in row], axis=0)


def _na_kernel(q_ref, *refs, n_blocks):
    kv_refs, (kc_ref, vc_ref, bias_ref, o_ref) = refs[:-4], refs[-4:]
    ones_col = (lax.broadcasted_iota(jnp.int32, (1, LANES), 1) == 0).astype(BF16)

    def with_ones(v):
        return jnp.concatenate([v, jnp.broadcast_to(ones_col, v.shape)], axis=-1)

    k_ctx = kc_ref[0]
    v_ctx = with_ones(vc_ref[0])
    tq = NA_ROWS * GRID_W
    for sub in range(NA_SUB):
        blk = pl.program_id(2) * NA_SUB + sub
        kind = jnp.where(blk == 0, 0, jnp.where(blk == n_blocks - 1, 2, 1))
        k_win = jnp.concatenate([r[0] for r in kv_refs[8 * sub:8 * sub + 4]], axis=0)
        v_win = jnp.concatenate([r[0] for r in kv_refs[8 * sub + 4:8 * sub + 8]], axis=0)
        q = q_ref[0, sub * tq:(sub + 1) * tq, :].astype(F32)
        out = _na_block(q, k_win, v_win, k_ctx, v_ctx, bias_ref, kind, with_ones)
        o_ref[0, sub * tq:(sub + 1) * tq, :] = out.astype(o_ref.dtype)


def _na_bias_tables(rpb, rows):
    n_keyrows = NA_ROWS + 2 * NA_UNIT
    col = np.arange(GRID_W)
    col_start = np.clip(col - NA_KW // 2, 0, GRID_W - NA_KW)
    kc = np.arange(GRID_W)
    col_ok = (kc[None, :] >= col_start[:, None]) & (kc[None, :] < col_start[:, None] + NA_KW)
    rel_col = np.clip(kc[None, :] - col[:, None] + (NA_KW - 1), 0, 2 * NA_KW - 2)
    col_sel = ((rel_col[:, :, None] == np.arange(2 * NA_KW - 1)) & col_ok[:, :, None]).astype(np.float32)
    n_heads = rpb.shape[0]
    by_col = jnp.einsum('hab,ckb->hack', rpb, jnp.asarray(col_sel), precision=lax.Precision.HIGHEST)
    by_col = jnp.where(jnp.asarray(col_ok)[None, None], by_col * LOG2_E, MASK_NEG)
    by_col = jnp.pad(by_col, ((0, 0), (NA_ROWS, n_keyrows - NA_ROWS), (0, 0), (0, 0)), constant_values=MASK_NEG)
    n_blocks = rows // NA_ROWS
    offs, row_mask = [], []
    for blk in (0, min(1, n_blocks - 1), n_blocks - 1):
        r = blk * NA_ROWS + np.arange(NA_ROWS)
        kh = min(NA_KH, rows)
        r_start = np.clip(r - kh // 2, 0, rows - kh)
        key_row0 = int(np.clip(blk * NA_ROWS - NA_UNIT, 0, rows - n_keyrows))
        kr = key_row0 + np.arange(n_keyrows)
        row_ok = (kr[None, :] >= r_start[:, None]) & (kr[None, :] < r_start[:, None] + kh)
        offs.append(key_row0 - blk * NA_ROWS + NA_KH - 1)
        row_mask.append(np.repeat(np.where(row_ok, 0.0, MASK_NEG), NA_GROUP_KEYS, axis=1))
    row_mask = jnp.asarray(np.stack(row_mask), F32)
    width = n_keyrows * NA_GROUP_KEYS
    parts = []
    for c0, gw, kc0 in NA_COL_GROUPS:
        strip = by_col[:, :, c0:c0 + gw, kc0:kc0 + NA_GROUP_KEYS]
        strip = strip.transpose(0, 2, 1, 3).reshape(n_heads, gw, -1)
        tiles = jnp.stack([jnp.stack([strip[:, :, (NA_ROWS + off - i) * NA_GROUP_KEYS:][:, :, :width]
                                      for i in range(NA_ROWS)], axis=1) for off in offs], axis=0)
        tiles = tiles + row_mask[:, None, :, None, :]
        parts.append(tiles.reshape(3, n_heads // 2, 2 * NA_ROWS * gw, width))
    return jnp.concatenate(parts, axis=2)


def _na_attention(qkv, bias, *, n_latent):
    n_batch, t, _ = qkv.shape
    rows = n_latent // GRID_W
    n_blocks = rows // NA_ROWS
    n_pairs = NA_HEADS // 2
    tq = NA_ROWS * GRID_W
    tu = NA_UNIT * GRID_W
    n_units = rows // NA_UNIT
    ctx_blk = n_latent // tu

    def slab(col0, sub, u):
        def im(p, b, rb):
            u0 = jnp.clip((rb * NA_SUB + sub) * (NA_ROWS // NA_UNIT) - 1, 0, n_units - 4)
            return (b, u0 + u, col0 + p)
        return pl.BlockSpec((1, tu, LANES), im)

    kv_specs = [slab(col0, sub, u) for sub in range(NA_SUB) for col0 in (n_pairs, 2 * n_pairs) for u in range(4)]
    return pl.pallas_call(
        functools.partial(_na_kernel, n_blocks=n_blocks),
        grid=(n_pairs, n_batch, n_blocks // NA_SUB),
        in_specs=[pl.BlockSpec((1, NA_SUB * tq, LANES), lambda p, b, rb: (b, rb, p))] + kv_specs
                 + [pl.BlockSpec((1, tu, LANES), lambda p, b, rb: (b, ctx_blk, n_pairs + p)),
                    pl.BlockSpec((1, tu, LANES), lambda p, b, rb: (b, ctx_blk, 2 * n_pairs + p)),
                    pl.BlockSpec((3, 1) + bias.shape[2:], lambda p, b, rb: (0, p, 0, 0))],
        out_specs=pl.BlockSpec((1, NA_SUB * tq, LANES), lambda p, b, rb: (b, rb, p)),
        out_shape=jax.ShapeDtypeStruct((n_batch, n_latent, NA_HEADS * NA_HEAD_DIM), BF16),
        compiler_params=_cparams(("arbitrary",) * 3),
        name="neighbourhood_attention",
    )(qkv, *([qkv] * (8 * NA_SUB + 2)), bias)


def _ctx_pair_kernel(q_ref, k_ref, v_ref, o_ref):
    k = k_ref[0]
    v = v_ref[0]
    lane_lo = lax.broadcasted_iota(jnp.int32, (1, LANES), 1) < NA_HEAD_DIM
    o_ref[0] = _pair_softmax_av(q_ref[0], lane_lo, lambda qh, half: [_dot_nt(qh, k)],
                                lambda ps: _dot(ps[0], v)).astype(o_ref.dtype)


def _ctx_pair_attention(qkv, *, n_latent):
    n_batch, t, _ = qkv.shape
    n_ctx = t - n_latent
    blk = n_latent // n_ctx
    n_pairs = NA_HEADS // 2
    spec = lambda off: pl.BlockSpec((1, n_ctx, LANES), lambda b, p: (b, blk, off + p))
    return pl.pallas_call(
        _ctx_pair_kernel,
        grid=(n_batch, n_pairs),
        in_specs=[spec(0), spec(n_pairs), spec(2 * n_pairs)],
        out_specs=pl.BlockSpec((1, n_ctx, LANES), lambda b, p: (b, 0, p)),
        out_shape=jax.ShapeDtypeStruct((n_batch, n_ctx, NA_HEADS * NA_HEAD_DIM), BF16),
        compiler_params=_cparams(("arbitrary", "arbitrary")),
        name="context_pair_attention",
    )(qkv, qkv, qkv)


def _even_weights(w_in, wg2_f, bg_f, wg2_b, bg_b, q_norm_g, kv_norm_g, w_uq, w_ukv):
    sizes = (GLA_HEADS * GLA_DK, GLA_HEADS * GLA_DK, GLA_HEADS * GLA_DV, GLA_HEADS * GLA_DV,
             GLA_RANK, GLA_RANK, MLA_D_CQ, MLA_D_CKV, MLA_D_ROPE)
    q_g, k_g, v_g, r_g, lr_f, lr_b, c_q, c_kv, k_r = jnp.split(w_in, np.cumsum(sizes)[:-1].tolist(), axis=-1)
    swap = np.arange(MLA_D_ROPE) ^ 1
    d = w_in.shape[0]
    zeros = lambda n: jnp.zeros((d, n), w_in.dtype)
    w1 = jnp.concatenate([
        q_g, k_g, v_g, r_g,
        lr_f, lr_b, zeros(LANES - 2 * GLA_RANK), c_q, c_kv,
        k_r, zeros(LANES - MLA_D_ROPE), k_r[:, swap], zeros(LANES - MLA_D_ROPE)], axis=-1).astype(BF16)
    hp = GLA_HEADS * GLA_DK
    wg = jnp.zeros((LANES, 2 * hp), F32)
    wg = wg.at[:GLA_RANK, :hp].set(wg2_f)
    wg = wg.at[GLA_RANK:2 * GLA_RANK, hp:].set(wg2_b)
    bg = jnp.concatenate([bg_f, bg_b])[None]
    uq = w_uq.reshape(MLA_D_CQ, MLA_HEADS, MLA_D_NOPE + MLA_D_ROPE)
    nope, rope = uq[..., :MLA_D_NOPE], uq[..., MLA_D_NOPE:]
    zr = jnp.zeros((MLA_D_CQ, MLA_HEADS, LANES - MLA_D_ROPE), w_uq.dtype)
    wq_main = jnp.concatenate([nope, rope, zr], axis=-1).reshape(MLA_D_CQ, -1).astype(BF16)
    wq_swap = jnp.concatenate([rope[..., swap], zr], axis=-1).reshape(MLA_D_CQ, -1).astype(BF16)
    ukv = w_ukv.reshape(MLA_D_CKV, MLA_HEADS, MLA_D_NOPE + MLA_D_V)
    wkv = jnp.concatenate([ukv[..., :MLA_D_NOPE].reshape(MLA_D_CKV, -1),
                           ukv[..., MLA_D_NOPE:].reshape(MLA_D_CKV, -1)], axis=-1).astype(BF16)
    return (w1, wg.astype(BF16), bg, q_norm_g[None], kv_norm_g[None], wq_main, wq_swap, wkv)


def _rope_tables(n_latent, n_ctx):
    t = jnp.arange(n_latent)
    row = (t // GRID_W).astype(F32)
    col = (t % GRID_W).astype(F32)
    n_freq = MLA_D_ROPE // 4
    inv = ROPE_BASE ** (-jnp.arange(n_freq, dtype=F32) / n_freq)
    ang = jnp.concatenate([row[:, None] * inv, col[:, None] * inv], axis=-1)
    cos = jnp.repeat(jnp.cos(ang), 2, axis=-1)
    sin = jnp.repeat(jnp.sin(ang), 2, axis=-1) * jnp.tile(jnp.array([-1.0, 1.0], F32), MLA_D_ROPE // 2)
    cos = jnp.concatenate([cos, jnp.ones((n_ctx, MLA_D_ROPE), F32)])
    sin = jnp.concatenate([sin, jnp.zeros((n_ctx, MLA_D_ROPE), F32)])
    pad = jnp.zeros((n_latent + n_ctx, LANES - MLA_D_ROPE), F32)
    return jnp.concatenate([cos, pad], axis=-1), jnp.concatenate([sin, pad], axis=-1)


def _row_tile(n_rows, target):
    best = 8
    for cand in range(8, target + 1, 8):
        if n_rows % cand == 0:
            best = cand
    return best


def kernel(x, c, ctx, c_ctx, ada_w, ada_b, ffn1_w_in, ffn1_w_out, ffn2_w_in, ffn2_w_out, even_w_in, gla_wg2_f, gla_bg_f, gla_wg2_b, gla_bg_b, gla_norm_g, mla_q_norm_g, mla_kv_norm_g, mla_w_uq, mla_w_ukv, even_w_out, na_w_in, na_rpb, na_w_out):
    n_batch, n_latent, d = x.shape
    n_ctx = ctx.shape[1]
    t_all = n_latent + n_ctx
    assert n_batch + 1 <= 8 and n_latent % (NA_SUB * NA_ROWS * GRID_W) == 0 and n_ctx == NA_UNIT * GRID_W
    assert n_latent % GLA_CHUNK == 0 and n_ctx % GLA_CHUNK == 0

    xa = jnp.concatenate([x, ctx], axis=1)
    cc = jnp.concatenate([c, c_ctx[None], jnp.zeros((7 - n_batch, d), F32)], axis=0)
    mods = _ada_modulation(cc, ada_w, ada_b)
    mods = mods[:, :n_batch + 1].reshape(DEPTH, n_batch + 1, 9, 1, d).transpose(0, 2, 1, 3, 4)

    tm_ffn = _row_tile(t_all, 768)
    tm_row = _row_tile(t_all, 384)
    rope_c, rope_s = _rope_tables(n_latent, n_ctx)
    tq = _row_tile(n_latent, 1024)
    ck = next(cand for cand in (768, 512, 256, 128) if t_all % cand == 0)
    ck_ctx = next(cand for cand in (768, 512, 256, 128) if n_ctx % cand == 0)

    for l in range(DEPTH):
        last = l == DEPTH - 1
        i = l // 2
        xa = _ffn(xa, mods, l, (0, 1, 2), ffn1_w_in[l].astype(BF16), ffn1_w_out[l].astype(BF16),
                  n_latent=n_latent, n_rows=t_all, tm=tm_ffn)
        n_rows = n_latent if last else t_all
        tm_out = _row_tile(n_rows, 512 if last else 384)
        if l % 2 == 0:
            wts = _even_weights(even_w_in[i], gla_wg2_f[i], gla_bg_f[i], gla_wg2_b[i], gla_bg_b[i],
                                mla_q_norm_g[i], mla_kv_norm_g[i], mla_w_uq[i], mla_w_ukv[i])
            gq, gk, gv, r, gf, gb, qm, km, vm = _even_proj(xa, mods, l, (3, 4), wts, rope_c, rope_s,
                                                           n_latent=n_latent, tm=tm_row)
            o_f, o_b = _gla(gq, gk, gv, gf, gb, n_latent=n_latent, chunk=GLA_CHUNK)
            mla_o = _flash(qm, km, vm, n_heads=MLA_HEADS, dq=2 * LANES, dv=MLA_D_V, tq=tq, tk=t_all, ck=ck,
                           q_blk0=0, n_q=n_latent // tq, k_blk0=0, n_k=1)
            if not last:
                mla_c = _flash(qm, km, vm, n_heads=MLA_HEADS, dq=2 * LANES, dv=MLA_D_V,
                               tq=n_ctx, tk=n_ctx, ck=ck_ctx, q_blk0=n_latent // n_ctx, n_q=1,
                               k_blk0=n_latent // n_ctx, n_k=1)
                mla_o = jnp.concatenate([mla_o, mla_c], axis=1)
            w_out = even_w_out[i].astype(BF16)
            gv_w = GLA_HEADS * GLA_DV
            mixer = dict(mixer="even", mixer_rows=(o_f, o_b, r, mla_o),
                         mixer_consts=(gla_norm_g[i][None], w_out[:gv_w], w_out[gv_w:]))
        else:
            qkv = _qkv_proj(xa, mods, l, (3, 4), na_w_in[i].astype(BF16), n_latent=n_latent, tm=tm_row)
            bias = _na_bias_tables(na_rpb[i], n_latent // GRID_W)
            att = _na_attention(qkv, bias, n_latent=n_latent)
            if not last:
                att = jnp.concatenate([att, _ctx_pair_attention(qkv, n_latent=n_latent)], axis=1)
            mixer = dict(mixer="odd", mixer_rows=(att,), mixer_consts=(na_w_out[i].astype(BF16),))
        xa = _ffn(xa, mods, l, (6, 7, 8), ffn2_w_in[l].astype(BF16), ffn2_w_out[l].astype(BF16),
                  n_latent=n_latent, n_rows=n_rows, tm=tm_out, mixer_gate=5, **mixer)
    return xa
```

```python
import functools

import numpy as np
import jax
import jax.numpy as jnp
from jax import lax
from jax.experimental import pallas as pl
from jax.experimental.pallas import tpu as pltpu

DEPTH = 4
GRID_W = 64
D_FF = 2816
GLA_HEADS = 4
GLA_DK = 64
GLA_DV = 128
GLA_RANK = 16
GLA_TAU = 16.0
MLA_HEADS = 4
MLA_D_NOPE = 128
MLA_D_ROPE = 64
MLA_D_V = 128
MLA_D_CQ = 384
MLA_D_CKV = 128
MLA_SCALE = (MLA_D_NOPE + MLA_D_ROPE) ** -0.5
LOG2_E = 1.4426950408889634
MLA_QSCALE = MLA_SCALE * LOG2_E
NA_HEADS = 16
NA_HEAD_DIM = 64
NA_KH = 8
NA_KW = 16
ROPE_BASE = 10000.0
EPS = 1e-6
ALPHA = (2 * DEPTH) ** 0.25

LANES = 128
VMEM_LIMIT = 56 * 1024 * 1024

ROWS_PLAIN = 768
ROWS_FUSED = 384
ROWS_FUSED_LATENT = 512
ROWS_QUERY = 1024
KEYS_SUB_BLOCK = (768, 512, 256, 128)
GLA_CHUNK = 256
NA_ROWS = 8
NA_SUB = 4
NA_UNIT = 4
MASK_NEG = -1e30
GLA_FAST_MAX_DECAY = 60.0
GLA_FAST_MAX_KEY = 1e10
NA_COL_GROUPS = ((0, 24, 0), (24, 16, 16), (40, 24, 32))
NA_GROUP_KEYS = 32

BF16 = jnp.bfloat16
F32 = jnp.float32


def _cparams(sem):
    return pltpu.CompilerParams(dimension_semantics=sem, vmem_limit_bytes=VMEM_LIMIT)


def _dot(a, b):
    return jnp.dot(a, b, preferred_element_type=F32)


def _dot_nt(a, b):
    return lax.dot_general(a, b, (((1,), (1,)), ((), ())), preferred_element_type=F32)


def _dot_tn(a, b):
    return lax.dot_general(a, b, (((0,), (0,)), ((), ())), preferred_element_type=F32)


def _silu(v):
    return v * (1.0 / (1.0 + jnp.exp(-v)))


def _row_select(is_ctx, lat_ref, ctx_ref):
    return jnp.where(is_ctx, ctx_ref[0], lat_ref[0])


def _is_ctx_rows(tile_idx, tm, n_latent):
    rows = tile_idx * tm + lax.broadcasted_iota(jnp.int32, (tm, 1), 0)
    return rows >= n_latent


def _modulate(x, is_ctx, sh_l, sh_c, sc_l, sc_c):
    shift = _row_select(is_ctx, sh_l, sh_c)
    scale = _row_select(is_ctx, sc_l, sc_c)
    return x * (1.0 + scale) + shift


def _post_norm(x, y, gate, coef):
    z = ALPHA * x + (coef * gate) * y
    mu = jnp.mean(z, axis=-1, keepdims=True)
    zc = z - mu
    var = jnp.mean(zc * zc, axis=-1, keepdims=True)
    return zc * lax.rsqrt(var + EPS)


def _mod_specs(n_batch, d, layer, k):
    blk = (None, None, 1, 1, d)
    return (pl.BlockSpec(blk, lambda b, i: (layer, k, b, 0, 0)),
            pl.BlockSpec(blk, lambda b, i: (layer, k, n_batch, 0, 0)))


def _ada_kernel(c_ref, w_ref, b_ref, o_ref):
    a = _silu(c_ref[...]).astype(BF16)
    o_ref[0] = _dot(a, w_ref[0].astype(BF16)) + b_ref[0]


def _ada_modulation(cc, ada_w, ada_b):
    depth, d, n9 = ada_w.shape
    tn = n9 // 8
    return pl.pallas_call(
        _ada_kernel,
        grid=(depth, n9 // tn),
        in_specs=[pl.BlockSpec((8, d), lambda l, j: (0, 0)),
                  pl.BlockSpec((1, d, tn), lambda l, j: (l, 0, j)),
                  pl.BlockSpec((1, 1, tn), lambda l, j: (l, 0, j))],
        out_specs=pl.BlockSpec((1, 8, tn), lambda l, j: (l, 0, j)),
        out_shape=jax.ShapeDtypeStruct((depth, 8, n9), F32),
        compiler_params=_cparams(("arbitrary", "arbitrary")),
        name="ada_modulation",
    )(cc, ada_w, ada_b.reshape(depth, 1, n9))


def _gla_gate_norm(o, r, norm_g):
    parts = []
    for hd in range(GLA_HEADS):
        sl = slice(hd * GLA_DV, (hd + 1) * GLA_DV)
        oh = o[:, sl]
        y = oh * lax.rsqrt(jnp.mean(oh * oh, axis=-1, keepdims=True) + EPS) * norm_g
        parts.append((y * _silu(r[:, sl])).astype(BF16))
    return jnp.concatenate(parts, axis=-1)


def _ffn_kernel(*refs, tm, n_latent, mixer):
    x_ref, shl, shc, scl, scc, gl, gc, wi_ref, wo_ref = refs[:9]
    o_ref = refs[-1]
    i = pl.program_id(1)
    is_ctx = _is_ctx_rows(i, tm, n_latent)
    x = x_ref[0]
    if mixer == "even":
        of_ref, ob_ref, r_ref, ng_ref, mo_ref, wa_ref, wb_ref, g2l, g2c = refs[9:-1]
        a = _gla_gate_norm(of_ref[0] + ob_ref[0], r_ref[0], ng_ref[...])
        y_mix = _dot(a, wa_ref[...]) + _dot(mo_ref[0], wb_ref[...])
        x = _post_norm(x, y_mix, _row_select(is_ctx, g2l, g2c), 1.0)
    elif mixer == "odd":
        a_ref, wp_ref, g2l, g2c = refs[9:-1]
        x = _post_norm(x, _dot(a_ref[0], wp_ref[...]), _row_select(is_ctx, g2l, g2c), 1.0)
    h = _modulate(x, is_ctx, shl, shc, scl, scc).astype(BF16)
    d_ff = wo_ref.shape[0]
    gate = _dot(h, wi_ref[:, :d_ff])
    up = _dot(h, wi_ref[:, d_ff:])
    y = _dot((_silu(gate) * up).astype(BF16), wo_ref[...])
    o_ref[0] = _post_norm(x, y, _row_select(is_ctx, gl, gc), 0.5)


def _ffn(x, mods, layer, ks, w_in, w_out, *, n_latent, n_rows, tm, mixer=None, mixer_rows=(), mixer_consts=(),
         mixer_gate=None):
    n_batch, _, d = x.shape
    mod_specs = [spec for k in ks for spec in _mod_specs(n_batch, d, layer, k)]
    gate_specs = list(_mod_specs(n_batch, d, layer, mixer_gate)) if mixer else []
    resident = lambda w: pl.BlockSpec(w.shape, lambda b, i: (0,) * w.ndim, pipeline_mode=pl.Buffered(1))
    row = lambda a: pl.BlockSpec((1, tm, a.shape[-1]), lambda b, i: (b, i, 0))
    if mixer == "even":
        o_f, o_b, r, mla_o = mixer_rows
        norm_g, w_a, w_b = mixer_consts
        extra = [o_f, o_b, r, norm_g, mla_o, w_a, w_b, mods, mods]
        extra_specs = [row(o_f), row(o_b), row(r), resident(norm_g), row(mla_o), resident(w_a), resident(w_b)] + gate_specs
    elif mixer == "odd":
        (att,), (w_p,) = mixer_rows, mixer_consts
        extra = [att, w_p, mods, mods]
        extra_specs = [row(att), resident(w_p)] + gate_specs
    else:
        extra, extra_specs = [], []
    return pl.pallas_call(
        functools.partial(_ffn_kernel, tm=tm, n_latent=n_latent, mixer=mixer),
        grid=(n_batch, n_rows // tm),
        in_specs=[pl.BlockSpec((1, tm, d), lambda b, i: (b, i, 0))] + mod_specs
                 + [resident(w_in), resident(w_out)] + extra_specs,
        out_specs=pl.BlockSpec((1, tm, d), lambda b, i: (b, i, 0)),
        out_shape=jax.ShapeDtypeStruct((n_batch, n_rows, d), F32),
        compiler_params=_cparams(("arbitrary", "arbitrary")),
        name="ffn_postnorm" if mixer is None else "mixer_out_ffn_postnorm",
    )(x, *([mods] * 6), w_in, w_out, *extra)


def _even_proj_kernel(x_ref, shl, shc, scl, scc, w1_ref, wg_ref, bg_ref, qng_ref, kvng_ref,
                      wqm_ref, wqs_ref, wkv_ref, cb_ref, sb_ref,
                      gq_ref, gk_ref, gv_ref, r_ref, gf_ref, gb_ref, qm_ref, km_ref, vm_ref,
                      *, tm, n_latent):
    i = pl.program_id(1)
    is_ctx = _is_ctx_rows(i, tm, n_latent)
    h = _modulate(x_ref[0], is_ctx, shl, shc, scl, scc).astype(BF16)
    p = _dot(h, w1_ref[...])
    hp = GLA_HEADS * GLA_DK
    gv_w = GLA_HEADS * GLA_DV
    o0 = 0
    gq_ref[0] = p[:, o0:o0 + hp] * (GLA_DK ** -0.5)
    o0 += hp
    gk_ref[0] = p[:, o0:o0 + hp]
    o0 += hp
    gv_ref[0] = p[:, o0:o0 + gv_w].astype(BF16)
    o0 += gv_w
    r_ref[0] = p[:, o0:o0 + gv_w]
    o0 += gv_w
    lr = p[:, o0:o0 + LANES].astype(BF16)
    o0 += LANES
    cq = p[:, o0:o0 + MLA_D_CQ]
    o0 += MLA_D_CQ
    ckv = p[:, o0:o0 + MLA_D_CKV]
    o0 += MLA_D_CKV
    kr = p[:, o0:o0 + LANES]
    o0 += LANES
    krs = p[:, o0:o0 + LANES]

    z = _dot(lr, wg_ref[...]) + bg_ref[...]
    logg = (jnp.minimum(z, 0.0) - jnp.log1p(jnp.exp(-jnp.abs(z)))) / GLA_TAU
    gf_ref[0] = logg[:, :hp]
    gb_ref[0] = logg[:, hp:]

    cb = cb_ref[...]
    sb = sb_ref[...]
    cqn = (cq * lax.rsqrt(jnp.mean(cq * cq, axis=-1, keepdims=True) + EPS) * qng_ref[...]).astype(BF16)
    qmain = _dot(cqn, wqm_ref[...])
    qswap = _dot(cqn, wqs_ref[...])
    ckn = (ckv * lax.rsqrt(jnp.mean(ckv * ckv, axis=-1, keepdims=True) + EPS) * kvng_ref[...]).astype(BF16)
    kv = _dot(ckn, wkv_ref[...])
    k_rope = (kr * cb + krs * sb).astype(BF16)
    ones_col = (lax.broadcasted_iota(jnp.int32, (1, LANES), 1) == 0).astype(BF16)
    for hd in range(MLA_HEADS):
        b0 = 2 * LANES * hd
        qm_ref[0, :, b0:b0 + LANES] = (qmain[:, b0:b0 + LANES] * MLA_QSCALE).astype(BF16)
        q_rope = qmain[:, b0 + LANES:b0 + 2 * LANES] * cb + qswap[:, LANES * hd:LANES * (hd + 1)] * sb
        qm_ref[0, :, b0 + LANES:b0 + 2 * LANES] = (q_rope * MLA_QSCALE).astype(BF16)
        km_ref[0, :, b0:b0 + LANES] = kv[:, LANES * hd:LANES * (hd + 1)].astype(BF16)
        km_ref[0, :, b0 + LANES:b0 + 2 * LANES] = k_rope
        v0 = MLA_HEADS * MLA_D_NOPE + MLA_D_V * hd
        vm_ref[0, :, b0:b0 + LANES] = kv[:, v0:v0 + MLA_D_V].astype(BF16)
        vm_ref[0, :, b0 + LANES:b0 + 2 * LANES] = jnp.broadcast_to(ones_col, (tm, LANES))


def _even_proj(x, mods, layer, ks, wts, rope_c, rope_s, *, n_latent, tm):
    n_batch, t, d = x.shape
    mod_specs = [spec for k in ks for spec in _mod_specs(n_batch, d, layer, k)]
    full = lambda a: pl.BlockSpec(a.shape, lambda b, i: (0,) * a.ndim)
    row = lambda w: pl.BlockSpec((1, tm, w), lambda b, i: (b, i, 0))
    hp = GLA_HEADS * GLA_DK
    gv_w = GLA_HEADS * GLA_DV
    out_w = [(hp, F32), (hp, F32), (gv_w, BF16), (gv_w, F32), (hp, F32), (hp, F32),
             (MLA_HEADS * 2 * LANES, BF16), (MLA_HEADS * 2 * LANES, BF16), (MLA_HEADS * 2 * MLA_D_V, BF16)]
    return pl.pallas_call(
        functools.partial(_even_proj_kernel, tm=tm, n_latent=n_latent),
        grid=(n_batch, t // tm),
        in_specs=[row(d)] + mod_specs + [full(w) for w in wts]
                 + [pl.BlockSpec((tm, LANES), lambda b, i: (i, 0))] * 2,
        out_specs=[row(w) for w, _ in out_w],
        out_shape=[jax.ShapeDtypeStruct((n_batch, t, w), dt) for w, dt in out_w],
        compiler_params=_cparams(("arbitrary", "arbitrary")),
        name="even_project",
    )(x, *([mods] * 4), *wts, rope_c, rope_s)


def _gla_constants(chunk, reverse):
    n_lvl = int(np.log2(chunk))
    pos = np.arange(chunk)
    src = pos[None, :]
    tri = src <= pos[:, None]
    cum, mask = [], []
    for lvl in range(n_lvl):
        size = chunk >> lvl
        half = size // 2
        ref = ((pos // size) * size + half - 1)[:, None]
        upper = (pos % size) >= half
        a_up = (src > ref) & (src <= pos[:, None])
        a_lo = (src > pos[:, None]) & (src <= ref)
        cum.append(np.where(upper[:, None], a_up, a_lo))
        same = (pos[:, None] // size) == (pos[None, :] // size)
        mask.append(same & upper[:, None] & ~upper[None, :])
    mask.append(np.any(mask, axis=0) | (np.eye(chunk, dtype=bool) & (not reverse)))
    cum = np.stack([tri, ~tri] + cum).astype(np.float32)
    mask = np.stack(mask).astype(np.float32)
    if reverse:
        cum = cum[:, ::-1, ::-1]
        mask = mask[:, ::-1, ::-1]
    return np.ascontiguousarray(cum).reshape(-1, chunk), np.ascontiguousarray(mask)


class _GlaDirection:
    def __init__(self, q_ref, k_ref, v_ref, g_ref, cum_ref, mask_ref, st_ref, att_ref, o_ref, *, chunk, reverse):
        self.q_ref, self.k_ref, self.v_ref = q_ref, k_ref, v_ref
        self.cum_ref, self.mask_ref, self.st_ref, self.att_ref, self.o_ref = cum_ref, mask_ref, st_ref, att_ref, o_ref
        self.chunk, self.reverse = chunk, reverse
        self.n_lvl = mask_ref.shape[0] - 1
        self.heads_per_group = LANES // GLA_DK
        self.groups = [slice(grp * LANES, (grp + 1) * LANES) for grp in range(GLA_HEADS * GLA_DK // LANES)]
        g = g_ref[0]
        self.g_hi = g.astype(BF16)
        self.g_lo = (g - self.g_hi.astype(F32)).astype(BF16)
        base = self.partial_sums(slice(0, 2 * chunk))
        self.b = base[:chunk]
        self.rest = base[chunk:]
        last = 0 if reverse else chunk - 1
        self.b_last = self.b[last:last + 1, :]
        self.lane_lo = lax.broadcasted_iota(jnp.int32, (1, LANES), 1) < GLA_DK

    def partial_sums(self, rows):
        return _dot(self.cum_ref[rows, :], self.g_hi) + _dot(self.cum_ref[rows, :], self.g_lo)

    def head_only(self, x, hh):
        return jnp.where(self.lane_lo if hh == 0 else jnp.logical_not(self.lane_lo), x, jnp.zeros_like(x))

    def stack_heads(self, x):
        return jnp.concatenate([self.head_only(x, hh) for hh in range(self.heads_per_group)], axis=0)

    def single_reference_ok(self):
        return jnp.logical_and(jnp.max(-self.b_last) <= GLA_FAST_MAX_DECAY,
                               jnp.max(jnp.abs(self.k_ref[0])) <= GLA_FAST_MAX_KEY)

    def scores_single_reference(self):
        for grp, sl in enumerate(self.groups):
            b = self.b[:, sl]
            qe = (self.q_ref[0, :, sl] * jnp.exp(b)).astype(BF16)
            ki = (self.k_ref[0, :, sl] * jnp.exp(-b)).astype(BF16)
            att = jnp.where(self.mask_ref[self.n_lvl] > 0.0, _dot_nt(self.stack_heads(qe), ki), 0.0)
            self.att_ref[grp] = att.astype(BF16)

    def scores_by_level(self):
        chunk, n_lvl = self.chunk, self.n_lvl
        rows = self.heads_per_group * chunk
        sums = self.partial_sums(slice(2 * chunk, (n_lvl + 2) * chunk))
        eye = ((lax.broadcasted_iota(jnp.int32, (rows, chunk), 0) & (chunk - 1))
               == lax.broadcasted_iota(jnp.int32, (rows, chunk), 1)).astype(F32)
        for grp, sl in enumerate(self.groups):
            q = self.q_ref[0, :, sl]
            k = self.k_ref[0, :, sl]
            att = jnp.zeros((rows, chunk), F32)
            for lvl in range(n_lvl):
                w = jnp.exp(sums[lvl * chunk:(lvl + 1) * chunk, sl])
                att = att + _dot_nt(self.stack_heads((q * w).astype(BF16)), (k * w).astype(BF16)) * self.mask_ref[lvl]
            if not self.reverse:
                att = att + _dot_nt(self.stack_heads(q.astype(BF16)), k.astype(BF16)) * eye
            self.att_ref[grp] = att.astype(BF16)

    def outputs_and_state(self):
        chunk = self.chunk
        for grp, sl in enumerate(self.groups):
            q = self.q_ref[0, :, sl]
            k = self.k_ref[0, :, sl]
            qe = (q * jnp.exp(self.b[:, sl])).astype(BF16)
            kd = (k * jnp.exp(self.rest[:, sl])).astype(BF16)
            decay = jnp.exp(self.b_last[:, sl])
            att = self.att_ref[grp]
            for hh in range(self.heads_per_group):
                hd = grp * self.heads_per_group + hh
                vs = slice(hd * GLA_DV, (hd + 1) * GLA_DV)
                v = self.v_ref[0, :, vs]
                st = self.st_ref[hd]
                o = _dot(att[hh * chunk:(hh + 1) * chunk], v) + _dot_nt(self.head_only(qe, hh), st.astype(BF16))
                self.o_ref[0, :, vs] = o
                self.st_ref[hd] = st * decay + _dot_tn(v, kd)


def _gla_kernel(qf, kf, vf, gf, qb, kb, vb, gb, cumf, maskf, cumb, maskb, of_ref, ob_ref, stf, stb, attf, attb,
                *, chunk):
    @pl.when(pl.program_id(1) == 0)
    def _():
        stf[...] = jnp.zeros_like(stf)
        stb[...] = jnp.zeros_like(stb)

    dirs = [_GlaDirection(qf, kf, vf, gf, cumf, maskf, stf, attf, of_ref, chunk=chunk, reverse=False),
            _GlaDirection(qb, kb, vb, gb, cumb, maskb, stb, attb, ob_ref, chunk=chunk, reverse=True)]
    ok = jnp.logical_and(dirs[0].single_reference_ok(), dirs[1].single_reference_ok())

    @pl.when(ok)
    def _():
        for d in dirs:
            d.scores_single_reference()

    @pl.when(jnp.logical_not(ok))
    def _():
        for d in dirs:
            d.scores_by_level()

    for d in dirs:
        d.outputs_and_state()


def _gla(gq, gk, gv, gf, gb, *, n_latent, chunk):
    n_batch, t, hp = gq.shape
    n_lat = n_latent // chunk
    n_ctx = (t - n_latent) // chunk
    n_steps = n_lat + n_ctx

    def fwd(b, s):
        return (b, jnp.where(s < n_ctx, n_lat + s, s - n_ctx), 0)

    def bwd(b, s):
        return (b, n_steps - 1 - s, 0)

    cumf, maskf = _gla_constants(chunk, False)
    cumb, maskb = _gla_constants(chunk, True)
    tile_heads = lambda m: jnp.asarray(np.tile(m, (1, LANES // GLA_DK, 1)))
    consts = [jnp.asarray(cumf, BF16), tile_heads(maskf), jnp.asarray(cumb, BF16), tile_heads(maskb)]
    full = lambda a: pl.BlockSpec(a.shape, lambda b, s: (0,) * a.ndim)
    blk = lambda w, im: pl.BlockSpec((1, chunk, w), im)
    gv_w = gv.shape[-1]
    return pl.pallas_call(
        functools.partial(_gla_kernel, chunk=chunk),
        grid=(n_batch, n_steps),
        in_specs=[blk(hp, fwd), blk(hp, fwd), blk(gv_w, fwd), blk(hp, fwd),
                  blk(hp, bwd), blk(hp, bwd), blk(gv_w, bwd), blk(hp, bwd)] + [full(a) for a in consts],
        out_specs=[blk(gv_w, fwd), blk(gv_w, bwd)],
        out_shape=[jax.ShapeDtypeStruct((n_batch, t, gv_w), F32)] * 2,
        scratch_shapes=[pltpu.VMEM((GLA_HEADS, GLA_DV, LANES), F32)] * 2
                       + [pltpu.VMEM((GLA_HEADS * GLA_DK // LANES, chunk * LANES // GLA_DK, chunk), BF16)] * 2,
        compiler_params=_cparams(("arbitrary", "arbitrary")),
        name="gla_bidirectional",
    )(gq, gk, gv, gf, gq, gk, gv, gb, *consts)


def _flash_kernel(q_ref, k_ref, v_ref, o_ref, m_ref, acc_ref, *, ck, dv):
    j = pl.program_id(3)
    n_sub = k_ref.shape[1] // ck

    @pl.when(j == 0)
    def _():
        m_ref[...] = jnp.full_like(m_ref, -jnp.inf)
        acc_ref[...] = jnp.zeros_like(acc_ref)

    q = q_ref[0]

    for c in range(n_sub):
        off = c * ck
        s = _dot_nt(q, k_ref[0, pl.ds(off, ck), :])
        m_prev = m_ref[...]
        m_new = jnp.maximum(m_prev, jnp.max(s, axis=-1, keepdims=True))
        alpha = jnp.exp2(m_prev - m_new)
        p = jnp.concatenate([jnp.exp2(s[:, u * LANES:(u + 1) * LANES] - m_new)
                             for u in range(ck // LANES)], axis=-1).astype(BF16)
        pv = _dot(p, v_ref[0, pl.ds(off, ck), :])
        acc_ref[...] = jnp.concatenate([alpha] * (acc_ref.shape[1] // LANES), axis=-1) * acc_ref[...] + pv
        m_ref[...] = m_new

    @pl.when(j == pl.num_programs(3) - 1)
    def _():
        acc = acc_ref[...]
        denom = jnp.sum(acc[:, dv:], axis=-1, keepdims=True)
        o_ref[0] = (acc[:, :dv] / denom).astype(o_ref.dtype)


def _flash(q, k, v, *, n_heads, dq, dv, tq, tk, ck, q_blk0, n_q, k_blk0, n_k):
    n_batch = q.shape[0]
    return pl.pallas_call(
        functools.partial(_flash_kernel, ck=ck, dv=dv),
        grid=(n_batch, n_heads, n_q, n_k),
        in_specs=[pl.BlockSpec((1, tq, dq), lambda b, h, i, j: (b, q_blk0 + i, h)),
                  pl.BlockSpec((1, tk, dq), lambda b, h, i, j: (b, k_blk0 + j, h)),
                  pl.BlockSpec((1, tk, 2 * dv), lambda b, h, i, j: (b, k_blk0 + j, h))],
        out_specs=pl.BlockSpec((1, tq, dv), lambda b, h, i, j: (b, i, h)),
        out_shape=jax.ShapeDtypeStruct((n_batch, n_q * tq, n_heads * dv), BF16),
        scratch_shapes=[pltpu.VMEM((tq, LANES), F32), pltpu.VMEM((tq, 2 * dv), F32)],
        compiler_params=_cparams(("arbitrary",) * 4),
        name="flash_attention",
    )(q, k, v)


def _qkv_kernel(x_ref, shl, shc, scl, scc, w_ref, o_ref, *, tm, n_latent, q_width, q_scale):
    i = pl.program_id(1)
    is_ctx = _is_ctx_rows(i, tm, n_latent)
    h = _modulate(x_ref[0], is_ctx, shl, shc, scl, scc).astype(BF16)
    p = _dot(h, w_ref[...])
    o_ref[0, :, :q_width] = (p[:, :q_width] * q_scale).astype(BF16)
    o_ref[0, :, q_width:] = p[:, q_width:].astype(BF16)


def _qkv_proj(x, mods, layer, ks, w, *, n_latent, tm):
    n_batch, t, d = x.shape
    mod_specs = [spec for k in ks for spec in _mod_specs(n_batch, d, layer, k)]
    n_out = w.shape[1]
    return pl.pallas_call(
        functools.partial(_qkv_kernel, tm=tm, n_latent=n_latent, q_width=n_out // 3,
                          q_scale=NA_HEAD_DIM ** -0.5 * LOG2_E),
        grid=(n_batch, t // tm),
        in_specs=[pl.BlockSpec((1, tm, d), lambda b, i: (b, i, 0))] + mod_specs
                 + [pl.BlockSpec(w.shape, lambda b, i: (0, 0))],
        out_specs=pl.BlockSpec((1, tm, n_out), lambda b, i: (b, i, 0)),
        out_shape=jax.ShapeDtypeStruct((n_batch, t, n_out), BF16),
        compiler_params=_cparams(("arbitrary", "arbitrary")),
        name="qkv_project",
    )(x, *([mods] * 4), w)


def _pair_softmax_av(q, lane_lo, scores_fn, av_fn):
    outs = []
    for half in range(2):
        keep = lane_lo if half == 0 else jnp.logical_not(lane_lo)
        qh = jnp.where(keep, q, jnp.zeros_like(q))
        s_list = scores_fn(qh, half)
        m = s_list[0].max(axis=-1, keepdims=True)
        for s in s_list[1:]:
            m = jnp.maximum(m, s.max(axis=-1, keepdims=True))
        p_list = [jnp.exp2(s - m) for s in s_list]
        denom = p_list[0].sum(axis=-1, keepdims=True)
        for p in p_list[1:]:
            denom = denom + p.sum(axis=-1, keepdims=True)
        outs.append(av_fn([p.astype(BF16) for p in p_list]) / denom)
    return jnp.where(lane_lo, outs[0], outs[1])


def _na_block(q, k_win, v_win, k_ctx, v_ctx, bias_ref, kind, with_ones):
    n_keyrows = k_win.shape[0] // GRID_W
    lane_lo = lax.broadcasted_iota(jnp.int32, (1, LANES), 1) < NA_HEAD_DIM
    pieces = [[None] * len(NA_COL_GROUPS) for _ in range(NA_ROWS)]
    row0 = 0
    for g, (c0, gw, kc0) in enumerate(NA_COL_GROUPS):
        mg = NA_ROWS * gw
        qg = jnp.concatenate([q[i * GRID_W + c0:i * GRID_W + c0 + gw] for i in range(NA_ROWS)], axis=0)
        qs = jnp.concatenate([jnp.where(lane_lo, qg, 0.0), jnp.where(lane_lo, 0.0, qg)], axis=0).astype(BF16)
        kg = jnp.concatenate([k_win[t * GRID_W + kc0:t * GRID_W + kc0 + NA_GROUP_KEYS] for t in range(n_keyrows)], axis=0)
        vg = jnp.concatenate([v_win[t * GRID_W + kc0:t * GRID_W + kc0 + NA_GROUP_KEYS] for t in range(n_keyrows)], axis=0)
        s_w = _dot_nt(qs, kg) + bias_ref[kind, 0, pl.ds(row0, 2 * mg), :]
        s_c = _dot_nt(qs, k_ctx)
        m = jnp.maximum(jnp.max(s_w, axis=-1, keepdims=True), jnp.max(s_c, axis=-1, keepdims=True))
        pv = _dot(jnp.exp2(s_w - m).astype(BF16), with_ones(vg)) + _dot(jnp.exp2(s_c - m).astype(BF16), v_ctx)
        o = pv[:, :LANES] / jnp.sum(pv[:, LANES:], axis=-1, keepdims=True)
        og = jnp.where(lane_lo, o[:mg], o[mg:])
        for i in range(NA_ROWS):
            pieces[i][g] = og[i * gw:(i + 1) * gw]
        row0 += 2 * mg
    return jnp.concatenate([pc for row in pieces for pc in row], axis=0)


def _na_kernel(q_ref, *refs, n_blocks):
    kv_refs, (kc_ref, vc_ref, bias_ref, o_ref) = refs[:-4], refs[-4:]
    ones_col = (lax.broadcasted_iota(jnp.int32, (1, LANES), 1) == 0).astype(BF16)

    def with_ones(v):
        return jnp.concatenate([v, jnp.broadcast_to(ones_col, v.shape)], axis=-1)

    k_ctx = kc_ref[0]
    v_ctx = with_ones(vc_ref[0])
    tq = NA_ROWS * GRID_W
    for sub in range(NA_SUB):
        blk = pl.program_id(2) * NA_SUB + sub
        kind = jnp.where(blk == 0, 0, jnp.where(blk == n_blocks - 1, 2, 1))
        k_win = jnp.concatenate([r[0] for r in kv_refs[8 * sub:8 * sub + 4]], axis=0)
        v_win = jnp.concatenate([r[0] for r in kv_refs[8 * sub + 4:8 * sub + 8]], axis=0)
        q = q_ref[0, sub * tq:(sub + 1) * tq, :].astype(F32)
        out = _na_block(q, k_win, v_win, k_ctx, v_ctx, bias_ref, kind, with_ones)
        o_ref[0, sub * tq:(sub + 1) * tq, :] = out.astype(o_ref.dtype)


def _na_bias_tables(rpb, rows):
    n_keyrows = NA_ROWS + 2 * NA_UNIT
    col = np.arange(GRID_W)
    col_start = np.clip(col - NA_KW // 2, 0, GRID_W - NA_KW)
    kc = np.arange(GRID_W)
    col_ok = (kc[None, :] >= col_start[:, None]) & (kc[None, :] < col_start[:, None] + NA_KW)
    rel_col = np.clip(kc[None, :] - col[:, None] + (NA_KW - 1), 0, 2 * NA_KW - 2)
    col_sel = ((rel_col[:, :, None] == np.arange(2 * NA_KW - 1)) & col_ok[:, :, None]).astype(np.float32)
    n_heads = rpb.shape[0]
    by_col = jnp.einsum('hab,ckb->hack', rpb, jnp.asarray(col_sel), precision=lax.Precision.HIGHEST)
    by_col = jnp.where(jnp.asarray(col_ok)[None, None], by_col * LOG2_E, MASK_NEG)
    by_col = jnp.pad(by_col, ((0, 0), (NA_ROWS, n_keyrows - NA_ROWS), (0, 0), (0, 0)), constant_values=MASK_NEG)
    n_blocks = rows // NA_ROWS
    offs, row_mask = [], []
    for blk in (0, min(1, n_blocks - 1), n_blocks - 1):
        r = blk * NA_ROWS + np.arange(NA_ROWS)
        kh = min(NA_KH, rows)
        r_start = np.clip(r - kh // 2, 0, rows - kh)
        key_row0 = int(np.clip(blk * NA_ROWS - NA_UNIT, 0, rows - n_keyrows))
        kr = key_row0 + np.arange(n_keyrows)
        row_ok = (kr[None, :] >= r_start[:, None]) & (kr[None, :] < r_start[:, None] + kh)
        offs.append(key_row0 - blk * NA_ROWS + NA_KH - 1)
        row_mask.append(np.repeat(np.where(row_ok, 0.0, MASK_NEG), NA_GROUP_KEYS, axis=1))
    row_mask = jnp.asarray(np.stack(row_mask), F32)
    width = n_keyrows * NA_GROUP_KEYS
    parts = []
    for c0, gw, kc0 in NA_COL_GROUPS:
        strip = by_col[:, :, c0:c0 + gw, kc0:kc0 + NA_GROUP_KEYS]
        strip = strip.transpose(0, 2, 1, 3).reshape(n_heads, gw, -1)
        tiles = jnp.stack([jnp.stack([strip[:, :, (NA_ROWS + off - i) * NA_GROUP_KEYS:][:, :, :width]
                                      for i in range(NA_ROWS)], axis=1) for off in offs], axis=0)
        tiles = tiles + row_mask[:, None, :, None, :]
        parts.append(tiles.reshape(3, n_heads // 2, 2 * NA_ROWS * gw, width))
    return jnp.concatenate(parts, axis=2)


def _na_attention(qkv, bias, *, n_latent):
    n_batch, t, _ = qkv.shape
    rows = n_latent // GRID_W
    n_blocks = rows // NA_ROWS
    n_pairs = NA_HEADS // 2
    tq = NA_ROWS * GRID_W
    tu = NA_UNIT * GRID_W
    n_units = rows // NA_UNIT
    ctx_blk = n_latent // tu

    def slab(col0, sub, u):
        def im(p, b, rb):
            u0 = jnp.clip((rb * NA_SUB + sub) * (NA_ROWS // NA_UNIT) - 1, 0, n_units - 4)
            return (b, u0 + u, col0 + p)
        return pl.BlockSpec((1, tu, LANES), im)

    kv_specs = [slab(col0, sub, u) for sub in range(NA_SUB) for col0 in (n_pairs, 2 * n_pairs) for u in range(4)]
    return pl.pallas_call(
        functools.partial(_na_kernel, n_blocks=n_blocks),
        grid=(n_pairs, n_batch, n_blocks // NA_SUB),
        in_specs=[pl.BlockSpec((1, NA_SUB * tq, LANES), lambda p, b, rb: (b, rb, p))] + kv_specs
                 + [pl.BlockSpec((1, tu, LANES), lambda p, b, rb: (b, ctx_blk, n_pairs + p)),
                    pl.BlockSpec((1, tu, LANES), lambda p, b, rb: (b, ctx_blk, 2 * n_pairs + p)),
                    pl.BlockSpec((3, 1) + bias.shape[2:], lambda p, b, rb: (0, p, 0, 0))],
        out_specs=pl.BlockSpec((1, NA_SUB * tq, LANES), lambda p, b, rb: (b, rb, p)),
        out_shape=jax.ShapeDtypeStruct((n_batch, n_latent, NA_HEADS * NA_HEAD_DIM), BF16),
        compiler_params=_cparams(("arbitrary",) * 3),
        name="neighbourhood_attention",
    )(qkv, *([qkv] * (8 * NA_SUB + 2)), bias)


def _ctx_pair_kernel(q_ref, k_ref, v_ref, o_ref):
    k = k_ref[0]
    v = v_ref[0]
    lane_lo = lax.broadcasted_iota(jnp.int32, (1, LANES), 1) < NA_HEAD_DIM
    o_ref[0] = _pair_softmax_av(q_ref[0], lane_lo, lambda qh, half: [_dot_nt(qh, k)],
                                lambda ps: _dot(ps[0], v)).astype(o_ref.dtype)


def _ctx_pair_attention(qkv, *, n_latent):
    n_batch, t, _ = qkv.shape
    n_ctx = t - n_latent
    blk = n_latent // n_ctx
    n_pairs = NA_HEADS // 2
    spec = lambda off: pl.BlockSpec((1, n_ctx, LANES), lambda b, p: (b, blk, off + p))
    return pl.pallas_call(
        _ctx_pair_kernel,
        grid=(n_batch, n_pairs),
        in_specs=[spec(0), spec(n_pairs), spec(2 * n_pairs)],
        out_specs=pl.BlockSpec((1, n_ctx, LANES), lambda b, p: (b, 0, p)),
        out_shape=jax.ShapeDtypeStruct((n_batch, n_ctx, NA_HEADS * NA_HEAD_DIM), BF16),
        compiler_params=_cparams(("arbitrary", "arbitrary")),
        name="context_pair_attention",
    )(qkv, qkv, qkv)


def _even_weights(w_in, wg2_f, bg_f, wg2_b, bg_b, q_norm_g, kv_norm_g, w_uq, w_ukv):
    sizes = (GLA_HEADS * GLA_DK, GLA_HEADS * GLA_DK, GLA_HEADS * GLA_DV, GLA_HEADS * GLA_DV,
             GLA_RANK, GLA_RANK, MLA_D_CQ, MLA_D_CKV, MLA_D_ROPE)
    q_g, k_g, v_g, r_g, lr_f, lr_b, c_q, c_kv, k_r = jnp.split(w_in, np.cumsum(sizes)[:-1].tolist(), axis=-1)
    swap = np.arange(MLA_D_ROPE) ^ 1
    d = w_in.shape[0]
    zeros = lambda n: jnp.zeros((d, n), w_in.dtype)
    w1 = jnp.concatenate([
        q_g, k_g, v_g, r_g,
        lr_f, lr_b, zeros(LANES - 2 * GLA_RANK), c_q, c_kv,
        k_r, zeros(LANES - MLA_D_ROPE), k_r[:, swap], zeros(LANES - MLA_D_ROPE)], axis=-1).astype(BF16)
    hp = GLA_HEADS * GLA_DK
    wg = jnp.zeros((LANES, 2 * hp), F32)
    wg = wg.at[:GLA_RANK, :hp].set(wg2_f)
    wg = wg.at[GLA_RANK:2 * GLA_RANK, hp:].set(wg2_b)
    bg = jnp.concatenate([bg_f, bg_b])[None]
    uq = w_uq.reshape(MLA_D_CQ, MLA_HEADS, MLA_D_NOPE + MLA_D_ROPE)
    nope, rope = uq[..., :MLA_D_NOPE], uq[..., MLA_D_NOPE:]
    zr = jnp.zeros((MLA_D_CQ, MLA_HEADS, LANES - MLA_D_ROPE), w_uq.dtype)
    wq_main = jnp.concatenate([nope, rope, zr], axis=-1).reshape(MLA_D_CQ, -1).astype(BF16)
    wq_swap = jnp.concatenate([rope[..., swap], zr], axis=-1).reshape(MLA_D_CQ, -1).astype(BF16)
    ukv = w_ukv.reshape(MLA_D_CKV, MLA_HEADS, MLA_D_NOPE + MLA_D_V)
    wkv = jnp.concatenate([ukv[..., :MLA_D_NOPE].reshape(MLA_D_CKV, -1),
                           ukv[..., MLA_D_NOPE:].reshape(MLA_D_CKV, -1)], axis=-1).astype(BF16)
    return (w1, wg.astype(BF16), bg, q_norm_g[None], kv_norm_g[None], wq_main, wq_swap, wkv)


def _rope_tables(n_latent, n_ctx):
    t = jnp.arange(n_latent)
    row = (t // GRID_W).astype(F32)
    col = (t % GRID_W).astype(F32)
    n_freq = MLA_D_ROPE // 4
    inv = ROPE_BASE ** (-jnp.arange(n_freq, dtype=F32) / n_freq)
    ang = jnp.concatenate([row[:, None] * inv, col[:, None] * inv], axis=-1)
    cos = jnp.repeat(jnp.cos(ang), 2, axis=-1)
    sin = jnp.repeat(jnp.sin(ang), 2, axis=-1) * jnp.tile(jnp.array([-1.0, 1.0], F32), MLA_D_ROPE // 2)
    cos = jnp.concatenate([cos, jnp.ones((n_ctx, MLA_D_ROPE), F32)])
    sin = jnp.concatenate([sin, jnp.zeros((n_ctx, MLA_D_ROPE), F32)])
    pad = jnp.zeros((n_latent + n_ctx, LANES - MLA_D_ROPE), F32)
    return jnp.concatenate([cos, pad], axis=-1), jnp.concatenate([sin, pad], axis=-1)


def _row_tile(n_rows, target):
    best = 8
    for cand in range(8, target + 1, 8):
        if n_rows % cand == 0:
            best = cand
    return best


def kernel(x, c, ctx, c_ctx, ada_w, ada_b, ffn1_w_in, ffn1_w_out, ffn2_w_in, ffn2_w_out, even_w_in, gla_wg2_f, gla_bg_f, gla_wg2_b, gla_bg_b, gla_norm_g, mla_q_norm_g, mla_kv_norm_g, mla_w_uq, mla_w_ukv, even_w_out, na_w_in, na_rpb, na_w_out):
    n_batch, n_latent, d = x.shape
    n_ctx = ctx.shape[1]
    t_all = n_latent + n_ctx
    assert n_batch + 1 <= 8 and n_latent % (NA_SUB * NA_ROWS * GRID_W) == 0 and n_ctx == NA_UNIT * GRID_W
    assert n_latent % GLA_CHUNK == 0 and n_ctx % GLA_CHUNK == 0

    xa = jnp.concatenate([x, ctx], axis=1)
    cc = jnp.concatenate([c, c_ctx[None], jnp.zeros((7 - n_batch, d), F32)], axis=0)
    mods = _ada_modulation(cc, ada_w, ada_b)
    mods = mods[:, :n_batch + 1].reshape(DEPTH, n_batch + 1, 9, 1, d).transpose(0, 2, 1, 3, 4)

    tm_plain = _row_tile(t_all, ROWS_PLAIN)
    rope_c, rope_s = _rope_tables(n_latent, n_ctx)
    tq = _row_tile(n_latent, ROWS_QUERY)
    ck = next(cand for cand in KEYS_SUB_BLOCK if t_all % cand == 0)
    ck_ctx = next(cand for cand in KEYS_SUB_BLOCK if n_ctx % cand == 0)

    for l in range(DEPTH):
        last = l == DEPTH - 1
        i = l // 2
        xa = _ffn(xa, mods, l, (0, 1, 2), ffn1_w_in[l].astype(BF16), ffn1_w_out[l].astype(BF16),
                  n_latent=n_latent, n_rows=t_all, tm=tm_plain)
        n_rows = n_latent if last else t_all
        tm_out = _row_tile(n_rows, ROWS_FUSED_LATENT if last else ROWS_FUSED)
        if l % 2 == 0:
            wts = _even_weights(even_w_in[i], gla_wg2_f[i], gla_bg_f[i], gla_wg2_b[i], gla_bg_b[i],
                                mla_q_norm_g[i], mla_kv_norm_g[i], mla_w_uq[i], mla_w_ukv[i])
            gq, gk, gv, r, gf, gb, qm, km, vm = _even_proj(xa, mods, l, (3, 4), wts, rope_c, rope_s,
                                                           n_latent=n_latent, tm=tm_plain)
            o_f, o_b = _gla(gq, gk, gv, gf, gb, n_latent=n_latent, chunk=GLA_CHUNK)
            mla_o = _flash(qm, km, vm, n_heads=MLA_HEADS, dq=2 * LANES, dv=MLA_D_V, tq=tq, tk=t_all, ck=ck,
                           q_blk0=0, n_q=n_latent // tq, k_blk0=0, n_k=1)
            if not last:
                mla_c = _flash(qm, km, vm, n_heads=MLA_HEADS, dq=2 * LANES, dv=MLA_D_V,
                               tq=n_ctx, tk=n_ctx, ck=ck_ctx, q_blk0=n_latent // n_ctx, n_q=1,
                               k_blk0=n_latent // n_ctx, n_k=1)
                mla_o = jnp.concatenate([mla_o, mla_c], axis=1)
            w_out = even_w_out[i].astype(BF16)
            gv_w = GLA_HEADS * GLA_DV
            mixer = dict(mixer="even", mixer_rows=(o_f, o_b, r, mla_o),
                         mixer_consts=(gla_norm_g[i][None], w_out[:gv_w], w_out[gv_w:]))
        else:
            qkv = _qkv_proj(xa, mods, l, (3, 4), na_w_in[i].astype(BF16), n_latent=n_latent, tm=tm_plain)
            bias = _na_bias_tables(na_rpb[i], n_latent // GRID_W)
            att = _na_attention(qkv, bias, n_latent=n_latent)
            if not last:
                att = jnp.concatenate([att, _ctx_pair_attention(qkv, n_latent=n_latent)], axis=1)
            mixer = dict(mixer="odd", mixer_rows=(att,), mixer_consts=(na_w_out[i].astype(BF16),))
        xa = _ffn(xa, mods, l, (6, 7, 8), ffn2_w_in[l].astype(BF16), ffn2_w_out[l].astype(BF16),
                  n_latent=n_latent, n_rows=n_rows, tm=tm_out, mixer_gate=5, **mixer)
    return xa
```

```python
import functools

import numpy as np
import jax
import jax.numpy as jnp
from jax import lax
from jax.experimental import pallas as pl
from jax.experimental.pallas import tpu as pltpu

DEPTH = 4
GRID_W = 64
D_FF = 2816
GLA_HEADS = 4
GLA_DK = 64
GLA_DV = 128
GLA_RANK = 16
GLA_TAU = 16.0
MLA_HEADS = 4
MLA_D_NOPE = 128
MLA_D_ROPE = 64
MLA_D_V = 128
MLA_D_CQ = 384
MLA_D_CKV = 128
MLA_SCALE = (MLA_D_NOPE + MLA_D_ROPE) ** -0.5
LOG2_E = 1.4426950408889634
MLA_QSCALE = MLA_SCALE * LOG2_E
NA_HEADS = 16
NA_HEAD_DIM = 64
NA_KH = 8
NA_KW = 16
ROPE_BASE = 10000.0
EPS = 1e-6
ALPHA = (2 * DEPTH) ** 0.25

LANES = 128
VMEM_LIMIT = 56 * 1024 * 1024

ROWS_PLAIN = 768
ROWS_FUSED = 384
ROWS_FUSED_LATENT = 512
ROWS_QUERY = 1024
KEYS_SUB_BLOCK = (768, 512, 256, 128)
GLA_CHUNK = 256
NA_ROWS = 8
NA_SUB = 8
NA_UNIT = 4
MASK_NEG = -1e30
GLA_FAST_MAX_DECAY = 60.0
GLA_FAST_MAX_KEY = 1e10
NA_COL_GROUPS = ((0, 24, 0), (24, 16, 16), (40, 24, 32))
NA_GROUP_KEYS = 32

BF16 = jnp.bfloat16
F32 = jnp.float32


def _cparams(sem):
    return pltpu.CompilerParams(dimension_semantics=sem, vmem_limit_bytes=VMEM_LIMIT)


def _dot(a, b):
    return jnp.dot(a, b, preferred_element_type=F32)


def _dot_nt(a, b):
    return lax.dot_general(a, b, (((1,), (1,)), ((), ())), preferred_element_type=F32)


def _dot_tn(a, b):
    return lax.dot_general(a, b, (((0,), (0,)), ((), ())), preferred_element_type=F32)


def _silu(v):
    return v * (1.0 / (1.0 + jnp.exp(-v)))


def _row_select(is_ctx, lat_ref, ctx_ref):
    return jnp.where(is_ctx, ctx_ref[0], lat_ref[0])


def _is_ctx_rows(tile_idx, tm, n_latent):
    rows = tile_idx * tm + lax.broadcasted_iota(jnp.int32, (tm, 1), 0)
    return rows >= n_latent


def _modulate(x, is_ctx, sh_l, sh_c, sc_l, sc_c):
    shift = _row_select(is_ctx, sh_l, sh_c)
    scale = _row_select(is_ctx, sc_l, sc_c)
    return x * (1.0 + scale) + shift


def _post_norm(x, y, gate, coef):
    z = ALPHA * x + (coef * gate) * y
    mu = jnp.mean(z, axis=-1, keepdims=True)
    zc = z - mu
    var = jnp.mean(zc * zc, axis=-1, keepdims=True)
    return zc * lax.rsqrt(var + EPS)


def _mod_specs(n_batch, d, layer, k):
    blk = (None, None, 1, 1, d)
    return (pl.BlockSpec(blk, lambda b, i: (layer, k, b, 0, 0)),
            pl.BlockSpec(blk, lambda b, i: (layer, k, n_batch, 0, 0)))


def _ada_kernel(c_ref, w_ref, b_ref, o_ref):
    a = _silu(c_ref[...]).astype(BF16)
    o_ref[0] = _dot(a, w_ref[0].astype(BF16)) + b_ref[0]


def _ada_modulation(cc, ada_w, ada_b):
    depth, d, n9 = ada_w.shape
    tn = n9 // 8
    return pl.pallas_call(
        _ada_kernel,
        grid=(depth, n9 // tn),
        in_specs=[pl.BlockSpec((8, d), lambda l, j: (0, 0)),
                  pl.BlockSpec((1, d, tn), lambda l, j: (l, 0, j)),
                  pl.BlockSpec((1, 1, tn), lambda l, j: (l, 0, j))],
        out_specs=pl.BlockSpec((1, 8, tn), lambda l, j: (l, 0, j)),
        out_shape=jax.ShapeDtypeStruct((depth, 8, n9), F32),
        compiler_params=_cparams(("arbitrary", "arbitrary")),
        name="ada_modulation",
    )(cc, ada_w, ada_b.reshape(depth, 1, n9))


def _gla_gate_norm(o, r, norm_g):
    parts = []
    for hd in range(GLA_HEADS):
        sl = slice(hd * GLA_DV, (hd + 1) * GLA_DV)
        oh = o[:, sl]
        y = oh * lax.rsqrt(jnp.mean(oh * oh, axis=-1, keepdims=True) + EPS) * norm_g
        parts.append((y * _silu(r[:, sl])).astype(BF16))
    return jnp.concatenate(parts, axis=-1)


def _ffn_kernel(*refs, tm, n_latent, mixer):
    x_ref, shl, shc, scl, scc, gl, gc, wi_ref, wo_ref = refs[:9]
    o_ref = refs[-1]
    i = pl.program_id(1)
    is_ctx = _is_ctx_rows(i, tm, n_latent)
    x = x_ref[0]
    if mixer == "even":
        of_ref, ob_ref, r_ref, ng_ref, mo_ref, wa_ref, wb_ref, g2l, g2c = refs[9:-1]
        a = _gla_gate_norm(of_ref[0] + ob_ref[0], r_ref[0], ng_ref[...])
        y_mix = _dot(a, wa_ref[...]) + _dot(mo_ref[0], wb_ref[...])
        x = _post_norm(x, y_mix, _row_select(is_ctx, g2l, g2c), 1.0)
    elif mixer == "odd":
        a_ref, wp_ref, g2l, g2c = refs[9:-1]
        x = _post_norm(x, _dot(a_ref[0], wp_ref[...]), _row_select(is_ctx, g2l, g2c), 1.0)
    h = _modulate(x, is_ctx, shl, shc, scl, scc).astype(BF16)
    d_ff = wo_ref.shape[0]
    gate = _dot(h, wi_ref[:, :d_ff])
    up = _dot(h, wi_ref[:, d_ff:])
    y = _dot((_silu(gate) * up).astype(BF16), wo_ref[...])
    o_ref[0] = _post_norm(x, y, _row_select(is_ctx, gl, gc), 0.5)


def _ffn(x, mods, layer, ks, w_in, w_out, *, n_latent, n_rows, tm, mixer=None, mixer_rows=(), mixer_consts=(),
         mixer_gate=None):
    n_batch, _, d = x.shape
    mod_specs = [spec for k in ks for spec in _mod_specs(n_batch, d, layer, k)]
    gate_specs = list(_mod_specs(n_batch, d, layer, mixer_gate)) if mixer else []
    resident = lambda w: pl.BlockSpec(w.shape, lambda b, i: (0,) * w.ndim, pipeline_mode=pl.Buffered(1))
    row = lambda a: pl.BlockSpec((1, tm, a.shape[-1]), lambda b, i: (b, i, 0))
    if mixer == "even":
        o_f, o_b, r, mla_o = mixer_rows
        norm_g, w_a, w_b = mixer_consts
        extra = [o_f, o_b, r, norm_g, mla_o, w_a, w_b, mods, mods]
        extra_specs = [row(o_f), row(o_b), row(r), resident(norm_g), row(mla_o), resident(w_a), resident(w_b)] + gate_specs
    elif mixer == "odd":
        (att,), (w_p,) = mixer_rows, mixer_consts
        extra = [att, w_p, mods, mods]
        extra_specs = [row(att), resident(w_p)] + gate_specs
    else:
        extra, extra_specs = [], []
    return pl.pallas_call(
        functools.partial(_ffn_kernel, tm=tm, n_latent=n_latent, mixer=mixer),
        grid=(n_batch, n_rows // tm),
        in_specs=[pl.BlockSpec((1, tm, d), lambda b, i: (b, i, 0))] + mod_specs
                 + [resident(w_in), resident(w_out)] + extra_specs,
        out_specs=pl.BlockSpec((1, tm, d), lambda b, i: (b, i, 0)),
        out_shape=jax.ShapeDtypeStruct((n_batch, n_rows, d), F32),
        compiler_params=_cparams(("arbitrary", "arbitrary")),
        name="ffn_postnorm" if mixer is None else "mixer_out_ffn_postnorm",
    )(x, *([mods] * 6), w_in, w_out, *extra)


def _even_proj_kernel(x_ref, shl, shc, scl, scc, w1_ref, wg_ref, bg_ref, qng_ref, kvng_ref,
                      wqm_ref, wqs_ref, wkv_ref, cb_ref, sb_ref,
                      gq_ref, gk_ref, gv_ref, r_ref, gf_ref, gb_ref, qm_ref, km_ref, vm_ref,
                      *, tm, n_latent):
    i = pl.program_id(1)
    is_ctx = _is_ctx_rows(i, tm, n_latent)
    h = _modulate(x_ref[0], is_ctx, shl, shc, scl, scc).astype(BF16)
    p = _dot(h, w1_ref[...])
    hp = GLA_HEADS * GLA_DK
    gv_w = GLA_HEADS * GLA_DV
    o0 = 0
    gq_ref[0] = p[:, o0:o0 + hp] * (GLA_DK ** -0.5)
    o0 += hp
    gk_ref[0] = p[:, o0:o0 + hp]
    o0 += hp
    gv_ref[0] = p[:, o0:o0 + gv_w].astype(BF16)
    o0 += gv_w
    r_ref[0] = p[:, o0:o0 + gv_w]
    o0 += gv_w
    lr = p[:, o0:o0 + LANES].astype(BF16)
    o0 += LANES
    cq = p[:, o0:o0 + MLA_D_CQ]
    o0 += MLA_D_CQ
    ckv = p[:, o0:o0 + MLA_D_CKV]
    o0 += MLA_D_CKV
    kr = p[:, o0:o0 + LANES]
    o0 += LANES
    krs = p[:, o0:o0 + LANES]

    z = _dot(lr, wg_ref[...]) + bg_ref[...]
    logg = (jnp.minimum(z, 0.0) - jnp.log1p(jnp.exp(-jnp.abs(z)))) / GLA_TAU
    gf_ref[0] = logg[:, :hp]
    gb_ref[0] = logg[:, hp:]

    cb = cb_ref[...]
    sb = sb_ref[...]
    cqn = (cq * lax.rsqrt(jnp.mean(cq * cq, axis=-1, keepdims=True) + EPS) * qng_ref[...]).astype(BF16)
    qmain = _dot(cqn, wqm_ref[...])
    qswap = _dot(cqn, wqs_ref[...])
    ckn = (ckv * lax.rsqrt(jnp.mean(ckv * ckv, axis=-1, keepdims=True) + EPS) * kvng_ref[...]).astype(BF16)
    kv = _dot(ckn, wkv_ref[...])
    k_rope = (kr * cb + krs * sb).astype(BF16)
    ones_col = (lax.broadcasted_iota(jnp.int32, (1, LANES), 1) == 0).astype(BF16)
    for hd in range(MLA_HEADS):
        b0 = 2 * LANES * hd
        qm_ref[0, :, b0:b0 + LANES] = (qmain[:, b0:b0 + LANES] * MLA_QSCALE).astype(BF16)
        q_rope = qmain[:, b0 + LANES:b0 + 2 * LANES] * cb + qswap[:, LANES * hd:LANES * (hd + 1)] * sb
        qm_ref[0, :, b0 + LANES:b0 + 2 * LANES] = (q_rope * MLA_QSCALE).astype(BF16)
        km_ref[0, :, b0:b0 + LANES] = kv[:, LANES * hd:LANES * (hd + 1)].astype(BF16)
        km_ref[0, :, b0 + LANES:b0 + 2 * LANES] = k_rope
        v0 = MLA_HEADS * MLA_D_NOPE + MLA_D_V * hd
        vm_ref[0, :, b0:b0 + LANES] = kv[:, v0:v0 + MLA_D_V].astype(BF16)
        vm_ref[0, :, b0 + LANES:b0 + 2 * LANES] = jnp.broadcast_to(ones_col, (tm, LANES))


def _even_proj(x, mods, layer, ks, wts, rope_c, rope_s, *, n_latent, tm):
    n_batch, t, d = x.shape
    mod_specs = [spec for k in ks for spec in _mod_specs(n_batch, d, layer, k)]
    full = lambda a: pl.BlockSpec(a.shape, lambda b, i: (0,) * a.ndim)
    row = lambda w: pl.BlockSpec((1, tm, w), lambda b, i: (b, i, 0))
    hp = GLA_HEADS * GLA_DK
    gv_w = GLA_HEADS * GLA_DV
    out_w = [(hp, F32), (hp, F32), (gv_w, BF16), (gv_w, F32), (hp, F32), (hp, F32),
             (MLA_HEADS * 2 * LANES, BF16), (MLA_HEADS * 2 * LANES, BF16), (MLA_HEADS * 2 * MLA_D_V, BF16)]
    return pl.pallas_call(
        functools.partial(_even_proj_kernel, tm=tm, n_latent=n_latent),
        grid=(n_batch, t // tm),
        in_specs=[row(d)] + mod_specs + [full(w) for w in wts]
                 + [pl.BlockSpec((tm, LANES), lambda b, i: (i, 0))] * 2,
        out_specs=[row(w) for w, _ in out_w],
        out_shape=[jax.ShapeDtypeStruct((n_batch, t, w), dt) for w, dt in out_w],
        compiler_params=_cparams(("arbitrary", "arbitrary")),
        name="even_project",
    )(x, *([mods] * 4), *wts, rope_c, rope_s)


def _gla_constants(chunk, reverse):
    n_lvl = int(np.log2(chunk))
    pos = np.arange(chunk)
    src = pos[None, :]
    tri = src <= pos[:, None]
    cum, mask = [], []
    for lvl in range(n_lvl):
        size = chunk >> lvl
        half = size // 2
        ref = ((pos // size) * size + half - 1)[:, None]
        upper = (pos % size) >= half
        a_up = (src > ref) & (src <= pos[:, None])
        a_lo = (src > pos[:, None]) & (src <= ref)
        cum.append(np.where(upper[:, None], a_up, a_lo))
        same = (pos[:, None] // size) == (pos[None, :] // size)
        mask.append(same & upper[:, None] & ~upper[None, :])
    mask.append(np.any(mask, axis=0) | (np.eye(chunk, dtype=bool) & (not reverse)))
    cum = np.stack([tri, ~tri] + cum).astype(np.float32)
    mask = np.stack(mask).astype(np.float32)
    if reverse:
        cum = cum[:, ::-1, ::-1]
        mask = mask[:, ::-1, ::-1]
    return np.ascontiguousarray(cum).reshape(-1, chunk), np.ascontiguousarray(mask)


class _GlaDirection:
    def __init__(self, q_ref, k_ref, v_ref, g_ref, cum_ref, mask_ref, st_ref, att_ref, o_ref, *, chunk, reverse):
        self.q_ref, self.k_ref, self.v_ref = q_ref, k_ref, v_ref
        self.cum_ref, self.mask_ref, self.st_ref, self.att_ref, self.o_ref = cum_ref, mask_ref, st_ref, att_ref, o_ref
        self.chunk, self.reverse = chunk, reverse
        self.n_lvl = mask_ref.shape[0] - 1
        self.heads_per_group = LANES // GLA_DK
        self.groups = [slice(grp * LANES, (grp + 1) * LANES) for grp in range(GLA_HEADS * GLA_DK // LANES)]
        g = g_ref[0]
        self.g_hi = g.astype(BF16)
        self.g_lo = (g - self.g_hi.astype(F32)).astype(BF16)
        base = self.partial_sums(slice(0, 2 * chunk))
        self.b = base[:chunk]
        self.rest = base[chunk:]
        last = 0 if reverse else chunk - 1
        self.b_last = self.b[last:last + 1, :]
        self.lane_lo = lax.broadcasted_iota(jnp.int32, (1, LANES), 1) < GLA_DK

    def partial_sums(self, rows):
        return _dot(self.cum_ref[rows, :], self.g_hi) + _dot(self.cum_ref[rows, :], self.g_lo)

    def head_only(self, x, hh):
        return jnp.where(self.lane_lo if hh == 0 else jnp.logical_not(self.lane_lo), x, jnp.zeros_like(x))

    def stack_heads(self, x):
        return jnp.concatenate([self.head_only(x, hh) for hh in range(self.heads_per_group)], axis=0)

    def single_reference_ok(self):
        return jnp.logical_and(jnp.max(-self.b_last) <= GLA_FAST_MAX_DECAY,
                               jnp.max(jnp.abs(self.k_ref[0])) <= GLA_FAST_MAX_KEY)

    def scores_single_reference(self):
        for grp, sl in enumerate(self.groups):
            b = self.b[:, sl]
            qe = (self.q_ref[0, :, sl] * jnp.exp(b)).astype(BF16)
            ki = (self.k_ref[0, :, sl] * jnp.exp(-b)).astype(BF16)
            att = jnp.where(self.mask_ref[self.n_lvl] > 0.0, _dot_nt(self.stack_heads(qe), ki), 0.0)
            self.att_ref[grp] = att.astype(BF16)

    def scores_by_level(self):
        chunk, n_lvl = self.chunk, self.n_lvl
        rows = self.heads_per_group * chunk
        sums = self.partial_sums(slice(2 * chunk, (n_lvl + 2) * chunk))
        eye = ((lax.broadcasted_iota(jnp.int32, (rows, chunk), 0) & (chunk - 1))
               == lax.broadcasted_iota(jnp.int32, (rows, chunk), 1)).astype(F32)
        for grp, sl in enumerate(self.groups):
            q = self.q_ref[0, :, sl]
            k = self.k_ref[0, :, sl]
            att = jnp.zeros((rows, chunk), F32)
            for lvl in range(n_lvl):
                w = jnp.exp(sums[lvl * chunk:(lvl + 1) * chunk, sl])
                att = att + _dot_nt(self.stack_heads((q * w).astype(BF16)), (k * w).astype(BF16)) * self.mask_ref[lvl]
            if not self.reverse:
                att = att + _dot_nt(self.stack_heads(q.astype(BF16)), k.astype(BF16)) * eye
            self.att_ref[grp] = att.astype(BF16)

    def outputs_and_state(self):
        chunk = self.chunk
        for grp, sl in enumerate(self.groups):
            q = self.q_ref[0, :, sl]
            k = self.k_ref[0, :, sl]
            qe = (q * jnp.exp(self.b[:, sl])).astype(BF16)
            kd = (k * jnp.exp(self.rest[:, sl])).astype(BF16)
            decay = jnp.exp(self.b_last[:, sl])
            att = self.att_ref[grp]
            for hh in range(self.heads_per_group):
                hd = grp * self.heads_per_group + hh
                vs = slice(hd * GLA_DV, (hd + 1) * GLA_DV)
                v = self.v_ref[0, :, vs]
                st = self.st_ref[hd]
                o = _dot(att[hh * chunk:(hh + 1) * chunk], v) + _dot_nt(self.head_only(qe, hh), st.astype(BF16))
                self.o_ref[0, :, vs] = o
                self.st_ref[hd] = st * decay + _dot_tn(v, kd)


def _gla_kernel(qf, kf, vf, gf, qb, kb, vb, gb, cumf, maskf, cumb, maskb, of_ref, ob_ref, stf, stb, attf, attb,
                *, chunk):
    @pl.when(pl.program_id(1) == 0)
    def _():
        stf[...] = jnp.zeros_like(stf)
        stb[...] = jnp.zeros_like(stb)

    dirs = [_GlaDirection(qf, kf, vf, gf, cumf, maskf, stf, attf, of_ref, chunk=chunk, reverse=False),
            _GlaDirection(qb, kb, vb, gb, cumb, maskb, stb, attb, ob_ref, chunk=chunk, reverse=True)]
    ok = jnp.logical_and(dirs[0].single_reference_ok(), dirs[1].single_reference_ok())

    @pl.when(ok)
    def _():
        for d in dirs:
            d.scores_single_reference()

    @pl.when(jnp.logical_not(ok))
    def _():
        for d in dirs:
            d.scores_by_level()

    for d in dirs:
        d.outputs_and_state()


def _gla(gq, gk, gv, gf, gb, *, n_latent, chunk):
    n_batch, t, hp = gq.shape
    n_lat = n_latent // chunk
    n_ctx = (t - n_latent) // chunk
    n_steps = n_lat + n_ctx

    def fwd(b, s):
        return (b, jnp.where(s < n_ctx, n_lat + s, s - n_ctx), 0)

    def bwd(b, s):
        return (b, n_steps - 1 - s, 0)

    cumf, maskf = _gla_constants(chunk, False)
    cumb, maskb = _gla_constants(chunk, True)
    tile_heads = lambda m: jnp.asarray(np.tile(m, (1, LANES // GLA_DK, 1)))
    consts = [jnp.asarray(cumf, BF16), tile_heads(maskf), jnp.asarray(cumb, BF16), tile_heads(maskb)]
    full = lambda a: pl.BlockSpec(a.shape, lambda b, s: (0,) * a.ndim)
    blk = lambda w, im: pl.BlockSpec((1, chunk, w), im)
    gv_w = gv.shape[-1]
    return pl.pallas_call(
        functools.partial(_gla_kernel, chunk=chunk),
        grid=(n_batch, n_steps),
        in_specs=[blk(hp, fwd), blk(hp, fwd), blk(gv_w, fwd), blk(hp, fwd),
                  blk(hp, bwd), blk(hp, bwd), blk(gv_w, bwd), blk(hp, bwd)] + [full(a) for a in consts],
        out_specs=[blk(gv_w, fwd), blk(gv_w, bwd)],
        out_shape=[jax.ShapeDtypeStruct((n_batch, t, gv_w), F32)] * 2,
        scratch_shapes=[pltpu.VMEM((GLA_HEADS, GLA_DV, LANES), F32)] * 2
                       + [pltpu.VMEM((GLA_HEADS * GLA_DK // LANES, chunk * LANES // GLA_DK, chunk), BF16)] * 2,
        compiler_params=_cparams(("arbitrary", "arbitrary")),
        name="gla_bidirectional",
    )(gq, gk, gv, gf, gq, gk, gv, gb, *consts)


def _flash_kernel(q_ref, k_ref, v_ref, o_ref, m_ref, acc_ref, *, ck, dv):
    j = pl.program_id(3)
    n_sub = k_ref.shape[1] // ck

    @pl.when(j == 0)
    def _():
        m_ref[...] = jnp.full_like(m_ref, -jnp.inf)
        acc_ref[...] = jnp.zeros_like(acc_ref)

    q = q_ref[0]

    for c in range(n_sub):
        off = c * ck
        s = _dot_nt(q, k_ref[0, pl.ds(off, ck), :])
        m_prev = m_ref[...]
        m_new = jnp.maximum(m_prev, jnp.max(s, axis=-1, keepdims=True))
        alpha = jnp.exp2(m_prev - m_new)
        p = jnp.concatenate([jnp.exp2(s[:, u * LANES:(u + 1) * LANES] - m_new)
                             for u in range(ck // LANES)], axis=-1).astype(BF16)
        pv = _dot(p, v_ref[0, pl.ds(off, ck), :])
        acc_ref[...] = jnp.concatenate([alpha] * (acc_ref.shape[1] // LANES), axis=-1) * acc_ref[...] + pv
        m_ref[...] = m_new

    @pl.when(j == pl.num_programs(3) - 1)
    def _():
        acc = acc_ref[...]
        denom = jnp.sum(acc[:, dv:], axis=-1, keepdims=True)
        o_ref[0] = (acc[:, :dv] / denom).astype(o_ref.dtype)


def _flash(q, k, v, *, n_heads, dq, dv, tq, tk, ck, q_blk0, n_q, k_blk0, n_k):
    n_batch = q.shape[0]
    return pl.pallas_call(
        functools.partial(_flash_kernel, ck=ck, dv=dv),
        grid=(n_batch, n_heads, n_q, n_k),
        in_specs=[pl.BlockSpec((1, tq, dq), lambda b, h, i, j: (b, q_blk0 + i, h)),
                  pl.BlockSpec((1, tk, dq), lambda b, h, i, j: (b, k_blk0 + j, h)),
                  pl.BlockSpec((1, tk, 2 * dv), lambda b, h, i, j: (b, k_blk0 + j, h))],
        out_specs=pl.BlockSpec((1, tq, dv), lambda b, h, i, j: (b, i, h)),
        out_shape=jax.ShapeDtypeStruct((n_batch, n_q * tq, n_heads * dv), BF16),
        scratch_shapes=[pltpu.VMEM((tq, LANES), F32), pltpu.VMEM((tq, 2 * dv), F32)],
        compiler_params=_cparams(("arbitrary",) * 4),
        name="flash_attention",
    )(q, k, v)


def _qkv_kernel(x_ref, shl, shc, scl, scc, w_ref, o_ref, *, tm, n_latent, q_width, q_scale):
    i = pl.program_id(1)
    is_ctx = _is_ctx_rows(i, tm, n_latent)
    h = _modulate(x_ref[0], is_ctx, shl, shc, scl, scc).astype(BF16)
    p = _dot(h, w_ref[...])
    o_ref[0, :, :q_width] = (p[:, :q_width] * q_scale).astype(BF16)
    o_ref[0, :, q_width:] = p[:, q_width:].astype(BF16)


def _qkv_proj(x, mods, layer, ks, w, *, n_latent, tm):
    n_batch, t, d = x.shape
    mod_specs = [spec for k in ks for spec in _mod_specs(n_batch, d, layer, k)]
    n_out = w.shape[1]
    return pl.pallas_call(
        functools.partial(_qkv_kernel, tm=tm, n_latent=n_latent, q_width=n_out // 3,
                          q_scale=NA_HEAD_DIM ** -0.5 * LOG2_E),
        grid=(n_batch, t // tm),
        in_specs=[pl.BlockSpec((1, tm, d), lambda b, i: (b, i, 0))] + mod_specs
                 + [pl.BlockSpec(w.shape, lambda b, i: (0, 0))],
        out_specs=pl.BlockSpec((1, tm, n_out), lambda b, i: (b, i, 0)),
        out_shape=jax.ShapeDtypeStruct((n_batch, t, n_out), BF16),
        compiler_params=_cparams(("arbitrary", "arbitrary")),
        name="qkv_project",
    )(x, *([mods] * 4), w)


def _pair_softmax_av(q, lane_lo, scores_fn, av_fn):
    outs = []
    for half in range(2):
        keep = lane_lo if half == 0 else jnp.logical_not(lane_lo)
        qh = jnp.where(keep, q, jnp.zeros_like(q))
        s_list = scores_fn(qh, half)
        m = s_list[0].max(axis=-1, keepdims=True)
        for s in s_list[1:]:
            m = jnp.maximum(m, s.max(axis=-1, keepdims=True))
        p_list = [jnp.exp2(s - m) for s in s_list]
        denom = p_list[0].sum(axis=-1, keepdims=True)
        for p in p_list[1:]:
            denom = denom + p.sum(axis=-1, keepdims=True)
        outs.append(av_fn([p.astype(BF16) for p in p_list]) / denom)
    return jnp.where(lane_lo, outs[0], outs[1])


def _na_block(q, k_win, v_win, k_ctx, v_ctx, bias_ref, kind, with_ones):
    n_keyrows = k_win.shape[0] // GRID_W
    lane_lo = lax.broadcasted_iota(jnp.int32, (1, LANES), 1) < NA_HEAD_DIM
    pieces = [[None] * len(NA_COL_GROUPS) for _ in range(NA_ROWS)]
    row0 = 0
    for g, (c0, gw, kc0) in enumerate(NA_COL_GROUPS):
        mg = NA_ROWS * gw
        qg = jnp.concatenate([q[i * GRID_W + c0:i * GRID_W + c0 + gw] for i in range(NA_ROWS)], axis=0)
        qs = jnp.concatenate([jnp.where(lane_lo, qg, 0.0), jnp.where(lane_lo, 0.0, qg)], axis=0).astype(BF16)
        kg = jnp.concatenate([k_win[t * GRID_W + kc0:t * GRID_W + kc0 + NA_GROUP_KEYS] for t in range(n_keyrows)], axis=0)
        vg = jnp.concatenate([v_win[t * GRID_W + kc0:t * GRID_W + kc0 + NA_GROUP_KEYS] for t in range(n_keyrows)], axis=0)
        s_w = _dot_nt(qs, kg) + bias_ref[kind, 0, pl.ds(row0, 2 * mg), :]
        s_c = _dot_nt(qs, k_ctx)
        m = jnp.maximum(jnp.max(s_w, axis=-1, keepdims=True), jnp.max(s_c, axis=-1, keepdims=True))
        pv = _dot(jnp.exp2(s_w - m).astype(BF16), with_ones(vg)) + _dot(jnp.exp2(s_c - m).astype(BF16), v_ctx)
        o = pv[:, :LANES] / jnp.sum(pv[:, LANES:], axis=-1, keepdims=True)
        og = jnp.where(lane_lo, o[:mg], o[mg:])
        for i in range(NA_ROWS):
            pieces[i][g] = og[i * gw:(i + 1) * gw]
        row0 += 2 * mg
    return jnp.concatenate([pc for row in pieces for pc in row], axis=0)


def _na_kernel(q_ref, *refs, n_blocks):
    kv_refs, (kc_ref, vc_ref, bias_ref, o_ref) = refs[:-4], refs[-4:]
    ones_col = (lax.broadcasted_iota(jnp.int32, (1, LANES), 1) == 0).astype(BF16)

    def with_ones(v):
        return jnp.concatenate([v, jnp.broadcast_to(ones_col, v.shape)], axis=-1)

    k_ctx = kc_ref[0]
    v_ctx = with_ones(vc_ref[0])
    tq = NA_ROWS * GRID_W
    for sub in range(NA_SUB):
        blk = pl.program_id(2) * NA_SUB + sub
        kind = jnp.where(blk == 0, 0, jnp.where(blk == n_blocks - 1, 2, 1))
        k_win = jnp.concatenate([r[0] for r in kv_refs[8 * sub:8 * sub + 4]], axis=0)
        v_win = jnp.concatenate([r[0] for r in kv_refs[8 * sub + 4:8 * sub + 8]], axis=0)
        q = q_ref[0, sub * tq:(sub + 1) * tq, :].astype(F32)
        out = _na_block(q, k_win, v_win, k_ctx, v_ctx, bias_ref, kind, with_ones)
        o_ref[0, sub * tq:(sub + 1) * tq, :] = out.astype(o_ref.dtype)


def _na_bias_tables(rpb, rows):
    n_keyrows = NA_ROWS + 2 * NA_UNIT
    col = np.arange(GRID_W)
    col_start = np.clip(col - NA_KW // 2, 0, GRID_W - NA_KW)
    kc = np.arange(GRID_W)
    col_ok = (kc[None, :] >= col_start[:, None]) & (kc[None, :] < col_start[:, None] + NA_KW)
    rel_col = np.clip(kc[None, :] - col[:, None] + (NA_KW - 1), 0, 2 * NA_KW - 2)
    col_sel = ((rel_col[:, :, None] == np.arange(2 * NA_KW - 1)) & col_ok[:, :, None]).astype(np.float32)
    n_heads = rpb.shape[0]
    by_col = jnp.einsum('hab,ckb->hack', rpb, jnp.asarray(col_sel), precision=lax.Precision.HIGHEST)
    by_col = jnp.where(jnp.asarray(col_ok)[None, None], by_col * LOG2_E, MASK_NEG)
    by_col = jnp.pad(by_col, ((0, 0), (NA_ROWS, n_keyrows - NA_ROWS), (0, 0), (0, 0)), constant_values=MASK_NEG)
    n_blocks = rows // NA_ROWS
    offs, row_mask = [], []
    for blk in (0, min(1, n_blocks - 1), n_blocks - 1):
        r = blk * NA_ROWS + np.arange(NA_ROWS)
        kh = min(NA_KH, rows)
        r_start = np.clip(r - kh // 2, 0, rows - kh)
        key_row0 = int(np.clip(blk * NA_ROWS - NA_UNIT, 0, rows - n_keyrows))
        kr = key_row0 + np.arange(n_keyrows)
        row_ok = (kr[None, :] >= r_start[:, None]) & (kr[None, :] < r_start[:, None] + kh)
        offs.append(key_row0 - blk * NA_ROWS + NA_KH - 1)
        row_mask.append(np.repeat(np.where(row_ok, 0.0, MASK_NEG), NA_GROUP_KEYS, axis=1))
    row_mask = jnp.asarray(np.stack(row_mask), F32)
    width = n_keyrows * NA_GROUP_KEYS
    parts = []
    for c0, gw, kc0 in NA_COL_GROUPS:
        strip = by_col[:, :, c0:c0 + gw, kc0:kc0 + NA_GROUP_KEYS]
        strip = strip.transpose(0, 2, 1, 3).reshape(n_heads, gw, -1)
        tiles = jnp.stack([jnp.stack([strip[:, :, (NA_ROWS + off - i) * NA_GROUP_KEYS:][:, :, :width]
                                      for i in range(NA_ROWS)], axis=1) for off in offs], axis=0)
        tiles = tiles + row_mask[:, None, :, None, :]
        parts.append(tiles.reshape(3, n_heads // 2, 2 * NA_ROWS * gw, width))
    return jnp.concatenate(parts, axis=2)


def _na_attention(qkv, bias, *, n_latent):
    n_batch, t, _ = qkv.shape
    rows = n_latent // GRID_W
    n_blocks = rows // NA_ROWS
    n_pairs = NA_HEADS // 2
    tq = NA_ROWS * GRID_W
    tu = NA_UNIT * GRID_W
    n_units = rows // NA_UNIT
    ctx_blk = n_latent // tu

    def slab(col0, sub, u):
        def im(p, b, rb):
            u0 = jnp.clip((rb * NA_SUB + sub) * (NA_ROWS // NA_UNIT) - 1, 0, n_units - 4)
            return (b, u0 + u, col0 + p)
        return pl.BlockSpec((1, tu, LANES), im)

    kv_specs = [slab(col0, sub, u) for sub in range(NA_SUB) for col0 in (n_pairs, 2 * n_pairs) for u in range(4)]
    return pl.pallas_call(
        functools.partial(_na_kernel, n_blocks=n_blocks),
        grid=(n_pairs, n_batch, n_blocks // NA_SUB),
        in_specs=[pl.BlockSpec((1, NA_SUB * tq, LANES), lambda p, b, rb: (b, rb, p))] + kv_specs
                 + [pl.BlockSpec((1, tu, LANES), lambda p, b, rb: (b, ctx_blk, n_pairs + p)),
                    pl.BlockSpec((1, tu, LANES), lambda p, b, rb: (b, ctx_blk, 2 * n_pairs + p)),
                    pl.BlockSpec((3, 1) + bias.shape[2:], lambda p, b, rb: (0, p, 0, 0))],
        out_specs=pl.BlockSpec((1, NA_SUB * tq, LANES), lambda p, b, rb: (b, rb, p)),
        out_shape=jax.ShapeDtypeStruct((n_batch, n_latent, NA_HEADS * NA_HEAD_DIM), BF16),
        compiler_params=_cparams(("arbitrary",) * 3),
        name="neighbourhood_attention",
    )(qkv, *([qkv] * (8 * NA_SUB + 2)), bias)


def _ctx_pair_kernel(q_ref, k_ref, v_ref, o_ref):
    k = k_ref[0]
    v = v_ref[0]
    lane_lo = lax.broadcasted_iota(jnp.int32, (1, LANES), 1) < NA_HEAD_DIM
    o_ref[0] = _pair_softmax_av(q_ref[0], lane_lo, lambda qh, half: [_dot_nt(qh, k)],
                                lambda ps: _dot(ps[0], v)).astype(o_ref.dtype)


def _ctx_pair_attention(qkv, *, n_latent):
    n_batch, t, _ = qkv.shape
    n_ctx = t - n_latent
    blk = n_latent // n_ctx
    n_pairs = NA_HEADS // 2
    spec = lambda off: pl.BlockSpec((1, n_ctx, LANES), lambda b, p: (b, blk, off + p))
    return pl.pallas_call(
        _ctx_pair_kernel,
        grid=(n_batch, n_pairs),
        in_specs=[spec(0), spec(n_pairs), spec(2 * n_pairs)],
        out_specs=pl.BlockSpec((1, n_ctx, LANES), lambda b, p: (b, 0, p)),
        out_shape=jax.ShapeDtypeStruct((n_batch, n_ctx, NA_HEADS * NA_HEAD_DIM), BF16),
        compiler_params=_cparams(("arbitrary", "arbitrary")),
        name="context_pair_attention",
    )(qkv, qkv, qkv)


def _even_weights(w_in, wg2_f, bg_f, wg2_b, bg_b, q_norm_g, kv_norm_g, w_uq, w_ukv):
    sizes = (GLA_HEADS * GLA_DK, GLA_HEADS * GLA_DK, GLA_HEADS * GLA_DV, GLA_HEADS * GLA_DV,
             GLA_RANK, GLA_RANK, MLA_D_CQ, MLA_D_CKV, MLA_D_ROPE)
    q_g, k_g, v_g, r_g, lr_f, lr_b, c_q, c_kv, k_r = jnp.split(w_in, np.cumsum(sizes)[:-1].tolist(), axis=-1)
    swap = np.arange(MLA_D_ROPE) ^ 1
    d = w_in.shape[0]
    zeros = lambda n: jnp.zeros((d, n), w_in.dtype)
    w1 = jnp.concatenate([
        q_g, k_g, v_g, r_g,
        lr_f, lr_b, zeros(LANES - 2 * GLA_RANK), c_q, c_kv,
        k_r, zeros(LANES - MLA_D_ROPE), k_r[:, swap], zeros(LANES - MLA_D_ROPE)], axis=-1).astype(BF16)
    hp = GLA_HEADS * GLA_DK
    wg = jnp.zeros((LANES, 2 * hp), F32)
    wg = wg.at[:GLA_RANK, :hp].set(wg2_f)
    wg = wg.at[GLA_RANK:2 * GLA_RANK, hp:].set(wg2_b)
    bg = jnp.concatenate([bg_f, bg_b])[None]
    uq = w_uq.reshape(MLA_D_CQ, MLA_HEADS, MLA_D_NOPE + MLA_D_ROPE)
    nope, rope = uq[..., :MLA_D_NOPE], uq[..., MLA_D_NOPE:]
    zr = jnp.zeros((MLA_D_CQ, MLA_HEADS, LANES - MLA_D_ROPE), w_uq.dtype)
    wq_main = jnp.concatenate([nope, rope, zr], axis=-1).reshape(MLA_D_CQ, -1).astype(BF16)
    wq_swap = jnp.concatenate([rope[..., swap], zr], axis=-1).reshape(MLA_D_CQ, -1).astype(BF16)
    ukv = w_ukv.reshape(MLA_D_CKV, MLA_HEADS, MLA_D_NOPE + MLA_D_V)
    wkv = jnp.concatenate([ukv[..., :MLA_D_NOPE].reshape(MLA_D_CKV, -1),
                           ukv[..., MLA_D_NOPE:].reshape(MLA_D_CKV, -1)], axis=-1).astype(BF16)
    return (w1, wg.astype(BF16), bg, q_norm_g[None], kv_norm_g[None], wq_main, wq_swap, wkv)


def _rope_tables(n_latent, n_ctx):
    t = jnp.arange(n_latent)
    row = (t // GRID_W).astype(F32)
    col = (t % GRID_W).astype(F32)
    n_freq = MLA_D_ROPE // 4
    inv = ROPE_BASE ** (-jnp.arange(n_freq, dtype=F32) / n_freq)
    ang = jnp.concatenate([row[:, None] * inv, col[:, None] * inv], axis=-1)
    cos = jnp.repeat(jnp.cos(ang), 2, axis=-1)
    sin = jnp.repeat(jnp.sin(ang), 2, axis=-1) * jnp.tile(jnp.array([-1.0, 1.0], F32), MLA_D_ROPE // 2)
    cos = jnp.concatenate([cos, jnp.ones((n_ctx, MLA_D_ROPE), F32)])
    sin = jnp.concatenate([sin, jnp.zeros((n_ctx, MLA_D_ROPE), F32)])
    pad = jnp.zeros((n_latent + n_ctx, LANES - MLA_D_ROPE), F32)
    return jnp.concatenate([cos, pad], axis=-1), jnp.concatenate([sin, pad], axis=-1)


def _row_tile(n_rows, target):
    best = 8
    for cand in range(8, target + 1, 8):
        if n_rows % cand == 0:
            best = cand
    return best


def kernel(x, c, ctx, c_ctx, ada_w, ada_b, ffn1_w_in, ffn1_w_out, ffn2_w_in, ffn2_w_out, even_w_in, gla_wg2_f, gla_bg_f, gla_wg2_b, gla_bg_b, gla_norm_g, mla_q_norm_g, mla_kv_norm_g, mla_w_uq, mla_w_ukv, even_w_out, na_w_in, na_rpb, na_w_out):
    n_batch, n_latent, d = x.shape
    n_ctx = ctx.shape[1]
    t_all = n_latent + n_ctx
    assert n_batch + 1 <= 8 and n_latent % (NA_SUB * NA_ROWS * GRID_W) == 0 and n_ctx == NA_UNIT * GRID_W
    assert n_latent % GLA_CHUNK == 0 and n_ctx % GLA_CHUNK == 0

    xa = jnp.concatenate([x, ctx], axis=1)
    cc = jnp.concatenate([c, c_ctx[None], jnp.zeros((7 - n_batch, d), F32)], axis=0)
    mods = _ada_modulation(cc, ada_w, ada_b)
    mods = mods[:, :n_batch + 1].reshape(DEPTH, n_batch + 1, 9, 1, d).transpose(0, 2, 1, 3, 4)

    tm_plain = _row_tile(t_all, ROWS_PLAIN)
    rope_c, rope_s = _rope_tables(n_latent, n_ctx)
    tq = _row_tile(n_latent, ROWS_QUERY)
    ck = next(cand for cand in KEYS_SUB_BLOCK if t_all % cand == 0)
    ck_ctx = next(cand for cand in KEYS_SUB_BLOCK if n_ctx % cand == 0)

    for l in range(DEPTH):
        last = l == DEPTH - 1
        i = l // 2
        xa = _ffn(xa, mods, l, (0, 1, 2), ffn1_w_in[l].astype(BF16), ffn1_w_out[l].astype(BF16),
                  n_latent=n_latent, n_rows=t_all, tm=tm_plain)
        n_rows = n_latent if last else t_all
        tm_out = _row_tile(n_rows, ROWS_FUSED_LATENT if last else ROWS_FUSED)
        if l % 2 == 0:
            wts = _even_weights(even_w_in[i], gla_wg2_f[i], gla_bg_f[i], gla_wg2_b[i], gla_bg_b[i],
                                mla_q_norm_g[i], mla_kv_norm_g[i], mla_w_uq[i], mla_w_ukv[i])
            gq, gk, gv, r, gf, gb, qm, km, vm = _even_proj(xa, mods, l, (3, 4), wts, rope_c, rope_s,
                                                           n_latent=n_latent, tm=tm_plain)
            o_f, o_b = _gla(gq, gk, gv, gf, gb, n_latent=n_latent, chunk=GLA_CHUNK)
            mla_o = _flash(qm, km, vm, n_heads=MLA_HEADS, dq=2 * LANES, dv=MLA_D_V, tq=tq, tk=t_all, ck=ck,
                           q_blk0=0, n_q=n_latent // tq, k_blk0=0, n_k=1)
            if not last:
                mla_c = _flash(qm, km, vm, n_heads=MLA_HEADS, dq=2 * LANES, dv=MLA_D_V,
                               tq=n_ctx, tk=n_ctx, ck=ck_ctx, q_blk0=n_latent // n_ctx, n_q=1,
                               k_blk0=n_latent // n_ctx, n_k=1)
                mla_o = jnp.concatenate([mla_o, mla_c], axis=1)
            w_out = even_w_out[i].astype(BF16)
            gv_w = GLA_HEADS * GLA_DV
            mixer = dict(mixer="even", mixer_rows=(o_f, o_b, r, mla_o),
                         mixer_consts=(gla_norm_g[i][None], w_out[:gv_w], w_out[gv_w:]))
        else:
            qkv = _qkv_proj(xa, mods, l, (3, 4), na_w_in[i].astype(BF16), n_latent=n_latent, tm=tm_plain)
            bias = _na_bias_tables(na_rpb[i], n_latent // GRID_W)
            att = _na_attention(qkv, bias, n_latent=n_latent)
            if not last:
                att = jnp.concatenate([att, _ctx_pair_attention(qkv, n_latent=n_latent)], axis=1)
            mixer = dict(mixer="odd", mixer_rows=(att,), mixer_consts=(na_w_out[i].astype(BF16),))
        xa = _ffn(xa, mods, l, (6, 7, 8), ffn2_w_in[l].astype(BF16), ffn2_w_out[l].astype(BF16),
                  n_latent=n_latent, n_rows=n_rows, tm=tm_out, mixer_gate=5, **mixer)
    return xa
```
